```python
import math
import jax, jax.numpy as jnp
from jax import lax
import numpy as np

D_MODEL = 1024
BATCH = 8
SEQ = 2048
DEPTH = 4
DEC_BATCH = 32
DEC_SEQ = 4
PAST_LEN = 8192
PAGE_SIZE = 128

N_AB_LAYERS = (DEPTH + 1) // 2
N_C_LAYERS = DEPTH // 2
EPS = 1e-6

A_WIDTH = D_MODEL // 2
SSM_GROUP = 16
SSM_GROUPS = A_WIDTH // SSM_GROUP
SSM_STATE = 64
DT_MIN = 1e-3
DT_MAX = 1e-1

HEAD_DIM = 64
B_WIDTH = D_MODEL // 2
N_HEADS = B_WIDTH // HEAD_DIM
N_KV_HEADS = 2
IDX_HEADS = 8
IDX_DIM = 64
TOPK_MAX = 256
ROPE_THETA = 500000.0
ROT_FRAC = 4
Q_BLOCK = 128

OFF_Q = A_WIDTH
OFF_K = OFF_Q + N_HEADS * HEAD_DIM
OFF_V = OFF_K + N_KV_HEADS * HEAD_DIM
OFF_QI = OFF_V + N_KV_HEADS * HEAD_DIM
OFF_KI = OFF_QI + IDX_HEADS * IDX_DIM
OFF_WI = OFF_KI + IDX_DIM
IN_COLS = OFF_WI + IDX_HEADS
IN_SPLITS = [OFF_Q, OFF_K, OFF_V, OFF_QI, OFF_KI, OFF_WI]

POOL_WINDOWS = (2, 4, 8, 16)
POOL_GROUPS = 4
POOL_GROUP_DIM = D_MODEL // POOL_GROUPS
POOL_STATE = max(POOL_WINDOWS) - 1

MOE_GROUPS = 4
EXPERTS_PER_GROUP = 4
N_EXPERTS = MOE_GROUPS * EXPERTS_PER_GROUP
MOE_TOPK = 2
D_FF_EXPERT = 256

kernel_name = 'hybrid_s5_dsa_pool_hmoe_step'


def rms_norm(x, gain):
    xf = x.astype(jnp.float32)
    var = jnp.mean(xf * xf, axis=-1, keepdims=True)
    return (xf * lax.rsqrt(var + EPS)).astype(x.dtype) * gain


def ada_params(c, w_ada, b_ada):
    m = jax.nn.silu(c) @ w_ada + b_ada
    return jnp.split(m[:, None, :], 6, axis=-1)


def rotary(x, pos):
    d = x.shape[-1]
    rd = d // ROT_FRAC
    half = rd // 2
    inv = ROPE_THETA ** (-jnp.arange(half, dtype=jnp.float32) / half)
    ang = pos.astype(jnp.float32)[:, None] * inv[None, :]
    cos = jnp.cos(ang)[None, :, None, :]
    sin = jnp.sin(ang)[None, :, None, :]
    xr = x[..., :rd].astype(jnp.float32)
    x1, x2 = xr[..., :half], xr[..., half:]
    rot = jnp.concatenate([x1 * cos - x2 * sin, x2 * cos + x1 * sin], axis=-1)
    return jnp.concatenate([rot.astype(x.dtype), x[..., rd:]], axis=-1)


def cmul(ar, ai, br, bi):
    return ar * br - ai * bi, ar * bi + ai * br


def s5_mixer(u, x0_re, x0_im, lam_re, lam_im, log_dt, b_re, b_im, c_re, c_im, d_skip, w_glu, b_glu):
    bsz, t, _ = u.shape
    uf = u.astype(jnp.float32).reshape(bsz, t, SSM_GROUPS, SSM_GROUP)
    dt = jnp.exp(log_dt.astype(jnp.float32))[:, None]
    lr = lam_re.astype(jnp.float32)
    li = lam_im.astype(jnp.float32)
    mag = jnp.exp(lr * dt)
    abar_re, abar_im = mag * jnp.cos(li * dt), mag * jnp.sin(li * dt)
    den = lr * lr + li * li
    coef_re, coef_im = cmul(abar_re - 1.0, abar_im, lr / den, -li / den)
    bb_re, bb_im = cmul(coef_re[..., None], coef_im[..., None],
                        b_re.astype(jnp.float32), b_im.astype(jnp.float32))
    bu_re = jnp.einsum('btgn,gpn->btgp', uf, bb_re)
    bu_im = jnp.einsum('btgn,gpn->btgp', uf, bb_im)
    s_re, s_im = cmul(abar_re, abar_im, x0_re.astype(jnp.float32), x0_im.astype(jnp.float32))
    bu_re = bu_re.at[:, 0].add(s_re)
    bu_im = bu_im.at[:, 0].add(s_im)
    a_re = jnp.broadcast_to(abar_re, bu_re.shape)
    a_im = jnp.broadcast_to(abar_im, bu_im.shape)

    def combine(e1, e2):
        a1r, a1i, b1r, b1i = e1
        a2r, a2i, b2r, b2i = e2
        ar, ai = cmul(a2r, a2i, a1r, a1i)
        br, bi = cmul(a2r, a2i, b1r, b1i)
        return ar, ai, br + b2r, bi + b2i

    _, _, xs_re, xs_im = lax.associative_scan(combine, (a_re, a_im, bu_re, bu_im), axis=1)
    y = (jnp.einsum('btgp,gnp->btgn', xs_re, c_re.astype(jnp.float32))
         - jnp.einsum('btgp,gnp->btgn', xs_im, c_im.astype(jnp.float32)))
    y = y + d_skip.astype(jnp.float32).reshape(SSM_GROUPS, SSM_GROUP) * uf
    z = jax.nn.gelu(y.reshape(bsz, t, A_WIDTH))
    out = z * jax.nn.sigmoid(z @ w_glu.astype(jnp.float32) + b_glu.astype(jnp.float32))
    return out.astype(u.dtype), xs_re[:, -1].astype(x0_re.dtype), xs_im[:, -1].astype(x0_im.dtype)


def gather_rows(rows, idx):
    return jax.vmap(lambda r, ix: r[ix])(rows, idx)


def gather_paged(pool, page_table, new_rows, idx):
    past_len = page_table.shape[1] * PAGE_SIZE
    t_new = new_rows.shape[1]
    in_past = idx < past_len
    p_idx = jnp.minimum(idx, past_len - 1)
    phys = jax.vmap(lambda pt, ix: pt[ix])(page_table, p_idx // PAGE_SIZE)
    flat = pool.reshape((-1,) + pool.shape[2:])
    past_sel = flat[phys * PAGE_SIZE + p_idx % PAGE_SIZE]
    new_sel = gather_rows(new_rows, jnp.clip(idx - past_len, 0, t_new - 1))
    mask = in_past.reshape(in_past.shape + (1,) * (past_sel.ndim - in_past.ndim))
    return jnp.where(mask, past_sel, new_sel.astype(past_sel.dtype))


def indexer_scores(qi, wi, ki):
    dots = jnp.einsum('bthd,bsd->bths', qi.astype(jnp.float32), ki.astype(jnp.float32)) * IDX_DIM ** -0.5
    return jnp.einsum('bths,bth->bts', jax.nn.relu(dots), wi.astype(jnp.float32))


def sparse_attend(q, k_sel, v_sel, valid):
    b, t, _, hd = q.shape
    qg = q.reshape(b, t, N_KV_HEADS, N_HEADS // N_KV_HEADS, hd).astype(jnp.float32)
    s = jnp.einsum('btkgd,btnkd->btkgn', qg, k_sel.astype(jnp.float32)) * hd ** -0.5
    s = jnp.where(valid[:, :, None, None, :], s, -jnp.inf)
    pr = jax.nn.softmax(s, axis=-1)
    o = jnp.einsum('btkgn,btnkd->btkgd', pr, v_sel.astype(jnp.float32))
    return o.reshape(b, t, N_HEADS * hd).astype(q.dtype)


def dsa_queries(q, qi, wi, q_pos, ki_all, n_keep, gather_kv):
    b, t = q.shape[:2]
    k_pos = jnp.arange(ki_all.shape[1], dtype=jnp.int32)

    def block(args):
        qb, qib, wib, pb = args
        sc = indexer_scores(qib, wib, ki_all)
        sc = jnp.where(k_pos[None, None, :] <= pb[None, :, None], sc, -jnp.inf)
        _, idx = lax.top_k(sc, n_keep)
        k_sel, v_sel = gather_kv(idx)
        return sparse_attend(qb, k_sel, v_sel, idx <= pb[None, :, None])

    if t > Q_BLOCK and t % Q_BLOCK == 0:
        nb = t // Q_BLOCK
        split = lambda a: jnp.swapaxes(a.reshape((b, nb, Q_BLOCK) + a.shape[2:]), 0, 1)
        out = lax.map(block, (split(q), split(qi), split(wi), q_pos.reshape(nb, Q_BLOCK)))
        return jnp.swapaxes(out, 0, 1).reshape(b, t, -1)
    return block((q, qi, wi, q_pos))


def attend_prompt(q, k, v, qi, ki, wi, pos):
    n_keep = min(TOPK_MAX, k.shape[1] // 4)
    gather_kv = lambda idx: (gather_rows(k, idx), gather_rows(v, idx))
    return dsa_queries(q, qi, wi, pos, ki, n_keep, gather_kv)


def make_sample_attend(pool_k, pool_v, pool_ki, page_table):
    def attend(q, k, v, qi, ki, wi, pos):
        b, t = q.shape[:2]
        past_len = page_table.shape[1] * PAGE_SIZE
        n_keep = min(TOPK_MAX, (past_len + t) // 4)
        past_ki = pool_ki[page_table].reshape(b, past_len, IDX_DIM)
        ki_all = jnp.concatenate([past_ki, ki.astype(past_ki.dtype)], axis=1)
        gather_kv = lambda idx: (gather_paged(pool_k, page_table, k, idx),
                                 gather_paged(pool_v, page_table, v, idx))
        return dsa_queries(q, qi, wi, pos, ki_all, n_keep, gather_kv)
    return attend


def ab_mixer(h, pos, ssm_re, ssm_im, attend, p, i):
    b, t, _ = h.shape
    u, q, k, v, qi, ki, wi = jnp.split(h @ p['w_in'][i], IN_SPLITS, axis=-1)
    q = rotary(rms_norm(q.reshape(b, t, N_HEADS, HEAD_DIM), p['q_gain'][i]), pos)
    k = rotary(rms_norm(k.reshape(b, t, N_KV_HEADS, HEAD_DIM), p['k_gain'][i]), pos)
    v = v.reshape(b, t, N_KV_HEADS, HEAD_DIM)
    qi = rotary(qi.reshape(b, t, IDX_HEADS, IDX_DIM), pos)
    ki = rotary(ki.reshape(b, t, 1, IDX_DIM), pos)[:, :, 0]
    wi = wi * IDX_HEADS ** -0.5
    y_a, s_re, s_im = s5_mixer(u, ssm_re, ssm_im, p['lam_re'][i], p['lam_im'][i], p['log_dt'][i],
                               p['ssm_b_re'][i], p['ssm_b_im'][i], p['ssm_c_re'][i], p['ssm_c_im'][i],
                               p['ssm_d'][i], p['w_glu'][i], p['b_glu'][i])
    y_b = attend(q, k, v, qi, ki, wi, pos)
    y = jnp.concatenate([y_a, y_b.astype(y_a.dtype)], axis=-1) @ p['w_out'][i]
    return y, k, v, ki, s_re, s_im


def pool_mixer(h, buf, pos, w_pool, pool_scale):
    b, t, _ = h.shape
    hf = h.astype(jnp.float32)
    n_prev = 0 if buf is None else buf.shape[1]
    ext = hf if buf is None else jnp.concatenate([buf.astype(jnp.float32), hf], axis=1)
    cs = jnp.cumsum(ext, axis=1)
    cs = jnp.concatenate([jnp.zeros_like(cs[:, :1]), cs], axis=1)
    end = n_prev + jnp.arange(t, dtype=jnp.int32) + 1
    means = []
    for g, w in enumerate(POOL_WINDOWS):
        csg = cs[..., g * POOL_GROUP_DIM:(g + 1) * POOL_GROUP_DIM]
        cnt = jnp.minimum(pos + 1, w)
        means.append((csg[:, end] - csg[:, end - cnt]) / cnt[None, :, None].astype(jnp.float32))
    pooled = (jnp.concatenate(means, axis=-1) - hf).reshape(b, t, POOL_GROUPS, POOL_GROUP_DIM)
    y = jnp.einsum('btgc,gce->btge', pooled, w_pool.astype(jnp.float32)).reshape(b, t, D_MODEL)
    y = y * pool_scale.astype(jnp.float32)
    return y.astype(h.dtype), ext[:, -POOL_STATE:].astype(h.dtype)


def hier_moe(h, w_coarse, b_coarse, w_fine, b_fine, w_gate, w_up, w_down):
    b, t, d = h.shape
    hf = h.reshape(-1, d)
    pc = jax.nn.softmax((hf @ w_coarse).astype(jnp.float32) + b_coarse, axis=-1)
    pg, grp = lax.top_k(pc, 1)
    fl = ((hf @ w_fine).astype(jnp.float32) + b_fine).reshape(-1, MOE_GROUPS, EXPERTS_PER_GROUP)
    fl_sel = jnp.take_along_axis(fl, grp[:, :, None], axis=1)[:, 0]
    fv, fi = lax.top_k(fl_sel, MOE_TOPK)
    fw = jax.nn.softmax(fv, axis=-1) * pg
    eid = grp * EXPERTS_PER_GROUP + fi
    gates = jnp.sum(jax.nn.one_hot(eid, N_EXPERTS, dtype=jnp.float32) * fw[..., None], axis=1)
    hid = jax.nn.silu(jnp.einsum('nd,edf->nef', hf, w_gate)) * jnp.einsum('nd,edf->nef', hf, w_up)
    hid = hid * gates.astype(hid.dtype)[..., None]
    return jnp.einsum('nef,efd->nd', hid, w_down).reshape(b, t, d)


def run_trunk(x, c, pos, ssm_re, ssm_im, pool_buf, attend_fns, p):
    ks, vs, kis, sres, sims, pools = [], [], [], [], [], []
    for layer in range(DEPTH):
        i = layer // 2
        sh1, sc1, g1, sh2, sc2, g2 = ada_params(c, p['w_ada'][layer], p['b_ada'][layer])
        h = rms_norm(x, p['g_mix'][layer]) * (1 + sc1) + sh1
        if layer % 2 == 0:
            y, k, v, ki, s_re, s_im = ab_mixer(h, pos, ssm_re[i], ssm_im[i], attend_fns[i], p, i)
            ks.append(k)
            vs.append(v)
            kis.append(ki)
            sres.append(s_re)
            sims.append(s_im)
        else:
            buf = None if pool_buf is None else pool_buf[i]
            y, nb = pool_mixer(h, buf, pos, p['w_pool'][i], p['pool_scale'][i])
            pools.append(nb)
        x = x + g1 * y
        h = rms_norm(x, p['g_ffn'][layer]) * (1 + sc2) + sh2
        x = x + g2 * hier_moe(h, p['w_coarse'][layer], p['b_coarse'][layer], p['w_fine'][layer],
                              p['b_fine'][layer], p['w_gate'][layer], p['w_up'][layer], p['w_down'][layer])
    return x, jnp.stack(ks), jnp.stack(vs), jnp.stack(kis), jnp.stack(sres), jnp.stack(sims), jnp.stack(pools)


def setup_inputs(seed: int = 0) -> dict:
    key = jax.random.key(seed)
    it = iter(list(jax.random.split(key, 48)))
    nrm = lambda shape, scale: jax.random.normal(next(it), shape, jnp.float32) * scale
    n_pages = PAST_LEN // PAGE_SIZE
    n_phys = (DEC_BATCH * n_pages * 5) // 4
    D = D_MODEL
    inp = {}
    inp['x_prompt'] = nrm((BATCH, SEQ, D), 1.0)
    inp['x_sample'] = nrm((DEC_BATCH, DEC_SEQ, D), 1.0)
    inp['c_prompt'] = nrm((BATCH, D), 1.0)
    inp['c_sample'] = nrm((DEC_BATCH, D), 1.0)
    inp['cache_k'] = nrm((N_AB_LAYERS, n_phys, PAGE_SIZE, N_KV_HEADS, HEAD_DIM), 1.0)
    inp['cache_v'] = nrm((N_AB_LAYERS, n_phys, PAGE_SIZE, N_KV_HEADS, HEAD_DIM), 1.0)
    inp['cache_kidx'] = nrm((N_AB_LAYERS, n_phys, PAGE_SIZE, IDX_DIM), 1.0)
    inp['state_ssm_re'] = nrm((N_AB_LAYERS, DEC_BATCH, SSM_GROUPS, SSM_STATE), 0.1)
    inp['state_ssm_im'] = nrm((N_AB_LAYERS, DEC_BATCH, SSM_GROUPS, SSM_STATE), 0.1)
    inp['state_pool'] = nrm((N_C_LAYERS, DEC_BATCH, POOL_STATE, D), 1.0)
    perm = jax.random.permutation(next(it), n_phys)
    inp['page_table'] = perm[:DEC_BATCH * n_pages].reshape(DEC_BATCH, n_pages).astype(jnp.int32)
    inp['w_ada'] = nrm((DEPTH, D, 6 * D), 0.5 * D ** -0.5)
    inp['b_ada'] = nrm((DEPTH, 6 * D), 0.02)
    inp['g_mix'] = 1.0 + nrm((DEPTH, D), 0.02)
    inp['g_ffn'] = 1.0 + nrm((DEPTH, D), 0.02)
    inp['w_in'] = nrm((N_AB_LAYERS, D, IN_COLS), D ** -0.5)
    inp['q_gain'] = 1.0 + nrm((N_AB_LAYERS, HEAD_DIM), 0.02)
    inp['k_gain'] = 1.0 + nrm((N_AB_LAYERS, HEAD_DIM), 0.02)
    inp['lam_re'] = -0.5 + nrm((N_AB_LAYERS, SSM_GROUPS, SSM_STATE), 0.01)
    inp['lam_im'] = (math.pi * jnp.arange(SSM_STATE, dtype=jnp.float32)
                     + nrm((N_AB_LAYERS, SSM_GROUPS, SSM_STATE), 0.01))
    inp['log_dt'] = jax.random.uniform(next(it), (N_AB_LAYERS, SSM_GROUPS), jnp.float32,
                                       minval=math.log(DT_MIN), maxval=math.log(DT_MAX))
    inp['ssm_b_re'] = nrm((N_AB_LAYERS, SSM_GROUPS, SSM_STATE, SSM_GROUP), (2 * SSM_GROUP) ** -0.5)
    inp['ssm_b_im'] = nrm((N_AB_LAYERS, SSM_GROUPS, SSM_STATE, SSM_GROUP), (2 * SSM_GROUP) ** -0.5)
    inp['ssm_c_re'] = nrm((N_AB_LAYERS, SSM_GROUPS, SSM_GROUP, SSM_STATE), SSM_STATE ** -0.5)
    inp['ssm_c_im'] = nrm((N_AB_LAYERS, SSM_GROUPS, SSM_GROUP, SSM_STATE), SSM_STATE ** -0.5)
    inp['ssm_d'] = nrm((N_AB_LAYERS, A_WIDTH), 1.0)
    inp['w_glu'] = nrm((N_AB_LAYERS, A_WIDTH, A_WIDTH), A_WIDTH ** -0.5)
    inp['b_glu'] = nrm((N_AB_LAYERS, A_WIDTH), 0.02)
    inp['w_out'] = nrm((N_AB_LAYERS, A_WIDTH + B_WIDTH, D), (A_WIDTH + B_WIDTH) ** -0.5)
    inp['w_pool'] = nrm((N_C_LAYERS, POOL_GROUPS, POOL_GROUP_DIM, POOL_GROUP_DIM), POOL_GROUP_DIM ** -0.5)
    inp['pool_scale'] = 1.0 + nrm((N_C_LAYERS, D), 0.1)
    inp['w_coarse'] = nrm((DEPTH, D, MOE_GROUPS), D ** -0.5)
    inp['b_coarse'] = nrm((DEPTH, MOE_GROUPS), 0.01)
    inp['w_fine'] = nrm((DEPTH, D, N_EXPERTS), D ** -0.5)
    inp['b_fine'] = nrm((DEPTH, N_EXPERTS), 0.01)
    inp['w_gate'] = nrm((DEPTH, N_EXPERTS, D, D_FF_EXPERT), D ** -0.5)
    inp['w_up'] = nrm((DEPTH, N_EXPERTS, D, D_FF_EXPERT), D ** -0.5)
    inp['w_down'] = nrm((DEPTH, N_EXPERTS, D_FF_EXPERT, D), D_FF_EXPERT ** -0.5)
    return inp


def reference(x_prompt, x_sample, c_prompt, c_sample, cache_k, cache_v, cache_kidx, state_ssm_re,
              state_ssm_im, state_pool, page_table, w_ada, b_ada, g_mix, g_ffn, w_in, q_gain, k_gain,
              lam_re, lam_im, log_dt, ssm_b_re, ssm_b_im, ssm_c_re, ssm_c_im, ssm_d, w_glu, b_glu, w_out,
              w_pool, pool_scale, w_coarse, b_coarse, w_fine, b_fine, w_gate, w_up, w_down):
    p = dict(w_ada=w_ada, b_ada=b_ada, g_mix=g_mix, g_ffn=g_ffn, w_in=w_in, q_gain=q_gain, k_gain=k_gain,
             lam_re=lam_re, lam_im=lam_im, log_dt=log_dt, ssm_b_re=ssm_b_re, ssm_b_im=ssm_b_im,
             ssm_c_re=ssm_c_re, ssm_c_im=ssm_c_im, ssm_d=ssm_d, w_glu=w_glu, b_glu=b_glu, w_out=w_out,
             w_pool=w_pool, pool_scale=pool_scale, w_coarse=w_coarse, b_coarse=b_coarse, w_fine=w_fine,
             b_fine=b_fine, w_gate=w_gate, w_up=w_up, w_down=w_down)
    pos_p = jnp.arange(x_prompt.shape[1], dtype=jnp.int32)
    zero_ssm = jnp.zeros((N_AB_LAYERS, x_prompt.shape[0], SSM_GROUPS, SSM_STATE), x_prompt.dtype)
    y_prompt, k_p, v_p, kidx_p, ssm_re_p, ssm_im_p, pool_p = run_trunk(
        x_prompt, c_prompt, pos_p, zero_ssm, zero_ssm, None, [attend_prompt] * N_AB_LAYERS, p)
    past_len = page_table.shape[1] * PAGE_SIZE
    pos_s = past_len + jnp.arange(x_sample.shape[1], dtype=jnp.int32)
    attend_s = [make_sample_attend(cache_k[i], cache_v[i], cache_kidx[i], page_table) for i in range(N_AB_LAYERS)]
    y_sample, k_s, v_s, kidx_s, ssm_re_s, ssm_im_s, pool_s = run_trunk(
        x_sample, c_sample, pos_s, state_ssm_re, state_ssm_im, state_pool, attend_s, p)
    return (y_prompt, y_sample, k_p, v_p, kidx_p, ssm_re_p, ssm_im_p, pool_p,
            k_s, v_s, kidx_s, ssm_re_s, ssm_im_s, pool_s)
```

```python
import functools
import math

import jax
import jax.numpy as jnp
from jax import lax
from jax.experimental import pallas as pl
from jax.experimental.pallas import tpu as pltpu

F32 = jnp.float32
BF16 = jnp.bfloat16
I32 = jnp.int32

D_MODEL = 1024
EPS = 1e-6
A_WIDTH = 512
SSM_GROUP = 16
SSM_GROUPS = 32
SSM_STATE = 64
N_STATES = SSM_GROUPS * SSM_STATE
HEAD_DIM = 64
N_HEADS = 8
N_KV_HEADS = 2
KV_WIDTH = N_KV_HEADS * HEAD_DIM
IDX_HEADS = 8
IDX_DIM = 64
TOPK_MAX = 256
ROPE_THETA = 500000.0
ROT_HALF = 8
PAGE_SIZE = 128
POOL_WINDOWS = (2, 4, 8, 16)
POOL_GROUP_DIM = 256
POOL_STATE = 15
HIST_ROWS = 16
MOE_GROUPS = 4
EXPERTS_PER_GROUP = 4
N_EXPERTS = 16
D_FF_EXPERT = 256

OFF_Q, OFF_K, OFF_V, OFF_QI, OFF_KI, OFF_WI, IN_COLS = 512, 1024, 1152, 1280, 1792, 1856, 1864
IN_COLS_PADDED = 1920

V7X_LANES = 128
V7X_SUBLANES = 8
V7X_VMEM_BYTES = 64 * 2**20
VMEM_LIMIT_BYTES = (V7X_VMEM_BYTES * 3) // 4
INT_MIN = -(2**31)
NEG_INF = float("-inf")


def _params(*semantics):
    return pltpu.CompilerParams(dimension_semantics=semantics, vmem_limit_bytes=VMEM_LIMIT_BYTES)


def _norm_mod(x, gain, scale, shift):
    var = jnp.mean(x * x, axis=-1, keepdims=True)
    return (x * lax.rsqrt(var + EPS)) * gain * (1.0 + scale) + shift


def _silu(x):
    return x * jax.nn.sigmoid(x)


def _dot(a, b):
    return jnp.dot(a, b, preferred_element_type=F32)


def _dot_t(a, b):
    return lax.dot_general(a, b, (((1,), (1,)), ((), ())), preferred_element_type=F32)


class TokenGroup:
    def __init__(self, n_rows, rows_per_mod, tile):
        assert n_rows % rows_per_mod == 0 and (rows_per_mod % tile == 0 or tile % rows_per_mod == 0)
        self.n_rows, self.tile = n_rows, tile
        self.n_tiles = n_rows // tile
        if rows_per_mod >= tile:
            self.mod_rows = 1
            self.tiles_per_mod = rows_per_mod // tile
        else:
            self.mod_rows = tile
            self.tiles_per_mod = 1

    def mods(self, mod):
        if self.mod_rows == 1:
            return mod[:, None, :]
        rep = self.n_rows // mod.shape[0]
        return jnp.repeat(mod, rep, axis=0).reshape(self.n_tiles, self.tile, mod.shape[1])

    def mod_spec(self, piece):
        tpm = self.tiles_per_mod
        return pl.BlockSpec((None, self.mod_rows, D_MODEL), lambda i: (i // tpm, 0, piece))

    def row_spec(self, width):
        return pl.BlockSpec((self.tile, width), lambda i: (i, 0))


def _const_spec(shape):
    nd = len(shape)
    return pl.BlockSpec(shape, lambda *_: (0,) * nd)


def _ada_kernel(c_ref, w_ref, b_ref, o_ref):
    s = _silu(c_ref[...]).astype(BF16)
    o_ref[...] = _dot(s, w_ref[...].astype(BF16)) + b_ref[...]


def ada_params_all(c_all, w_ada, b_ada):
    n_layers, d, n6 = w_ada.shape
    m = c_all.shape[0]
    tn = 1536
    return pl.pallas_call(
        _ada_kernel,
        grid=(n_layers, n6 // tn),
        in_specs=[pl.BlockSpec((m, d), lambda l, j: (0, 0)),
                  pl.BlockSpec((None, d, tn), lambda l, j: (l, 0, j)),
                  pl.BlockSpec((None, 1, tn), lambda l, j: (l, 0, j))],
        out_specs=pl.BlockSpec((None, m, tn), lambda l, j: (l, 0, j)),
        out_shape=jax.ShapeDtypeStruct((n_layers, m, n6), F32),
        compiler_params=_params("arbitrary", "arbitrary"),
    )(c_all, w_ada, b_ada.reshape(n_layers, 1, n6))


def _rope(x, cos, s1, s2):
    w = x.shape[1]
    reps = w // V7X_LANES
    if reps > 1:
        cos, s1, s2 = (jnp.tile(t, (1, reps)) for t in (cos, s1, s2))
    return x * cos + pltpu.roll(x, w - ROT_HALF, 1) * s1 + pltpu.roll(x, ROT_HALF, 1) * s2


def _head_norm(x, mavg, gain):
    sq = x * x
    hi = sq.astype(BF16)
    lo = (sq - hi.astype(F32)).astype(BF16)
    var = _dot(hi, mavg) + _dot(lo, mavg)
    return (x * lax.rsqrt(var + EPS)) * gain


def _in_proj_kernel(x_ref, sc_ref, sh_ref, g_ref, w_ref, qg_ref, kg_ref, mavg_ref, rope_ref,
                    u_ref, q_ref, k_ref, v_ref, qi_ref, kiwi_ref):
    h = _norm_mod(x_ref[...], g_ref[...], sc_ref[...], sh_ref[...])
    p = _dot(h.astype(BF16), w_ref[...])
    cos, s1, s2 = rope_ref[0], rope_ref[1], rope_ref[2]
    mavg = mavg_ref[...]
    u_ref[...] = p[:, 0:OFF_Q]
    q = _rope(_head_norm(p[:, OFF_Q:OFF_K], mavg, qg_ref[...]), cos, s1, s2)
    q_ref[...] = (q * HEAD_DIM ** -0.5).astype(BF16)
    k = _head_norm(p[:, OFF_K:OFF_V], mavg[0:KV_WIDTH, 0:KV_WIDTH], kg_ref[...])
    k_ref[...] = _rope(k, cos, s1, s2)
    v_ref[...] = p[:, OFF_V:OFF_QI]
    qi = _rope(p[:, OFF_QI:OFF_KI], cos, s1, s2)
    qi_ref[...] = (qi * IDX_DIM ** -0.5).astype(BF16)
    kiwi_ref[...] = _rope(p[:, OFF_KI:IN_COLS_PADDED], rope_ref[3], rope_ref[4], rope_ref[5])


def _rope_tables(pos):
    t = pos.shape[0]
    inv = ROPE_THETA ** (-jnp.arange(ROT_HALF, dtype=F32) / ROT_HALF)
    ang = pos.astype(F32)[:, None] * inv[None, :]
    cos, sin = jnp.cos(ang), jnp.sin(ang)
    rest = HEAD_DIM - 2 * ROT_HALF
    z8, zr, onesr = jnp.zeros((t, ROT_HALF), F32), jnp.zeros((t, rest), F32), jnp.ones((t, rest), F32)
    c64 = jnp.concatenate([cos, cos, onesr], axis=1)
    s1_64 = jnp.concatenate([-sin, z8, zr], axis=1)
    s2_64 = jnp.concatenate([z8, sin, zr], axis=1)
    z64 = jnp.zeros((t, HEAD_DIM), F32)
    wscale = jnp.concatenate([jnp.full((t, IDX_HEADS), IDX_HEADS ** -0.5, F32),
                              jnp.ones((t, HEAD_DIM - IDX_HEADS), F32)], axis=1)
    return jnp.stack([jnp.concatenate([c64, c64], axis=1), jnp.concatenate([s1_64, s1_64], axis=1),
                      jnp.concatenate([s2_64, s2_64], axis=1), jnp.concatenate([c64, wscale], axis=1),
                      jnp.concatenate([s1_64, z64], axis=1), jnp.concatenate([s2_64, z64], axis=1)])


def in_proj(x2, mods, grp, g_mix, w_in_p, q_gain, k_gain, mavg, rope):
    n = x2.shape[0]
    tm = grp.tile
    rope_tiles = rope.shape[1] // tm
    outs = pl.pallas_call(
        _in_proj_kernel,
        grid=(grp.n_tiles,),
        in_specs=[grp.row_spec(D_MODEL), grp.mod_spec(1), grp.mod_spec(0), _const_spec((1, D_MODEL)),
                  _const_spec((D_MODEL, IN_COLS_PADDED)), _const_spec((1, 512)), _const_spec((1, KV_WIDTH)),
                  _const_spec((512, 512)),
                  pl.BlockSpec((6, tm, V7X_LANES), lambda i: (0, i % rope_tiles, 0))],
        out_specs=[grp.row_spec(512), grp.row_spec(512), grp.row_spec(KV_WIDTH), grp.row_spec(KV_WIDTH),
                   grp.row_spec(512), grp.row_spec(V7X_LANES)],
        out_shape=[jax.ShapeDtypeStruct((n, 512), F32), jax.ShapeDtypeStruct((n, 512), BF16),
                   jax.ShapeDtypeStruct((n, KV_WIDTH), F32), jax.ShapeDtypeStruct((n, KV_WIDTH), F32),
                   jax.ShapeDtypeStruct((n, 512), BF16), jax.ShapeDtypeStruct((n, V7X_LANES), F32)],
        compiler_params=_params("arbitrary"),
    )(x2, mods, mods, g_mix, w_in_p, q_gain, k_gain, mavg, rope)
    return outs


S5_BATCH_BLOCK = V7X_SUBLANES
S5_HALF_IN = A_WIDTH // 2
S5_HALF_STATES = N_STATES // 2
S5_SCAN_COLS = 1024


def _gelu_tanh(x):
    cdf = 0.5 * (1.0 + jnp.tanh(math.sqrt(2.0 / math.pi) * (x + 0.044715 * (x * x * x))))
    return x * cdf


def _s5_kernel(u_ref, x0re_ref, x0im_ref, are_ref, aim_ref, bre_ref, bim_ref, cre_ref, cim_ref, d_ref,
               wglu_ref, bglu_ref, y_ref, sre_ref, sim_ref, st_re, st_im, xs_re, xs_im, *, tt):
    bb = S5_BATCH_BLOCK
    tb = pl.program_id(1)

    @pl.when(tb == 0)
    def _():
        st_re[...] = x0re_ref[...]
        st_im[...] = x0im_ref[...]

    u = u_ref[...].reshape(tt * bb, A_WIDTH)
    ub = u.astype(BF16)
    for half in range(2):
        rows = slice(half * S5_HALF_IN, (half + 1) * S5_HALF_IN)
        cols = slice(half * S5_HALF_STATES, (half + 1) * S5_HALF_STATES)
        xs_re[:, cols] = _dot(ub[:, rows], bre_ref[rows, cols])
        xs_im[:, cols] = _dot(ub[:, rows], bim_ref[rows, cols])

    for c in range(N_STATES // S5_SCAN_COLS):
        cs = slice(c * S5_SCAN_COLS, (c + 1) * S5_SCAN_COLS)
        a_re = are_ref[:, cs]
        a_im = aim_ref[:, cs]

        def step(t, carry, cs=cs, a_re=a_re, a_im=a_im):
            sr, si = carry
            r0 = pl.multiple_of(t * bb, bb)
            nr = a_re * sr - a_im * si + xs_re[pl.ds(r0, bb), cs]
            ni = a_re * si + a_im * sr + xs_im[pl.ds(r0, bb), cs]
            xs_re[pl.ds(r0, bb), cs] = nr
            xs_im[pl.ds(r0, bb), cs] = ni
            return nr, ni

        sr, si = lax.fori_loop(0, tt, step, (st_re[:, cs], st_im[:, cs]))
        st_re[:, cs] = sr
        st_im[:, cs] = si

    xr = xs_re[...].astype(BF16)
    xi = xs_im[...].astype(BF16)
    ys = []
    for half in range(2):
        rows = slice(half * S5_HALF_STATES, (half + 1) * S5_HALF_STATES)
        cols = slice(half * S5_HALF_IN, (half + 1) * S5_HALF_IN)
        ys.append(_dot(xr[:, rows], cre_ref[rows, cols]) - _dot(xi[:, rows], cim_ref[rows, cols]))
    y = jnp.concatenate(ys, axis=1) + d_ref[...] * u
    z = _gelu_tanh(y)
    o = z * jax.nn.sigmoid(_dot(z.astype(BF16), wglu_ref[...]) + bglu_ref[...])
    y_ref[...] = o.reshape(tt, bb, A_WIDTH)

    @pl.when(tb == pl.num_programs(1) - 1)
    def _():
        sre_ref[...] = st_re[...]
        sim_ref[...] = st_im[...]


def s5_discretise(lam_re, lam_im, log_dt, b_re, b_im, c_re, c_im):
    dt = jnp.exp(log_dt)[:, None]
    mag = jnp.exp(lam_re * dt)
    abar_re, abar_im = mag * jnp.cos(lam_im * dt), mag * jnp.sin(lam_im * dt)
    den = lam_re * lam_re + lam_im * lam_im
    ir, ii = lam_re / den, -lam_im / den
    cr = (abar_re - 1.0) * ir - abar_im * ii
    ci = (abar_re - 1.0) * ii + abar_im * ir
    bb_re = cr[..., None] * b_re - ci[..., None] * b_im
    bb_im = cr[..., None] * b_im + ci[..., None] * b_re
    eye = jnp.eye(SSM_GROUPS, dtype=F32)
    pack_b = lambda bb: jnp.einsum("gpn,gh->gnhp", bb, eye).reshape(A_WIDTH, N_STATES).astype(BF16)
    pack_c = lambda cc: jnp.einsum("gnp,gh->gphn", cc, eye).reshape(N_STATES, A_WIDTH).astype(BF16)
    bcast = lambda a: jnp.broadcast_to(a.reshape(1, N_STATES), (S5_BATCH_BLOCK, N_STATES))
    return bcast(abar_re), bcast(abar_im), pack_b(bb_re), pack_b(bb_im), pack_c(c_re), pack_c(c_im)


def s5_mixer(u3, x0_re, x0_im, disc, d_skip, w_glu, b_glu, tt):
    t, b, _ = u3.shape
    bb = S5_BATCH_BLOCK
    a_re, a_im, bre, bim, cre, cim = disc
    state_spec = pl.BlockSpec((bb, N_STATES), lambda i, j: (i, 0))
    seq_spec = pl.BlockSpec((tt, bb, A_WIDTH), lambda i, j: (j, i, 0))
    return pl.pallas_call(
        functools.partial(_s5_kernel, tt=tt),
        grid=(b // bb, t // tt),
        in_specs=[seq_spec, state_spec, state_spec, _const_spec((bb, N_STATES)), _const_spec((bb, N_STATES)),
                  _const_spec((A_WIDTH, N_STATES)), _const_spec((A_WIDTH, N_STATES)),
                  _const_spec((N_STATES, A_WIDTH)), _const_spec((N_STATES, A_WIDTH)),
                  _const_spec((1, A_WIDTH)), _const_spec((A_WIDTH, A_WIDTH)), _const_spec((1, A_WIDTH))],
        out_specs=[seq_spec, state_spec, state_spec],
        out_shape=[jax.ShapeDtypeStruct((t, b, A_WIDTH), F32), jax.ShapeDtypeStruct((b, N_STATES), F32),
                   jax.ShapeDtypeStruct((b, N_STATES), F32)],
        scratch_shapes=[pltpu.VMEM((bb, N_STATES), F32), pltpu.VMEM((bb, N_STATES), F32),
                        pltpu.VMEM((tt * bb, N_STATES), F32), pltpu.VMEM((tt * bb, N_STATES), F32)],
        compiler_params=_params("arbitrary", "arbitrary"),
    )(u3, x0_re, x0_im, a_re, a_im, bre, bim, cre, cim, d_skip, w_glu, b_glu)


def _count(mask):
    return jnp.sum(jnp.where(mask, 1.0, 0.0), axis=1, keepdims=True)


def _select_attend(s_keys, qpos0, q, qi, wi, ki, k, v, key_ref, bias_ref, o_ref):
    m = q.shape[0]
    sc = jnp.zeros((m, s_keys), F32)
    for h in range(IDX_HEADS):
        d = _dot_t(qi[:, h * IDX_DIM:(h + 1) * IDX_DIM], ki)
        sc = sc + jnp.maximum(d, 0.0) * wi[:, h:h + 1]
    kpos = lax.broadcasted_iota(I32, (m, s_keys), 1)
    qpos = qpos0 + lax.broadcasted_iota(I32, (m, 1), 0)
    causal = kpos <= qpos
    bits = lax.bitcast_convert_type(sc, I32)
    mag = bits & 0x7FFFFFFF
    key_ref[...] = jnp.where(causal, jnp.where(bits < 0, -mag, mag), INT_MIN)
    keep = float(TOPK_MAX)

    def bit_step(it, thr):
        cand = thr + lax.shift_left(jnp.int32(1), 31 - it)
        return jnp.where(_count(key_ref[...] >= cand) >= keep, cand, thr)

    thr = lax.fori_loop(0, 32, bit_step, jnp.full((m, 1), INT_MIN, I32))
    key = key_ref[...]
    gt = key > thr
    eq = key == thr
    need = keep - _count(gt)
    bias_ref[...] = jnp.where((gt | eq) & causal, 0.0, NEG_INF)
    tie = (_count(eq) > need) & (thr > INT_MIN)

    @pl.when(jnp.max(jnp.where(tie, 1.0, 0.0)) > 0.0)
    def _():
        def idx_step(_, lohi):
            lo, hi = lohi
            mid = (lo + hi) >> 1
            ok = _count((key_ref[...] == thr) & (kpos <= mid)) >= need
            return jnp.where(ok, lo, mid + 1), jnp.where(ok, mid, hi)

        n_steps = max(1, (s_keys - 1).bit_length())
        last, _ = lax.fori_loop(0, n_steps, idx_step,
                                (jnp.zeros((m, 1), I32), jnp.full((m, 1), s_keys - 1, I32)))
        k2 = key_ref[...]
        sel = (k2 > thr) | ((k2 == thr) & (kpos <= last))
        bias_ref[...] = jnp.where(sel & causal, 0.0, NEG_INF)

    bias = bias_ref[...]
    group = N_HEADS // N_KV_HEADS
    for g in range(N_KV_HEADS):
        kg = k[:, g * HEAD_DIM:(g + 1) * HEAD_DIM]
        vg = v[:, g * HEAD_DIM:(g + 1) * HEAD_DIM]
        for hh in range(g * group, (g + 1) * group):
            s = _dot_t(q[:, hh * HEAD_DIM:(hh + 1) * HEAD_DIM], kg) + bias
            p = jnp.exp(s - jnp.max(s, axis=1, keepdims=True))
            l = jnp.sum(p, axis=1, keepdims=True)
            o = _dot(p.astype(BF16), vg) / l
            o_ref[:, hh * HEAD_DIM:(hh + 1) * HEAD_DIM] = o.astype(o_ref.dtype)


def _attn_prompt_kernel(q_ref, qi_ref, wi_ref, k_ref, v_ref, ki_ref, o_ref, key_ref, bias_ref, *, tq, buckets):
    i = pl.program_id(1)
    need_keys = (i + 1) * tq
    lo = 0
    for s_keys in buckets:
        @pl.when((need_keys > lo) & (need_keys <= s_keys))
        def _(s_keys=s_keys):
            _select_attend(s_keys, i * tq, q_ref[...], qi_ref[...], wi_ref[:, IDX_DIM:IDX_DIM + IDX_HEADS],
                           ki_ref[0:s_keys, 0:IDX_DIM].astype(BF16), k_ref[0:s_keys, :].astype(BF16),
                           v_ref[0:s_keys, :].astype(BF16), key_ref.at[:, 0:s_keys], bias_ref.at[:, 0:s_keys],
                           o_ref)
        lo = s_keys


def attend_prompt(q, qi, kiwi, k, v, batch, seq, tq=128, n_buckets=4):
    n = q.shape[0]
    nq = seq // tq
    buckets = tuple(seq * (j + 1) // n_buckets for j in range(n_buckets))
    q_spec = lambda w: pl.BlockSpec((tq, w), lambda b, i: (b * nq + i, 0))
    kv_spec = pl.BlockSpec((seq, KV_WIDTH), lambda b, i: (b, 0))
    return pl.pallas_call(
        functools.partial(_attn_prompt_kernel, tq=tq, buckets=buckets),
        grid=(batch, nq),
        in_specs=[q_spec(512), q_spec(512), q_spec(V7X_LANES), kv_spec, kv_spec, kv_spec],
        out_specs=q_spec(512),
        out_shape=jax.ShapeDtypeStruct((n, 512), BF16),
        scratch_shapes=[pltpu.VMEM((tq, seq), I32), pltpu.VMEM((tq, seq), F32)],
        compiler_params=_params("arbitrary", "arbitrary"),
    )(q, qi, kiwi, k, v, kiwi)


SAMPLE_Q_ROWS = V7X_SUBLANES


def _attn_sample_kernel(pt_ref, q_ref, qi_ref, wi_ref, kpg_ref, vpg_ref, kipg_ref, knew_ref, vnew_ref, kinew_ref,
                        o_ref, k_all, v_all, ki_all, key_ref, bias_ref, *, n_pages):
    p = pl.program_id(1)
    r0 = pl.multiple_of(p * PAGE_SIZE, PAGE_SIZE)

    @pl.when(p < n_pages)
    def _():
        k_all[pl.ds(r0, PAGE_SIZE), :] = kpg_ref[...].astype(BF16)
        v_all[pl.ds(r0, PAGE_SIZE), :] = vpg_ref[...].astype(BF16)
        ki_all[pl.ds(r0, PAGE_SIZE), :] = kipg_ref[...].astype(BF16)

    @pl.when(p == n_pages)
    def _():
        k_all[pl.ds(r0, PAGE_SIZE), :] = knew_ref[...].astype(BF16)
        v_all[pl.ds(r0, PAGE_SIZE), :] = vnew_ref[...].astype(BF16)
        ki_all[pl.ds(r0, PAGE_SIZE), :] = kinew_ref[:, 0:IDX_DIM].astype(BF16)
        s_keys = (n_pages + 1) * PAGE_SIZE
        _select_attend(s_keys, n_pages * PAGE_SIZE, q_ref[...], qi_ref[...],
                       wi_ref[:, IDX_DIM:IDX_DIM + IDX_HEADS], ki_all[...], k_all[...], v_all[...],
                       key_ref, bias_ref, o_ref)


def attend_sample(q3, qi3, kiwi3, k_new, v_new, kiwi_new, cache_k, cache_v, cache_ki, page_table):
    b = q3.shape[0]
    n_pages = page_table.shape[1]
    s_keys = (n_pages + 1) * PAGE_SIZE
    rows = SAMPLE_Q_ROWS
    q_spec = lambda w: pl.BlockSpec((None, rows, w), lambda i, p, pt: (i, 0, 0))
    page_spec = lambda w: pl.BlockSpec((None, PAGE_SIZE, w),
                                       lambda i, p, pt: (pt[i, jnp.minimum(p, n_pages - 1)], 0, 0))
    new_spec = pl.BlockSpec((None, PAGE_SIZE, V7X_LANES), lambda i, p, pt: (i, 0, 0))
    grid_spec = pltpu.PrefetchScalarGridSpec(
        num_scalar_prefetch=1,
        grid=(b, n_pages + 1),
        in_specs=[q_spec(512), q_spec(512), q_spec(V7X_LANES), page_spec(KV_WIDTH), page_spec(KV_WIDTH),
                  page_spec(IDX_DIM), new_spec, new_spec, new_spec],
        out_specs=q_spec(512),
        scratch_shapes=[pltpu.VMEM((s_keys, KV_WIDTH), BF16), pltpu.VMEM((s_keys, KV_WIDTH), BF16),
                        pltpu.VMEM((s_keys, IDX_DIM), BF16), pltpu.VMEM((rows, s_keys), I32),
                        pltpu.VMEM((rows, s_keys), F32)],
    )
    return pl.pallas_call(
        functools.partial(_attn_sample_kernel, n_pages=n_pages),
        grid_spec=grid_spec,
        out_shape=jax.ShapeDtypeStruct((b, rows, 512), BF16),
        compiler_params=_params("arbitrary", "arbitrary"),
    )(page_table, q3, qi3, kiwi3, cache_k, cache_v, cache_ki, k_new, v_new, kiwi_new)


def _out_proj_kernel(x_ref, ya_ref, yb_ref, gate_ref, w_ref, o_ref):
    y = _dot(ya_ref[...].astype(BF16), w_ref[0:A_WIDTH, :]) + _dot(yb_ref[...], w_ref[A_WIDTH:2 * A_WIDTH, :])
    o_ref[...] = x_ref[...] + gate_ref[...] * y


def out_proj(x2, y_a, y_b, mods, grp, w_out):
    return pl.pallas_call(
        _out_proj_kernel,
        grid=(grp.n_tiles,),
        in_specs=[grp.row_spec(D_MODEL), grp.row_spec(A_WIDTH), grp.row_spec(512), grp.mod_spec(2),
                  _const_spec((2 * A_WIDTH, D_MODEL))],
        out_specs=grp.row_spec(D_MODEL),
        out_shape=jax.ShapeDtypeStruct(x2.shape, F32),
        compiler_params=_params("arbitrary"),
    )(x2, y_a, y_b, mods, w_out)


def _pool_kernel(x_ref, hist_ref, sc_ref, sh_ref, gate_ref, g_ref, w_ref, ps_ref, o_ref, st_ref, *,
                 tiles_per_seq, hist_is_x, pos_base, state_rows):
    i = pl.program_id(0)
    tm = x_ref.shape[0]
    x = x_ref[...]
    h = _norm_mod(x, g_ref[...], sc_ref[...], sh_ref[...])
    if hist_is_x:
        hist = _norm_mod(hist_ref[...], g_ref[...], sc_ref[...], sh_ref[...])
        hist = jnp.where(i % tiles_per_seq == 0, 0.0, hist)
    else:
        hist = hist_ref[...]
    ext = jnp.concatenate([hist, h], axis=0)
    pos = pos_base + (i % tiles_per_seq) * tm + lax.broadcasted_iota(I32, (tm, 1), 0)
    ys = []
    for g, win in enumerate(POOL_WINDOWS):
        cols = slice(g * POOL_GROUP_DIM, (g + 1) * POOL_GROUP_DIM)
        s = ext[:, cols]
        shift = 1
        while shift < win:
            s = s + pltpu.roll(s, shift, 0)
            shift *= 2
        cnt = jnp.minimum(pos + 1, win).astype(F32)
        pooled = s[HIST_ROWS:, :] / cnt - h[:, cols]
        ys.append(_dot(pooled.astype(BF16), w_ref[g]))
    y = jnp.concatenate(ys, axis=1) * ps_ref[...]
    o_ref[...] = x + gate_ref[...] * y
    st_ref[...] = ext[HIST_ROWS + tm - state_rows:, :]


def pool_mixer(x2, hist, mods, grp, g_mix, w_pool, pool_scale, *, tiles_per_seq, hist_is_x, pos_base, state_rows):
    tm = grp.tile
    n_seq = grp.n_tiles // tiles_per_seq
    if hist_is_x:
        per_tile = tm // HIST_ROWS
        hist_spec = pl.BlockSpec((HIST_ROWS, D_MODEL), lambda i: (jnp.maximum(i * per_tile - 1, 0), 0))
    else:
        hist_spec = pl.BlockSpec((None, HIST_ROWS, D_MODEL), lambda i: (i, 0, 0))
    return pl.pallas_call(
        functools.partial(_pool_kernel, tiles_per_seq=tiles_per_seq, hist_is_x=hist_is_x, pos_base=pos_base,
                          state_rows=state_rows),
        grid=(grp.n_tiles,),
        in_specs=[grp.row_spec(D_MODEL), hist_spec, grp.mod_spec(1), grp.mod_spec(0), grp.mod_spec(2),
                  _const_spec((1, D_MODEL)), _const_spec((len(POOL_WINDOWS), POOL_GROUP_DIM, POOL_GROUP_DIM)),
                  _const_spec((1, D_MODEL))],
        out_specs=[grp.row_spec(D_MODEL),
                   pl.BlockSpec((None, state_rows, D_MODEL), lambda i: (i // tiles_per_seq, 0, 0))],
        out_shape=[jax.ShapeDtypeStruct(x2.shape, F32), jax.ShapeDtypeStruct((n_seq, state_rows, D_MODEL), F32)],
        compiler_params=_params("arbitrary"),
    )(x2, hist, mods, mods, mods, g_mix, w_pool, pool_scale)


ROUTER_COLS = V7X_LANES


def _router_gates(logits):
    col = lambda j: logits[:, j:j + 1]
    lc = [col(j) for j in range(MOE_GROUPS)]
    mc = functools.reduce(jnp.maximum, lc)
    pg = 1.0 / functools.reduce(lambda a, b: a + b, [jnp.exp(l - mc) for l in lc])
    grp = jnp.where(lc[0] == mc, 0, jnp.where(lc[1] == mc, 1, jnp.where(lc[2] == mc, 2, 3)))
    fl = []
    for j in range(EXPERTS_PER_GROUP):
        cands = [col(MOE_GROUPS + g * EXPERTS_PER_GROUP + j) for g in range(MOE_GROUPS)]
        fl.append(jnp.where(grp == 0, cands[0], jnp.where(grp == 1, cands[1],
                                                          jnp.where(grp == 2, cands[2], cands[3]))))
    first = lambda vals, mx: jnp.where(vals[0] == mx, 0, jnp.where(vals[1] == mx, 1, jnp.where(vals[2] == mx, 2, 3)))
    m1 = functools.reduce(jnp.maximum, fl)
    i1 = first(fl, m1)
    rest = [jnp.where(i1 == j, NEG_INF, fl[j]) for j in range(EXPERTS_PER_GROUP)]
    m2 = functools.reduce(jnp.maximum, rest)
    i2 = first(rest, m2)
    e2 = jnp.exp(m2 - m1)
    w1 = pg / (1.0 + e2)
    w2 = pg * e2 / (1.0 + e2)
    e_first = grp * EXPERTS_PER_GROUP + i1
    e_second = grp * EXPERTS_PER_GROUP + i2
    return [jnp.where(e_first == e, w1, jnp.where(e_second == e, w2, 0.0)) for e in range(N_EXPERTS)]


def _moe_kernel(x_ref, sc_ref, sh_ref, gate_ref, g_ref, wr_ref, br_ref, wg_ref, wu_ref, wd_ref, o_ref,
                hb_ref, gates_ref, acc_ref):
    e = pl.program_id(1)

    @pl.when(e == 0)
    def _():
        h = _norm_mod(x_ref[...], g_ref[...], sc_ref[...], sh_ref[...])
        hb_ref[...] = h.astype(BF16)
        logits = jnp.dot(h, wr_ref[...], preferred_element_type=F32, precision=lax.Precision.HIGHEST) + br_ref[...]
        gates = _router_gates(logits)
        lane = lax.broadcasted_iota(I32, (x_ref.shape[0], ROUTER_COLS), 1)
        g_all = jnp.zeros((x_ref.shape[0], ROUTER_COLS), F32)
        for j, gj in enumerate(gates):
            g_all = jnp.where(lane == j, gj, g_all)
        gates_ref[...] = g_all
        acc_ref[...] = jnp.zeros_like(acc_ref)

    hb = hb_ref[...]
    lane = lax.broadcasted_iota(I32, gates_ref.shape, 1)
    gate_e = jnp.sum(jnp.where(lane == e, gates_ref[...], 0.0), axis=1, keepdims=True)
    hid = _silu(_dot(hb, wg_ref[...])) * _dot(hb, wu_ref[...]) * gate_e
    acc_ref[...] += _dot(hid.astype(BF16), wd_ref[...])

    @pl.when(e == pl.num_programs(1) - 1)
    def _():
        o_ref[...] = x_ref[...] + gate_ref[...] * acc_ref[...]


def hier_moe(x2, mods, grp, g_ffn, w_router, b_router, w_gate, w_up, w_down):
    tm = grp.tile
    row = pl.BlockSpec((tm, D_MODEL), lambda i, e: (i, 0))
    tpm = grp.tiles_per_mod
    mod_spec = lambda piece: pl.BlockSpec((None, grp.mod_rows, D_MODEL), lambda i, e: (i // tpm, 0, piece))
    return pl.pallas_call(
        _moe_kernel,
        grid=(grp.n_tiles, N_EXPERTS),
        in_specs=[row, mod_spec(4), mod_spec(3), mod_spec(5), _const_spec((1, D_MODEL)),
                  _const_spec((D_MODEL, ROUTER_COLS)), _const_spec((1, ROUTER_COLS)),
                  pl.BlockSpec((None, D_MODEL, D_FF_EXPERT), lambda i, e: (e, 0, 0)),
                  pl.BlockSpec((None, D_MODEL, D_FF_EXPERT), lambda i, e: (e, 0, 0)),
                  pl.BlockSpec((None, D_FF_EXPERT, D_MODEL), lambda i, e: (e, 0, 0))],
        out_specs=row,
        out_shape=jax.ShapeDtypeStruct(x2.shape, F32),
        scratch_shapes=[pltpu.VMEM((tm, D_MODEL), BF16), pltpu.VMEM((tm, ROUTER_COLS), F32),
                        pltpu.VMEM((tm, D_MODEL), F32)],
        compiler_params=_params("arbitrary", "arbitrary"),
    )(x2, mods, mods, mods, g_ffn, w_router, b_router, w_gate, w_up, w_down)


def _prep_weights(w):
    depth = w["w_ada"].shape[0]
    n_ab = w["w_in"].shape[0]
    prep = {"layers": [], "ab": [], "c": []}
    blk = jnp.arange(512) // HEAD_DIM
    prep["mavg"] = jnp.where(blk[:, None] == blk[None, :], 1.0 / HEAD_DIM, 0.0).astype(BF16)
    for l in range(depth):
        wr = jnp.concatenate([w["w_coarse"][l], w["w_fine"][l]], axis=1)
        br = jnp.concatenate([w["b_coarse"][l], w["b_fine"][l]])
        pad = ROUTER_COLS - wr.shape[1]
        prep["layers"].append(dict(
            g_mix=w["g_mix"][l][None, :], g_ffn=w["g_ffn"][l][None, :],
            w_router=jnp.pad(wr, ((0, 0), (0, pad))), b_router=jnp.pad(br, (0, pad))[None, :],
            w_gate=w["w_gate"][l].astype(BF16), w_up=w["w_up"][l].astype(BF16), w_down=w["w_down"][l].astype(BF16)))
    for i in range(n_ab):
        prep["ab"].append(dict(
            w_in=jnp.pad(w["w_in"][i], ((0, 0), (0, IN_COLS_PADDED - IN_COLS))).astype(BF16),
            q_gain=jnp.tile(w["q_gain"][i], N_HEADS)[None, :], k_gain=jnp.tile(w["k_gain"][i], N_KV_HEADS)[None, :],
            disc=s5_discretise(w["lam_re"][i], w["lam_im"][i], w["log_dt"][i], w["ssm_b_re"][i], w["ssm_b_im"][i],
                               w["ssm_c_re"][i], w["ssm_c_im"][i]),
            d_skip=w["ssm_d"][i][None, :], w_glu=w["w_glu"][i].astype(BF16), b_glu=w["b_glu"][i][None, :],
            w_out=w["w_out"][i].astype(BF16)))
    for i in range(w["w_pool"].shape[0]):
        prep["c"].append(dict(w_pool=w["w_pool"][i].astype(BF16), pool_scale=w["pool_scale"][i][None, :]))
    return prep


def _run_prompt(x, mod_all, prep):
    batch, seq, _ = x.shape
    n = batch * seq
    x2 = x.reshape(n, D_MODEL)
    rope = _rope_tables(jnp.arange(seq, dtype=I32))
    tile = 512
    grp = TokenGroup(n, seq, tile)
    grp_moe = TokenGroup(n, seq, 1024)
    zero_state = jnp.zeros((batch, N_STATES), F32)
    ks, vs, kis, sres, sims, pools = [], [], [], [], [], []
    for layer in range(mod_all.shape[0]):
        i = layer // 2
        lw = prep["layers"][layer]
        mods = grp.mods(mod_all[layer])
        if layer % 2 == 0:
            ab = prep["ab"][i]
            u, q, k, v, qi, kiwi = in_proj(x2, mods, grp, lw["g_mix"], ab["w_in"], ab["q_gain"], ab["k_gain"],
                                           prep["mavg"], rope)
            u3 = jnp.swapaxes(u.reshape(batch, seq, A_WIDTH), 0, 1)
            y3, s_re, s_im = s5_mixer(u3, zero_state, zero_state, ab["disc"], ab["d_skip"], ab["w_glu"],
                                      ab["b_glu"], tt=64)
            y_a = jnp.swapaxes(y3, 0, 1).reshape(n, A_WIDTH)
            y_b = attend_prompt(q, qi, kiwi, k, v, batch, seq)
            x2 = out_proj(x2, y_a, y_b, mods, grp, ab["w_out"])
            ks.append(k.reshape(batch, seq, N_KV_HEADS, HEAD_DIM))
            vs.append(v.reshape(batch, seq, N_KV_HEADS, HEAD_DIM))
            kis.append(kiwi[:, :IDX_DIM].reshape(batch, seq, IDX_DIM))
            sres.append(s_re.reshape(batch, SSM_GROUPS, SSM_STATE))
            sims.append(s_im.reshape(batch, SSM_GROUPS, SSM_STATE))
        else:
            c = prep["c"][i]
            x2, st = pool_mixer(x2, x2, mods, grp, lw["g_mix"], c["w_pool"], c["pool_scale"],
                                tiles_per_seq=seq // tile, hist_is_x=True, pos_base=0, state_rows=HIST_ROWS)
            pools.append(st[:, HIST_ROWS - POOL_STATE:, :])
        x2 = hier_moe(x2, grp_moe.mods(mod_all[layer]), grp_moe, lw["g_ffn"], lw["w_router"], lw["b_router"],
                      lw["w_gate"], lw["w_up"], lw["w_down"])
    return (x2.reshape(batch, seq, D_MODEL), jnp.stack(ks), jnp.stack(vs), jnp.stack(kis), jnp.stack(sres),
            jnp.stack(sims), jnp.stack(pools))


def _run_sample(x, mod_all, prep, cache_k, cache_v, cache_kidx, state_re, state_im, state_pool, page_table):
    batch, seq, _ = x.shape
    n = batch * seq
    past_len = page_table.shape[1] * PAGE_SIZE
    x2 = x.reshape(n, D_MODEL)
    rope = _rope_tables(jnp.tile(past_len + jnp.arange(seq, dtype=I32), batch))
    grp = TokenGroup(n, seq, n)
    rows = SAMPLE_Q_ROWS
    grp_pool = TokenGroup(batch * rows, rows, rows)
    pad_q = lambda a: jnp.pad(a.reshape(batch, seq, a.shape[-1]), ((0, 0), (0, rows - seq), (0, 0)))
    pad_page = lambda a: jnp.pad(a.reshape(batch, seq, a.shape[-1]), ((0, 0), (0, PAGE_SIZE - seq), (0, 0)))
    ks, vs, kis, sres, sims, pools = [], [], [], [], [], []
    for layer in range(mod_all.shape[0]):
        i = layer // 2
        lw = prep["layers"][layer]
        mods = grp.mods(mod_all[layer])
        if layer % 2 == 0:
            ab = prep["ab"][i]
            u, q, k, v, qi, kiwi = in_proj(x2, mods, grp, lw["g_mix"], ab["w_in"], ab["q_gain"], ab["k_gain"],
                                           prep["mavg"], rope)
            u3 = jnp.swapaxes(u.reshape(batch, seq, A_WIDTH), 0, 1)
            y3, s_re, s_im = s5_mixer(u3, state_re[i].reshape(batch, N_STATES), state_im[i].reshape(batch, N_STATES),
                                      ab["disc"], ab["d_skip"], ab["w_glu"], ab["b_glu"], tt=seq)
            y_a = jnp.swapaxes(y3, 0, 1).reshape(n, A_WIDTH)
            n_phys = cache_k.shape[1]
            y_b = attend_sample(pad_q(q), pad_q(qi), pad_q(kiwi), pad_page(k), pad_page(v), pad_page(kiwi),
                                cache_k[i].reshape(n_phys, PAGE_SIZE, KV_WIDTH),
                                cache_v[i].reshape(n_phys, PAGE_SIZE, KV_WIDTH), cache_kidx[i], page_table)
            x2 = out_proj(x2, y_a, y_b[:, :seq].reshape(n, 512), mods, grp, ab["w_out"])
            ks.append(k.reshape(batch, seq, N_KV_HEADS, HEAD_DIM))
            vs.append(v.reshape(batch, seq, N_KV_HEADS, HEAD_DIM))
            kis.append(kiwi[:, :IDX_DIM].reshape(batch, seq, IDX_DIM))
            sres.append(s_re.reshape(batch, SSM_GROUPS, SSM_STATE))
            sims.append(s_im.reshape(batch, SSM_GROUPS, SSM_STATE))
        else:
            c = prep["c"][i]
            xp = jnp.pad(x2.reshape(batch, seq, D_MODEL), ((0, 0), (0, rows - seq), (0, 0)))
            hist = jnp.pad(state_pool[i], ((0, 0), (HIST_ROWS - POOL_STATE, 0), (0, 0)))
            xo, st = pool_mixer(xp.reshape(batch * rows, D_MODEL), hist, grp_pool.mods(mod_all[layer]), grp_pool,
                                lw["g_mix"], c["w_pool"], c["pool_scale"], tiles_per_seq=1, hist_is_x=False,
                                pos_base=past_len, state_rows=HIST_ROWS + rows)
            x2 = xo.reshape(batch, rows, D_MODEL)[:, :seq].reshape(n, D_MODEL)
            pools.append(st[:, seq + HIST_ROWS - POOL_STATE:seq + HIST_ROWS, :])
        x2 = hier_moe(x2, mods, grp, lw["g_ffn"], lw["w_router"], lw["b_router"], lw["w_gate"], lw["w_up"],
                      lw["w_down"])
    return (x2.reshape(batch, seq, D_MODEL), jnp.stack(ks), jnp.stack(vs), jnp.stack(kis), jnp.stack(sres),
            jnp.stack(sims), jnp.stack(pools))


def kernel(x_prompt, x_sample, c_prompt, c_sample, cache_k, cache_v, cache_kidx, state_ssm_re, state_ssm_im,
           state_pool, page_table, w_ada, b_ada, g_mix, g_ffn, w_in, q_gain, k_gain, lam_re, lam_im, log_dt,
           ssm_b_re, ssm_b_im, ssm_c_re, ssm_c_im, ssm_d, w_glu, b_glu, w_out, w_pool, pool_scale, w_coarse,
           b_coarse, w_fine, b_fine, w_gate, w_up, w_down):
    weights = dict(w_ada=w_ada, g_mix=g_mix, g_ffn=g_ffn, w_in=w_in, q_gain=q_gain, k_gain=k_gain, lam_re=lam_re,
                   lam_im=lam_im, log_dt=log_dt, ssm_b_re=ssm_b_re, ssm_b_im=ssm_b_im, ssm_c_re=ssm_c_re,
                   ssm_c_im=ssm_c_im, ssm_d=ssm_d, w_glu=w_glu, b_glu=b_glu, w_out=w_out, w_pool=w_pool,
                   pool_scale=pool_scale, w_coarse=w_coarse, b_coarse=b_coarse, w_fine=w_fine, b_fine=b_fine,
                   w_gate=w_gate, w_up=w_up, w_down=w_down)
    prep = _prep_weights(weights)
    n_prompt = c_prompt.shape[0]
    mod_all = ada_params_all(jnp.concatenate([c_prompt, c_sample], axis=0), w_ada, b_ada)
    out_p = _run_prompt(x_prompt, mod_all[:, :n_prompt], prep)
    out_s = _run_sample(x_sample, mod_all[:, n_prompt:], prep, cache_k, cache_v, cache_kidx, state_ssm_re,
                        state_ssm_im, state_pool, page_table)
    return (out_p[0], out_s[0]) + out_p[1:] + out_s[1:]
```

```python
import functools
import math

import jax
import jax.numpy as jnp
from jax import lax
from jax.experimental import pallas as pl
from jax.experimental.pallas import tpu as pltpu

F32 = jnp.float32
BF16 = jnp.bfloat16
I32 = jnp.int32

D_MODEL = 1024
EPS = 1e-6
A_WIDTH = 512
SSM_GROUP = 16
SSM_GROUPS = 32
SSM_STATE = 64
N_STATES = SSM_GROUPS * SSM_STATE
HEAD_DIM = 64
N_HEADS = 8
N_KV_HEADS = 2
KV_WIDTH = N_KV_HEADS * HEAD_DIM
IDX_HEADS = 8
IDX_DIM = 64
TOPK_MAX = 256
ROPE_THETA = 500000.0
ROT_HALF = 8
PAGE_SIZE = 128
POOL_WINDOWS = (2, 4, 8, 16)
POOL_GROUP_DIM = 256
POOL_STATE = 15
HIST_ROWS = 16
MOE_GROUPS = 4
EXPERTS_PER_GROUP = 4
N_EXPERTS = 16
D_FF_EXPERT = 256

OFF_Q, OFF_K, OFF_V, OFF_QI, OFF_KI, OFF_WI, IN_COLS = 512, 1024, 1152, 1280, 1792, 1856, 1864
IN_COLS_PADDED = 1920

V7X_LANES = 128
V7X_SUBLANES = 8
V7X_VMEM_BYTES = 64 * 2**20
VMEM_LIMIT_BYTES = (V7X_VMEM_BYTES * 3) // 4
INT_MIN = -(2**31)
NEG_INF = float("-inf")


def _params(*semantics):
    return pltpu.CompilerParams(dimension_semantics=semantics, vmem_limit_bytes=VMEM_LIMIT_BYTES)


def _norm_mod(x, gain, scale, shift):
    var = jnp.mean(x * x, axis=-1, keepdims=True)
    return (x * lax.rsqrt(var + EPS)) * gain * (1.0 + scale) + shift


def _silu(x):
    return x * jax.nn.sigmoid(x)


def _dot(a, b):
    return jnp.dot(a, b, preferred_element_type=F32)


def _dot_t(a, b):
    return lax.dot_general(a, b, (((1,), (1,)), ((), ())), preferred_element_type=F32)


class TokenGroup:
    def __init__(self, n_rows, rows_per_mod, tile):
        assert n_rows % rows_per_mod == 0 and (rows_per_mod % tile == 0 or tile % rows_per_mod == 0)
        self.n_rows, self.tile = n_rows, tile
        self.n_tiles = n_rows // tile
        if rows_per_mod >= tile:
            self.mod_rows = 1
            self.tiles_per_mod = rows_per_mod // tile
        else:
            self.mod_rows = tile
            self.tiles_per_mod = 1

    def mods(self, mod):
        if self.mod_rows == 1:
            return mod[:, None, :]
        rep = self.n_rows // mod.shape[0]
        return jnp.repeat(mod, rep, axis=0).reshape(self.n_tiles, self.tile, mod.shape[1])

    def mod_spec(self, piece):
        tpm = self.tiles_per_mod
        return pl.BlockSpec((None, self.mod_rows, D_MODEL), lambda i: (i // tpm, 0, piece))

    def row_spec(self, width):
        return pl.BlockSpec((self.tile, width), lambda i: (i, 0))


def _const_spec(shape):
    nd = len(shape)
    return pl.BlockSpec(shape, lambda *_: (0,) * nd)


def _ada_kernel(c_ref, w_ref, b_ref, o_ref):
    s = _silu(c_ref[...]).astype(BF16)
    o_ref[...] = _dot(s, w_ref[...].astype(BF16)) + b_ref[...]


def ada_params_all(c_all, w_ada, b_ada):
    n_layers, d, n6 = w_ada.shape
    m = c_all.shape[0]
    tn = 1536
    return pl.pallas_call(
        _ada_kernel,
        grid=(n_layers, n6 // tn),
        in_specs=[pl.BlockSpec((m, d), lambda l, j: (0, 0)),
                  pl.BlockSpec((None, d, tn), lambda l, j: (l, 0, j)),
                  pl.BlockSpec((None, 1, tn), lambda l, j: (l, 0, j))],
        out_specs=pl.BlockSpec((None, m, tn), lambda l, j: (l, 0, j)),
        out_shape=jax.ShapeDtypeStruct((n_layers, m, n6), F32),
        compiler_params=_params("arbitrary", "arbitrary"),
    )(c_all, w_ada, b_ada.reshape(n_layers, 1, n6))


def _rope(x, cos, s1, s2):
    w = x.shape[1]
    reps = w // V7X_LANES
    if reps > 1:
        cos, s1, s2 = (jnp.tile(t, (1, reps)) for t in (cos, s1, s2))
    return x * cos + pltpu.roll(x, w - ROT_HALF, 1) * s1 + pltpu.roll(x, ROT_HALF, 1) * s2


def _head_norm(x, mavg, gain):
    sq = x * x
    hi = sq.astype(BF16)
    lo = (sq - hi.astype(F32)).astype(BF16)
    var = _dot(hi, mavg) + _dot(lo, mavg)
    return (x * lax.rsqrt(var + EPS)) * gain


def _in_proj_kernel(x_ref, sc_ref, sh_ref, g_ref, w_ref, qg_ref, kg_ref, mavg_ref, rope_ref,
                    u_ref, q_ref, k_ref, v_ref, qi_ref, kiwi_ref):
    h = _norm_mod(x_ref[...], g_ref[...], sc_ref[...], sh_ref[...])
    p = _dot(h.astype(BF16), w_ref[...])
    cos, s1, s2 = rope_ref[0], rope_ref[1], rope_ref[2]
    mavg = mavg_ref[...]
    u_ref[...] = p[:, 0:OFF_Q]
    q = _rope(_head_norm(p[:, OFF_Q:OFF_K], mavg, qg_ref[...]), cos, s1, s2)
    q_ref[...] = (q * HEAD_DIM ** -0.5).astype(BF16)
    k = _head_norm(p[:, OFF_K:OFF_V], mavg[0:KV_WIDTH, 0:KV_WIDTH], kg_ref[...])
    k_ref[...] = _rope(k, cos, s1, s2)
    v_ref[...] = p[:, OFF_V:OFF_QI]
    qi = _rope(p[:, OFF_QI:OFF_KI], cos, s1, s2)
    qi_ref[...] = (qi * IDX_DIM ** -0.5).astype(BF16)
    kiwi_ref[...] = _rope(p[:, OFF_KI:IN_COLS_PADDED], rope_ref[3], rope_ref[4], rope_ref[5])


def _rope_tables(pos):
    t = pos.shape[0]
    inv = ROPE_THETA ** (-jnp.arange(ROT_HALF, dtype=F32) / ROT_HALF)
    ang = pos.astype(F32)[:, None] * inv[None, :]
    cos, sin = jnp.cos(ang), jnp.sin(ang)
    rest = HEAD_DIM - 2 * ROT_HALF
    z8, zr, onesr = jnp.zeros((t, ROT_HALF), F32), jnp.zeros((t, rest), F32), jnp.ones((t, rest), F32)
    c64 = jnp.concatenate([cos, cos, onesr], axis=1)
    s1_64 = jnp.concatenate([-sin, z8, zr], axis=1)
    s2_64 = jnp.concatenate([z8, sin, zr], axis=1)
    z64 = jnp.zeros((t, HEAD_DIM), F32)
    wscale = jnp.concatenate([jnp.full((t, IDX_HEADS), IDX_HEADS ** -0.5, F32),
                              jnp.ones((t, HEAD_DIM - IDX_HEADS), F32)], axis=1)
    return jnp.stack([jnp.concatenate([c64, c64], axis=1), jnp.concatenate([s1_64, s1_64], axis=1),
                      jnp.concatenate([s2_64, s2_64], axis=1), jnp.concatenate([c64, wscale], axis=1),
                      jnp.concatenate([s1_64, z64], axis=1), jnp.concatenate([s2_64, z64], axis=1)])


def in_proj(x2, mods, grp, g_mix, w_in_p, q_gain, k_gain, mavg, rope):
    n = x2.shape[0]
    tm = grp.tile
    rope_tiles = rope.shape[1] // tm
    outs = pl.pallas_call(
        _in_proj_kernel,
        grid=(grp.n_tiles,),
        in_specs=[grp.row_spec(D_MODEL), grp.mod_spec(1), grp.mod_spec(0), _const_spec((1, D_MODEL)),
                  _const_spec((D_MODEL, IN_COLS_PADDED)), _const_spec((1, 512)), _const_spec((1, KV_WIDTH)),
                  _const_spec((512, 512)),
                  pl.BlockSpec((6, tm, V7X_LANES), lambda i: (0, i % rope_tiles, 0))],
        out_specs=[grp.row_spec(512), grp.row_spec(512), grp.row_spec(KV_WIDTH), grp.row_spec(KV_WIDTH),
                   grp.row_spec(512), grp.row_spec(V7X_LANES)],
        out_shape=[jax.ShapeDtypeStruct((n, 512), F32), jax.ShapeDtypeStruct((n, 512), BF16),
                   jax.ShapeDtypeStruct((n, KV_WIDTH), F32), jax.ShapeDtypeStruct((n, KV_WIDTH), F32),
                   jax.ShapeDtypeStruct((n, 512), BF16), jax.ShapeDtypeStruct((n, V7X_LANES), F32)],
        compiler_params=_params("arbitrary"),
    )(x2, mods, mods, g_mix, w_in_p, q_gain, k_gain, mavg, rope)
    return outs


S5_BATCH_BLOCK = V7X_SUBLANES
S5_HALF_IN = A_WIDTH // 2
S5_HALF_STATES = N_STATES // 2
S5_SCAN_COLS = 1024


def _gelu_tanh(x):
    cdf = 0.5 * (1.0 + jnp.tanh(math.sqrt(2.0 / math.pi) * (x + 0.044715 * (x * x * x))))
    return x * cdf


def _s5_kernel(u_ref, x0re_ref, x0im_ref, are_ref, aim_ref, bre_ref, bim_ref, cre_ref, cim_ref, d_ref,
               wglu_ref, bglu_ref, y_ref, sre_ref, sim_ref, st_re, st_im, xs_re, xs_im, *, tt):
    bb = S5_BATCH_BLOCK
    tb = pl.program_id(1)

    @pl.when(tb == 0)
    def _():
        st_re[...] = x0re_ref[...]
        st_im[...] = x0im_ref[...]

    u = u_ref[...].reshape(tt * bb, A_WIDTH)
    ub = u.astype(BF16)
    for half in range(2):
        rows = slice(half * S5_HALF_IN, (half + 1) * S5_HALF_IN)
        cols = slice(half * S5_HALF_STATES, (half + 1) * S5_HALF_STATES)
        xs_re[:, cols] = _dot(ub[:, rows], bre_ref[rows, cols])
        xs_im[:, cols] = _dot(ub[:, rows], bim_ref[rows, cols])

    for c in range(N_STATES // S5_SCAN_COLS):
        cs = slice(c * S5_SCAN_COLS, (c + 1) * S5_SCAN_COLS)
        a_re = are_ref[:, cs]
        a_im = aim_ref[:, cs]

        def step(t, carry, cs=cs, a_re=a_re, a_im=a_im):
            sr, si = carry
            r0 = pl.multiple_of(t * bb, bb)
            nr = a_re * sr - a_im * si + xs_re[pl.ds(r0, bb), cs]
            ni = a_re * si + a_im * sr + xs_im[pl.ds(r0, bb), cs]
            xs_re[pl.ds(r0, bb), cs] = nr
            xs_im[pl.ds(r0, bb), cs] = ni
            return nr, ni

        sr, si = lax.fori_loop(0, tt, step, (st_re[:, cs], st_im[:, cs]))
        st_re[:, cs] = sr
        st_im[:, cs] = si

    xr = xs_re[...].astype(BF16)
    xi = xs_im[...].astype(BF16)
    ys = []
    for half in range(2):
        rows = slice(half * S5_HALF_STATES, (half + 1) * S5_HALF_STATES)
        cols = slice(half * S5_HALF_IN, (half + 1) * S5_HALF_IN)
        ys.append(_dot(xr[:, rows], cre_ref[rows, cols]) - _dot(xi[:, rows], cim_ref[rows, cols]))
    y = jnp.concatenate(ys, axis=1) + d_ref[...] * u
    z = _gelu_tanh(y)
    o = z * jax.nn.sigmoid(_dot(z.astype(BF16), wglu_ref[...]) + bglu_ref[...])
    y_ref[...] = o.reshape(tt, bb, A_WIDTH)

    @pl.when(tb == pl.num_programs(1) - 1)
    def _():
        sre_ref[...] = st_re[...]
        sim_ref[...] = st_im[...]


def s5_discretise(lam_re, lam_im, log_dt, b_re, b_im, c_re, c_im):
    dt = jnp.exp(log_dt)[:, None]
    mag = jnp.exp(lam_re * dt)
    abar_re, abar_im = mag * jnp.cos(lam_im * dt), mag * jnp.sin(lam_im * dt)
    den = lam_re * lam_re + lam_im * lam_im
    ir, ii = lam_re / den, -lam_im / den
    cr = (abar_re - 1.0) * ir - abar_im * ii
    ci = (abar_re - 1.0) * ii + abar_im * ir
    bb_re = cr[..., None] * b_re - ci[..., None] * b_im
    bb_im = cr[..., None] * b_im + ci[..., None] * b_re
    eye = jnp.eye(SSM_GROUPS, dtype=F32)
    pack_b = lambda bb: jnp.einsum("gpn,gh->gnhp", bb, eye).reshape(A_WIDTH, N_STATES).astype(BF16)
    pack_c = lambda cc: jnp.einsum("gnp,gh->gphn", cc, eye).reshape(N_STATES, A_WIDTH).astype(BF16)
    bcast = lambda a: jnp.broadcast_to(a.reshape(1, N_STATES), (S5_BATCH_BLOCK, N_STATES))
    return bcast(abar_re), bcast(abar_im), pack_b(bb_re), pack_b(bb_im), pack_c(c_re), pack_c(c_im)


def s5_mixer(u3, x0_re, x0_im, disc, d_skip, w_glu, b_glu, tt):
    t, b, _ = u3.shape
    bb = S5_BATCH_BLOCK
    a_re, a_im, bre, bim, cre, cim = disc
    state_spec = pl.BlockSpec((bb, N_STATES), lambda i, j: (i, 0))
    seq_spec = pl.BlockSpec((tt, bb, A_WIDTH), lambda i, j: (j, i, 0))
    return pl.pallas_call(
        functools.partial(_s5_kernel, tt=tt),
        grid=(b // bb, t // tt),
        in_specs=[seq_spec, state_spec, state_spec, _const_spec((bb, N_STATES)), _const_spec((bb, N_STATES)),
                  _const_spec((A_WIDTH, N_STATES)), _const_spec((A_WIDTH, N_STATES)),
                  _const_spec((N_STATES, A_WIDTH)), _const_spec((N_STATES, A_WIDTH)),
                  _const_spec((1, A_WIDTH)), _const_spec((A_WIDTH, A_WIDTH)), _const_spec((1, A_WIDTH))],
        out_specs=[seq_spec, state_spec, state_spec],
        out_shape=[jax.ShapeDtypeStruct((t, b, A_WIDTH), F32), jax.ShapeDtypeStruct((b, N_STATES), F32),
                   jax.ShapeDtypeStruct((b, N_STATES), F32)],
        scratch_shapes=[pltpu.VMEM((bb, N_STATES), F32), pltpu.VMEM((bb, N_STATES), F32),
                        pltpu.VMEM((tt * bb, N_STATES), F32), pltpu.VMEM((tt * bb, N_STATES), F32)],
        compiler_params=_params("arbitrary", "arbitrary"),
    )(u3, x0_re, x0_im, a_re, a_im, bre, bim, cre, cim, d_skip, w_glu, b_glu)


def _count(mask):
    return jnp.sum(jnp.where(mask, 1.0, 0.0), axis=1, keepdims=True)


def _select_attend(s_keys, keep, qpos0, q, qi, wi, ki, k, v, key_ref, bias_ref, o_ref, keys_transposed=False):
    m = q.shape[0]
    keep = float(keep)
    qk = _dot if keys_transposed else _dot_t
    feat = (lambda a, lo, hi: a[lo:hi, :]) if keys_transposed else (lambda a, lo, hi: a[:, lo:hi])
    sc = jnp.zeros((m, s_keys), F32)
    for h in range(IDX_HEADS):
        d = qk(qi[:, h * IDX_DIM:(h + 1) * IDX_DIM], ki)
        sc = sc + jnp.maximum(d, 0.0) * wi[:, h:h + 1]
    kpos = lax.broadcasted_iota(I32, (m, s_keys), 1)
    qpos = qpos0 + lax.broadcasted_iota(I32, (m, 1), 0)
    causal = kpos <= qpos
    bits = lax.bitcast_convert_type(sc, I32)
    mag = bits & 0x7FFFFFFF
    key_ref[...] = jnp.where(causal, jnp.where(bits < 0, -mag, mag), INT_MIN)

    def bit_step(it, thr):
        cand = thr + lax.shift_left(jnp.int32(1), 31 - it)
        return jnp.where(_count(key_ref[...] >= cand) >= keep, cand, thr)

    thr = lax.fori_loop(0, 32, bit_step, jnp.full((m, 1), INT_MIN, I32))
    key = key_ref[...]
    gt = key > thr
    eq = key == thr
    need = keep - _count(gt)
    bias_ref[...] = jnp.where((gt | eq) & causal, 0.0, NEG_INF)
    tie = (_count(eq) > need) & (thr > INT_MIN)

    @pl.when(jnp.max(jnp.where(tie, 1.0, 0.0)) > 0.0)
    def _():
        def idx_step(_, lohi):
            lo, hi = lohi
            mid = (lo + hi) >> 1
            ok = _count((key_ref[...] == thr) & (kpos <= mid)) >= need
            return jnp.where(ok, lo, mid + 1), jnp.where(ok, mid, hi)

        n_steps = max(1, (s_keys - 1).bit_length())
        last, _ = lax.fori_loop(0, n_steps, idx_step,
                                (jnp.zeros((m, 1), I32), jnp.full((m, 1), s_keys - 1, I32)))
        k2 = key_ref[...]
        sel = (k2 > thr) | ((k2 == thr) & (kpos <= last))
        bias_ref[...] = jnp.where(sel & causal, 0.0, NEG_INF)

    bias = bias_ref[...]
    group = N_HEADS // N_KV_HEADS
    for g in range(N_KV_HEADS):
        kg = feat(k, g * HEAD_DIM, (g + 1) * HEAD_DIM)
        vg = feat(v, g * HEAD_DIM, (g + 1) * HEAD_DIM)
        pv = _dot_t if keys_transposed else _dot
        for hh in range(g * group, (g + 1) * group):
            s = qk(q[:, hh * HEAD_DIM:(hh + 1) * HEAD_DIM], kg) + bias
            p = jnp.exp(s - jnp.max(s, axis=1, keepdims=True))
            l = jnp.sum(p, axis=1, keepdims=True)
            o = pv(p.astype(BF16), vg) / l
            o_ref[:, hh * HEAD_DIM:(hh + 1) * HEAD_DIM] = o.astype(o_ref.dtype)


def _attn_prompt_kernel(q_ref, qi_ref, wi_ref, k_ref, v_ref, ki_ref, o_ref, key_ref, bias_ref, *, tq, buckets,
                        keep):
    i = pl.program_id(1)
    need_keys = (i + 1) * tq
    lo = 0
    for s_keys in buckets:
        @pl.when((need_keys > lo) & (need_keys <= s_keys))
        def _(s_keys=s_keys):
            _select_attend(s_keys, keep, i * tq, q_ref[...], qi_ref[...], wi_ref[:, IDX_DIM:IDX_DIM + IDX_HEADS],
                           ki_ref[0:s_keys, 0:IDX_DIM].astype(BF16), k_ref[0:s_keys, :].astype(BF16),
                           v_ref[0:s_keys, :].astype(BF16), key_ref.at[:, 0:s_keys], bias_ref.at[:, 0:s_keys],
                           o_ref)
        lo = s_keys


def attend_prompt(q, qi, kiwi, k, v, batch, seq, tq=128, n_buckets=4):
    n = q.shape[0]
    nq = seq // tq
    buckets = tuple(seq * (j + 1) // n_buckets for j in range(n_buckets))
    q_spec = lambda w: pl.BlockSpec((tq, w), lambda b, i: (b * nq + i, 0))
    kv_spec = pl.BlockSpec((seq, KV_WIDTH), lambda b, i: (b, 0))
    return pl.pallas_call(
        functools.partial(_attn_prompt_kernel, tq=tq, buckets=buckets, keep=min(TOPK_MAX, seq // 4)),
        grid=(batch, nq),
        in_specs=[q_spec(512), q_spec(512), q_spec(V7X_LANES), kv_spec, kv_spec, kv_spec],
        out_specs=q_spec(512),
        out_shape=jax.ShapeDtypeStruct((n, 512), BF16),
        scratch_shapes=[pltpu.VMEM((tq, seq), I32), pltpu.VMEM((tq, seq), F32)],
        compiler_params=_params("arbitrary", "arbitrary"),
    )(q, qi, kiwi, k, v, kiwi)


SAMPLE_Q_ROWS = V7X_SUBLANES


def _attn_sample_kernel(pt_ref, q_ref, qi_ref, wi_ref, knew_ref, vnew_ref, kinew_ref, ck_hbm, cv_hbm, cki_hbm,
                        o_ref, k_all, v_all, ki_all, sems, key_ref, bias_ref, *, layer, n_pages, keep):
    b = pl.program_id(0)
    slot = b % 2
    past = n_pages * PAGE_SIZE

    def page_copies(seq, sl):
        copies = []
        for p in range(n_pages):
            phys = pt_ref[seq, p]
            window = pl.ds(p * PAGE_SIZE, PAGE_SIZE)
            copies.append(pltpu.make_async_copy(ck_hbm.at[layer, phys], k_all.at[sl, :, window], sems.at[sl, 0]))
            copies.append(pltpu.make_async_copy(cv_hbm.at[layer, phys], v_all.at[sl, :, window], sems.at[sl, 1]))
            copies.append(pltpu.make_async_copy(cki_hbm.at[layer, phys], ki_all.at[sl, :, window], sems.at[sl, 2]))
        return copies

    @pl.when(b == 0)
    def _():
        for c in page_copies(0, 0):
            c.start()

    @pl.when(b + 1 < pl.num_programs(0))
    def _():
        for c in page_copies(b + 1, 1 - slot):
            c.start()

    k_all[slot, :, past:past + PAGE_SIZE] = knew_ref[...]
    v_all[slot, :, past:past + PAGE_SIZE] = vnew_ref[...]
    ki_all[slot, :, past:past + PAGE_SIZE] = kinew_ref[...]
    for c in page_copies(b, slot):
        c.wait()
    _select_attend(past + PAGE_SIZE, keep, past, q_ref[...], qi_ref[...], wi_ref[:, IDX_DIM:IDX_DIM + IDX_HEADS],
                   ki_all[slot].astype(BF16), k_all[slot].astype(BF16), v_all[slot].astype(BF16),
                   key_ref, bias_ref, o_ref, keys_transposed=True)


def attend_sample(q3, qi3, kiwi3, k_new_t, v_new_t, ki_new_t, cache_k_t, cache_v_t, cache_ki_t, page_table, layer,
                  seq):
    b = q3.shape[0]
    n_pages = page_table.shape[1]
    s_keys = (n_pages + 1) * PAGE_SIZE
    rows = SAMPLE_Q_ROWS
    q_spec = lambda w: pl.BlockSpec((None, rows, w), lambda i, pt: (i, 0, 0))
    new_spec = lambda f: pl.BlockSpec((None, f, PAGE_SIZE), lambda i, pt: (i, 0, 0))
    hbm = pl.BlockSpec(memory_space=pl.ANY)
    grid_spec = pltpu.PrefetchScalarGridSpec(
        num_scalar_prefetch=1,
        grid=(b,),
        in_specs=[q_spec(512), q_spec(512), q_spec(V7X_LANES), new_spec(KV_WIDTH), new_spec(KV_WIDTH),
                  new_spec(IDX_DIM), hbm, hbm, hbm],
        out_specs=q_spec(512),
        scratch_shapes=[pltpu.VMEM((2, KV_WIDTH, s_keys), F32), pltpu.VMEM((2, KV_WIDTH, s_keys), F32),
                        pltpu.VMEM((2, IDX_DIM, s_keys), F32), pltpu.SemaphoreType.DMA((2, 3)),
                        pltpu.VMEM((rows, s_keys), I32), pltpu.VMEM((rows, s_keys), F32)],
    )
    keep = min(TOPK_MAX, (n_pages * PAGE_SIZE + seq) // 4)
    return pl.pallas_call(
        functools.partial(_attn_sample_kernel, layer=layer, n_pages=n_pages, keep=keep),
        grid_spec=grid_spec,
        out_shape=jax.ShapeDtypeStruct((b, rows, 512), BF16),
        compiler_params=_params("arbitrary"),
    )(page_table, q3, qi3, kiwi3, k_new_t, v_new_t, ki_new_t, cache_k_t, cache_v_t, cache_ki_t)


def _out_proj_kernel(x_ref, ya_ref, yb_ref, gate_ref, w_ref, o_ref):
    y = _dot(ya_ref[...].astype(BF16), w_ref[0:A_WIDTH, :]) + _dot(yb_ref[...], w_ref[A_WIDTH:2 * A_WIDTH, :])
    o_ref[...] = x_ref[...] + gate_ref[...] * y


def out_proj(x2, y_a, y_b, mods, grp, w_out):
    return pl.pallas_call(
        _out_proj_kernel,
        grid=(grp.n_tiles,),
        in_specs=[grp.row_spec(D_MODEL), grp.row_spec(A_WIDTH), grp.row_spec(512), grp.mod_spec(2),
                  _const_spec((2 * A_WIDTH, D_MODEL))],
        out_specs=grp.row_spec(D_MODEL),
        out_shape=jax.ShapeDtypeStruct(x2.shape, F32),
        compiler_params=_params("arbitrary"),
    )(x2, y_a, y_b, mods, w_out)


def _pool_kernel(x_ref, hist_ref, sc_ref, sh_ref, gate_ref, g_ref, w_ref, ps_ref, o_ref, st_ref, *,
                 tiles_per_seq, hist_is_x, pos_base, state_rows):
    i = pl.program_id(0)
    tm = x_ref.shape[0]
    x = x_ref[...]
    h = _norm_mod(x, g_ref[...], sc_ref[...], sh_ref[...])
    if hist_is_x:
        hist = _norm_mod(hist_ref[...], g_ref[...], sc_ref[...], sh_ref[...])
        hist = jnp.where(i % tiles_per_seq == 0, 0.0, hist)
    else:
        hist = hist_ref[...]
    ext = jnp.concatenate([hist, h], axis=0)
    pos = pos_base + (i % tiles_per_seq) * tm + lax.broadcasted_iota(I32, (tm, 1), 0)
    ys = []
    for g, win in enumerate(POOL_WINDOWS):
        cols = slice(g * POOL_GROUP_DIM, (g + 1) * POOL_GROUP_DIM)
        s = ext[:, cols]
        shift = 1
        while shift < win:
            s = s + pltpu.roll(s, shift, 0)
            shift *= 2
        cnt = jnp.minimum(pos + 1, win).astype(F32)
        pooled = s[HIST_ROWS:, :] / cnt - h[:, cols]
        ys.append(_dot(pooled.astype(BF16), w_ref[g]))
    y = jnp.concatenate(ys, axis=1) * ps_ref[...]
    o_ref[...] = x + gate_ref[...] * y
    st_ref[...] = ext[HIST_ROWS + tm - state_rows:, :]


def pool_mixer(x2, hist, mods, grp, g_mix, w_pool, pool_scale, *, tiles_per_seq, hist_is_x, pos_base, state_rows):
    tm = grp.tile
    n_seq = grp.n_tiles // tiles_per_seq
    if hist_is_x:
        per_tile = tm // HIST_ROWS
        hist_spec = pl.BlockSpec((HIST_ROWS, D_MODEL), lambda i: (jnp.maximum(i * per_tile - 1, 0), 0))
    else:
        hist_spec = pl.BlockSpec((None, HIST_ROWS, D_MODEL), lambda i: (i, 0, 0))
    return pl.pallas_call(
        functools.partial(_pool_kernel, tiles_per_seq=tiles_per_seq, hist_is_x=hist_is_x, pos_base=pos_base,
                          state_rows=state_rows),
        grid=(grp.n_tiles,),
        in_specs=[grp.row_spec(D_MODEL), hist_spec, grp.mod_spec(1), grp.mod_spec(0), grp.mod_spec(2),
                  _const_spec((1, D_MODEL)), _const_spec((len(POOL_WINDOWS), POOL_GROUP_DIM, POOL_GROUP_DIM)),
                  _const_spec((1, D_MODEL))],
        out_specs=[grp.row_spec(D_MODEL),
                   pl.BlockSpec((None, state_rows, D_MODEL), lambda i: (i // tiles_per_seq, 0, 0))],
        out_shape=[jax.ShapeDtypeStruct(x2.shape, F32), jax.ShapeDtypeStruct((n_seq, state_rows, D_MODEL), F32)],
        compiler_params=_params("arbitrary"),
    )(x2, hist, mods, mods, mods, g_mix, w_pool, pool_scale)


ROUTER_COLS = V7X_LANES


def _router_gates(logits):
    col = lambda j: logits[:, j:j + 1]
    lc = [col(j) for j in range(MOE_GROUPS)]
    mc = functools.reduce(jnp.maximum, lc)
    pg = 1.0 / functools.reduce(lambda a, b: a + b, [jnp.exp(l - mc) for l in lc])
    grp = jnp.where(lc[0] == mc, 0, jnp.where(lc[1] == mc, 1, jnp.where(lc[2] == mc, 2, 3)))
    fl = []
    for j in range(EXPERTS_PER_GROUP):
        cands = [col(MOE_GROUPS + g * EXPERTS_PER_GROUP + j) for g in range(MOE_GROUPS)]
        fl.append(jnp.where(grp == 0, cands[0], jnp.where(grp == 1, cands[1],
                                                          jnp.where(grp == 2, cands[2], cands[3]))))
    first = lambda vals, mx: jnp.where(vals[0] == mx, 0, jnp.where(vals[1] == mx, 1, jnp.where(vals[2] == mx, 2, 3)))
    m1 = functools.reduce(jnp.maximum, fl)
    i1 = first(fl, m1)
    rest = [jnp.where(i1 == j, NEG_INF, fl[j]) for j in range(EXPERTS_PER_GROUP)]
    m2 = functools.reduce(jnp.maximum, rest)
    i2 = first(rest, m2)
    e2 = jnp.exp(m2 - m1)
    w1 = pg / (1.0 + e2)
    w2 = pg * e2 / (1.0 + e2)
    e_first = grp * EXPERTS_PER_GROUP + i1
    e_second = grp * EXPERTS_PER_GROUP + i2
    return [jnp.where(e_first == e, w1, jnp.where(e_second == e, w2, 0.0)) for e in range(N_EXPERTS)]


def _moe_kernel(x_ref, sc_ref, sh_ref, gate_ref, g_ref, wr_ref, br_ref, wg_ref, wu_ref, wd_ref, o_ref,
                hb_ref, gates_ref, acc_ref):
    e = pl.program_id(1)

    @pl.when(e == 0)
    def _():
        h = _norm_mod(x_ref[...], g_ref[...], sc_ref[...], sh_ref[...])
        hb_ref[...] = h.astype(BF16)
        logits = jnp.dot(h, wr_ref[...], preferred_element_type=F32, precision=lax.Precision.HIGHEST) + br_ref[...]
        gates = _router_gates(logits)
        lane = lax.broadcasted_iota(I32, (x_ref.shape[0], ROUTER_COLS), 1)
        g_all = jnp.zeros((x_ref.shape[0], ROUTER_COLS), F32)
        for j, gj in enumerate(gates):
            g_all = jnp.where(lane == j, gj, g_all)
        gates_ref[...] = g_all
        acc_ref[...] = jnp.zeros_like(acc_ref)

    hb = hb_ref[...]
    lane = lax.broadcasted_iota(I32, gates_ref.shape, 1)
    gate_e = jnp.sum(jnp.where(lane == e, gates_ref[...], 0.0), axis=1, keepdims=True)
    hid = _silu(_dot(hb, wg_ref[...])) * _dot(hb, wu_ref[...]) * gate_e
    acc_ref[...] += _dot(hid.astype(BF16), wd_ref[...])

    @pl.when(e == pl.num_programs(1) - 1)
    def _():
        o_ref[...] = x_ref[...] + gate_ref[...] * acc_ref[...]


def hier_moe(x2, mods, grp, g_ffn, w_router, b_router, w_gate, w_up, w_down):
    tm = grp.tile
    row = pl.BlockSpec((tm, D_MODEL), lambda i, e: (i, 0))
    tpm = grp.tiles_per_mod
    mod_spec = lambda piece: pl.BlockSpec((None, grp.mod_rows, D_MODEL), lambda i, e: (i // tpm, 0, piece))
    return pl.pallas_call(
        _moe_kernel,
        grid=(grp.n_tiles, N_EXPERTS),
        in_specs=[row, mod_spec(4), mod_spec(3), mod_spec(5), _const_spec((1, D_MODEL)),
                  _const_spec((D_MODEL, ROUTER_COLS)), _const_spec((1, ROUTER_COLS)),
                  pl.BlockSpec((None, D_MODEL, D_FF_EXPERT), lambda i, e: (e, 0, 0)),
                  pl.BlockSpec((None, D_MODEL, D_FF_EXPERT), lambda i, e: (e, 0, 0)),
                  pl.BlockSpec((None, D_FF_EXPERT, D_MODEL), lambda i, e: (e, 0, 0))],
        out_specs=row,
        out_shape=jax.ShapeDtypeStruct(x2.shape, F32),
        scratch_shapes=[pltpu.VMEM((tm, D_MODEL), BF16), pltpu.VMEM((tm, ROUTER_COLS), F32),
                        pltpu.VMEM((tm, D_MODEL), F32)],
        compiler_params=_params("arbitrary", "arbitrary"),
    )(x2, mods, mods, mods, g_ffn, w_router, b_router, w_gate, w_up, w_down)


def _prep_weights(w):
    depth = w["w_ada"].shape[0]
    n_ab = w["w_in"].shape[0]
    prep = {"layers": [], "ab": [], "c": []}
    blk = jnp.arange(512) // HEAD_DIM
    prep["mavg"] = jnp.where(blk[:, None] == blk[None, :], 1.0 / HEAD_DIM, 0.0).astype(BF16)
    for l in range(depth):
        wr = jnp.concatenate([w["w_coarse"][l], w["w_fine"][l]], axis=1)
        br = jnp.concatenate([w["b_coarse"][l], w["b_fine"][l]])
        pad = ROUTER_COLS - wr.shape[1]
        prep["layers"].append(dict(
            g_mix=w["g_mix"][l][None, :], g_ffn=w["g_ffn"][l][None, :],
            w_router=jnp.pad(wr, ((0, 0), (0, pad))), b_router=jnp.pad(br, (0, pad))[None, :],
            w_gate=w["w_gate"][l].astype(BF16), w_up=w["w_up"][l].astype(BF16), w_down=w["w_down"][l].astype(BF16)))
    for i in range(n_ab):
        prep["ab"].append(dict(
            w_in=jnp.pad(w["w_in"][i], ((0, 0), (0, IN_COLS_PADDED - IN_COLS))).astype(BF16),
            q_gain=jnp.tile(w["q_gain"][i], N_HEADS)[None, :], k_gain=jnp.tile(w["k_gain"][i], N_KV_HEADS)[None, :],
            disc=s5_discretise(w["lam_re"][i], w["lam_im"][i], w["log_dt"][i], w["ssm_b_re"][i], w["ssm_b_im"][i],
                               w["ssm_c_re"][i], w["ssm_c_im"][i]),
            d_skip=w["ssm_d"][i][None, :], w_glu=w["w_glu"][i].astype(BF16), b_glu=w["b_glu"][i][None, :],
            w_out=w["w_out"][i].astype(BF16)))
    for i in range(w["w_pool"].shape[0]):
        prep["c"].append(dict(w_pool=w["w_pool"][i].astype(BF16), pool_scale=w["pool_scale"][i][None, :]))
    return prep


def _run_prompt(x, mod_all, prep):
    batch, seq, _ = x.shape
    n = batch * seq
    x2 = x.reshape(n, D_MODEL)
    rope = _rope_tables(jnp.arange(seq, dtype=I32))
    tile = 512
    grp = TokenGroup(n, seq, tile)
    grp_moe = TokenGroup(n, seq, 1024)
    zero_state = jnp.zeros((batch, N_STATES), F32)
    ks, vs, kis, sres, sims, pools = [], [], [], [], [], []
    for layer in range(mod_all.shape[0]):
        i = layer // 2
        lw = prep["layers"][layer]
        mods = grp.mods(mod_all[layer])
        if layer % 2 == 0:
            ab = prep["ab"][i]
            u, q, k, v, qi, kiwi = in_proj(x2, mods, grp, lw["g_mix"], ab["w_in"], ab["q_gain"], ab["k_gain"],
                                           prep["mavg"], rope)
            u3 = jnp.swapaxes(u.reshape(batch, seq, A_WIDTH), 0, 1)
            y3, s_re, s_im = s5_mixer(u3, zero_state, zero_state, ab["disc"], ab["d_skip"], ab["w_glu"],
                                      ab["b_glu"], tt=64)
            y_a = jnp.swapaxes(y3, 0, 1).reshape(n, A_WIDTH)
            y_b = attend_prompt(q, qi, kiwi, k, v, batch, seq)
            x2 = out_proj(x2, y_a, y_b, mods, grp, ab["w_out"])
            ks.append(k.reshape(batch, seq, N_KV_HEADS, HEAD_DIM))
            vs.append(v.reshape(batch, seq, N_KV_HEADS, HEAD_DIM))
            kis.append(kiwi[:, :IDX_DIM].reshape(batch, seq, IDX_DIM))
            sres.append(s_re.reshape(batch, SSM_GROUPS, SSM_STATE))
            sims.append(s_im.reshape(batch, SSM_GROUPS, SSM_STATE))
        else:
            c = prep["c"][i]
            x2, st = pool_mixer(x2, x2, mods, grp, lw["g_mix"], c["w_pool"], c["pool_scale"],
                                tiles_per_seq=seq // tile, hist_is_x=True, pos_base=0, state_rows=HIST_ROWS)
            pools.append(st[:, HIST_ROWS - POOL_STATE:, :])
        x2 = hier_moe(x2, grp_moe.mods(mod_all[layer]), grp_moe, lw["g_ffn"], lw["w_router"], lw["b_router"],
                      lw["w_gate"], lw["w_up"], lw["w_down"])
    return (x2.reshape(batch, seq, D_MODEL), jnp.stack(ks), jnp.stack(vs), jnp.stack(kis), jnp.stack(sres),
            jnp.stack(sims), jnp.stack(pools))


def _run_sample(x, mod_all, prep, cache_k, cache_v, cache_kidx, state_re, state_im, state_pool, page_table):
    batch, seq, _ = x.shape
    n = batch * seq
    past_len = page_table.shape[1] * PAGE_SIZE
    x2 = x.reshape(n, D_MODEL)
    rope = _rope_tables(jnp.tile(past_len + jnp.arange(seq, dtype=I32), batch))
    grp = TokenGroup(n, seq, n)
    rows = SAMPLE_Q_ROWS
    grp_pool = TokenGroup(batch * rows, rows, rows)
    pad_q = lambda a: jnp.pad(a.reshape(batch, seq, a.shape[-1]), ((0, 0), (0, rows - seq), (0, 0)))
    page_t = lambda a: jnp.pad(jnp.swapaxes(a.reshape(batch, seq, a.shape[-1]), 1, 2),
                               ((0, 0), (0, 0), (0, PAGE_SIZE - seq)))
    n_layers_ab, n_phys = cache_k.shape[:2]
    cache_k_t = jnp.transpose(cache_k, (0, 1, 3, 4, 2)).reshape(n_layers_ab, n_phys, KV_WIDTH, PAGE_SIZE)
    cache_v_t = jnp.transpose(cache_v, (0, 1, 3, 4, 2)).reshape(n_layers_ab, n_phys, KV_WIDTH, PAGE_SIZE)
    cache_ki_t = jnp.transpose(cache_kidx, (0, 1, 3, 2))
    ks, vs, kis, sres, sims, pools = [], [], [], [], [], []
    for layer in range(mod_all.shape[0]):
        i = layer // 2
        lw = prep["layers"][layer]
        mods = grp.mods(mod_all[layer])
        if layer % 2 == 0:
            ab = prep["ab"][i]
            u, q, k, v, qi, kiwi = in_proj(x2, mods, grp, lw["g_mix"], ab["w_in"], ab["q_gain"], ab["k_gain"],
                                           prep["mavg"], rope)
            u3 = jnp.swapaxes(u.reshape(batch, seq, A_WIDTH), 0, 1)
            y3, s_re, s_im = s5_mixer(u3, state_re[i].reshape(batch, N_STATES), state_im[i].reshape(batch, N_STATES),
                                      ab["disc"], ab["d_skip"], ab["w_glu"], ab["b_glu"], tt=seq)
            y_a = jnp.swapaxes(y3, 0, 1).reshape(n, A_WIDTH)
            y_b = attend_sample(pad_q(q), pad_q(qi), pad_q(kiwi), page_t(k), page_t(v), page_t(kiwi[:, :IDX_DIM]),
                                cache_k_t, cache_v_t, cache_ki_t, page_table, i, seq)
            x2 = out_proj(x2, y_a, y_b[:, :seq].reshape(n, 512), mods, grp, ab["w_out"])
            ks.append(k.reshape(batch, seq, N_KV_HEADS, HEAD_DIM))
            vs.append(v.reshape(batch, seq, N_KV_HEADS, HEAD_DIM))
            kis.append(kiwi[:, :IDX_DIM].reshape(batch, seq, IDX_DIM))
            sres.append(s_re.reshape(batch, SSM_GROUPS, SSM_STATE))
            sims.append(s_im.reshape(batch, SSM_GROUPS, SSM_STATE))
        else:
            c = prep["c"][i]
            xp = jnp.pad(x2.reshape(batch, seq, D_MODEL), ((0, 0), (0, rows - seq), (0, 0)))
            hist = jnp.pad(state_pool[i], ((0, 0), (HIST_ROWS - POOL_STATE, 0), (0, 0)))
            xo, st = pool_mixer(xp.reshape(batch * rows, D_MODEL), hist, grp_pool.mods(mod_all[layer]), grp_pool,
                                lw["g_mix"], c["w_pool"], c["pool_scale"], tiles_per_seq=1, hist_is_x=False,
                                pos_base=past_len, state_rows=HIST_ROWS + rows)
            x2 = xo.reshape(batch, rows, D_MODEL)[:, :seq].reshape(n, D_MODEL)
            pools.append(st[:, seq + HIST_ROWS - POOL_STATE:seq + HIST_ROWS, :])
        x2 = hier_moe(x2, mods, grp, lw["g_ffn"], lw["w_router"], lw["b_router"], lw["w_gate"], lw["w_up"],
                      lw["w_down"])
    return (x2.reshape(batch, seq, D_MODEL), jnp.stack(ks), jnp.stack(vs), jnp.stack(kis), jnp.stack(sres),
            jnp.stack(sims), jnp.stack(pools))


def kernel(x_prompt, x_sample, c_prompt, c_sample, cache_k, cache_v, cache_kidx, state_ssm_re, state_ssm_im,
           state_pool, page_table, w_ada, b_ada, g_mix, g_ffn, w_in, q_gain, k_gain, lam_re, lam_im, log_dt,
           ssm_b_re, ssm_b_im, ssm_c_re, ssm_c_im, ssm_d, w_glu, b_glu, w_out, w_pool, pool_scale, w_coarse,
           b_coarse, w_fine, b_fine, w_gate, w_up, w_down):
    weights = dict(w_ada=w_ada, g_mix=g_mix, g_ffn=g_ffn, w_in=w_in, q_gain=q_gain, k_gain=k_gain, lam_re=lam_re,
                   lam_im=lam_im, log_dt=log_dt, ssm_b_re=ssm_b_re, ssm_b_im=ssm_b_im, ssm_c_re=ssm_c_re,
                   ssm_c_im=ssm_c_im, ssm_d=ssm_d, w_glu=w_glu, b_glu=b_glu, w_out=w_out, w_pool=w_pool,
                   pool_scale=pool_scale, w_coarse=w_coarse, b_coarse=b_coarse, w_fine=w_fine, b_fine=b_fine,
                   w_gate=w_gate, w_up=w_up, w_down=w_down)
    prep = _prep_weights(weights)
    n_prompt = c_prompt.shape[0]
    mod_all = ada_params_all(jnp.concatenate([c_prompt, c_sample], axis=0), w_ada, b_ada)
    out_p = _run_prompt(x_prompt, mod_all[:, :n_prompt], prep)
    out_s = _run_sample(x_sample, mod_all[:, n_prompt:], prep, cache_k, cache_v, cache_kidx, state_ssm_re,
                        state_ssm_im, state_pool, page_table)
    return (out_p[0], out_s[0]) + out_p[1:] + out_s[1:]
```

```python
import functools
import math

import jax
import jax.numpy as jnp
from jax import lax
from jax.experimental import pallas as pl
from jax.experimental.pallas import tpu as pltpu

F32 = jnp.float32
BF16 = jnp.bfloat16
I32 = jnp.int32

D_MODEL = 1024
EPS = 1e-6
A_WIDTH = 512
SSM_GROUP = 16
SSM_GROUPS = 32
SSM_STATE = 64
N_STATES = SSM_GROUPS * SSM_STATE
HEAD_DIM = 64
N_HEADS = 8
N_KV_HEADS = 2
KV_WIDTH = N_KV_HEADS * HEAD_DIM
IDX_HEADS = 8
IDX_DIM = 64
TOPK_MAX = 256
ROPE_THETA = 500000.0
ROT_HALF = 8
PAGE_SIZE = 128
POOL_WINDOWS = (2, 4, 8, 16)
POOL_GROUP_DIM = 256
POOL_STATE = 15
HIST_ROWS = 16
MOE_GROUPS = 4
EXPERTS_PER_GROUP = 4
N_EXPERTS = 16
D_FF_EXPERT = 256

OFF_Q, OFF_K, OFF_V, OFF_QI, OFF_KI, OFF_WI, IN_COLS = 512, 1024, 1152, 1280, 1792, 1856, 1864
IN_COLS_PADDED = 1920

V7X_LANES = 128
V7X_SUBLANES = 8
V7X_VMEM_BYTES = 64 * 2**20
VMEM_LIMIT_BYTES = (V7X_VMEM_BYTES * 3) // 4
INT_MIN = -(2**31)
NEG_INF = float("-inf")


def _params(*semantics):
    return pltpu.CompilerParams(dimension_semantics=semantics, vmem_limit_bytes=VMEM_LIMIT_BYTES)


def _norm_mod(x, gain, scale, shift):
    var = jnp.mean(x * x, axis=-1, keepdims=True)
    return (x * lax.rsqrt(var + EPS)) * gain * (1.0 + scale) + shift


def _silu(x):
    return x * jax.nn.sigmoid(x)


def _dot(a, b):
    return jnp.dot(a, b, preferred_element_type=F32)


def _dot_t(a, b):
    return lax.dot_general(a, b, (((1,), (1,)), ((), ())), preferred_element_type=F32)


class TokenGroup:
    def __init__(self, n_rows, rows_per_mod, tile):
        assert n_rows % rows_per_mod == 0 and (rows_per_mod % tile == 0 or tile % rows_per_mod == 0)
        self.n_rows, self.tile = n_rows, tile
        self.n_tiles = n_rows // tile
        if rows_per_mod >= tile:
            self.mod_rows = 1
            self.tiles_per_mod = rows_per_mod // tile
        else:
            self.mod_rows = tile
            self.tiles_per_mod = 1

    def mods(self, mod):
        if self.mod_rows == 1:
            return mod[:, None, :]
        rep = self.n_rows // mod.shape[0]
        return jnp.repeat(mod, rep, axis=0).reshape(self.n_tiles, self.tile, mod.shape[1])

    def mod_spec(self, piece):
        tpm = self.tiles_per_mod
        return pl.BlockSpec((None, self.mod_rows, D_MODEL), lambda i: (i // tpm, 0, piece))

    def row_spec(self, width):
        return pl.BlockSpec((self.tile, width), lambda i: (i, 0))


def _const_spec(shape):
    nd = len(shape)
    return pl.BlockSpec(shape, lambda *_: (0,) * nd)


def _ada_kernel(c_ref, w_ref, b_ref, o_ref):
    s = _silu(c_ref[...]).astype(BF16)
    o_ref[...] = _dot(s, w_ref[...].astype(BF16)) + b_ref[...]


def ada_params_all(c_all, w_ada, b_ada):
    n_layers, d, n6 = w_ada.shape
    m = c_all.shape[0]
    tn = 1536
    return pl.pallas_call(
        _ada_kernel,
        grid=(n_layers, n6 // tn),
        in_specs=[pl.BlockSpec((m, d), lambda l, j: (0, 0)),
                  pl.BlockSpec((None, d, tn), lambda l, j: (l, 0, j)),
                  pl.BlockSpec((None, 1, tn), lambda l, j: (l, 0, j))],
        out_specs=pl.BlockSpec((None, m, tn), lambda l, j: (l, 0, j)),
        out_shape=jax.ShapeDtypeStruct((n_layers, m, n6), F32),
        compiler_params=_params("arbitrary", "arbitrary"),
    )(c_all, w_ada, b_ada.reshape(n_layers, 1, n6))


def _rope(x, cos, s1, s2):
    w = x.shape[1]
    reps = w // V7X_LANES
    if reps > 1:
        cos, s1, s2 = (jnp.tile(t, (1, reps)) for t in (cos, s1, s2))
    return x * cos + pltpu.roll(x, w - ROT_HALF, 1) * s1 + pltpu.roll(x, ROT_HALF, 1) * s2


def _head_norm(x, mavg, gain):
    sq = x * x
    hi = sq.astype(BF16)
    lo = (sq - hi.astype(F32)).astype(BF16)
    var = _dot(hi, mavg) + _dot(lo, mavg)
    return (x * lax.rsqrt(var + EPS)) * gain


def _in_proj_kernel(x_ref, sc_ref, sh_ref, g_ref, w_ref, qg_ref, kg_ref, mavg_ref, rope_ref,
                    u_ref, q_ref, k_ref, v_ref, qi_ref, kiwi_ref):
    h = _norm_mod(x_ref[...], g_ref[...], sc_ref[...], sh_ref[...])
    p = _dot(h.astype(BF16), w_ref[...])
    cos, s1, s2 = rope_ref[0], rope_ref[1], rope_ref[2]
    mavg = mavg_ref[...]
    u_ref[...] = p[:, 0:OFF_Q]
    q = _rope(_head_norm(p[:, OFF_Q:OFF_K], mavg, qg_ref[...]), cos, s1, s2)
    q_ref[...] = (q * HEAD_DIM ** -0.5).astype(BF16)
    k = _head_norm(p[:, OFF_K:OFF_V], mavg[0:KV_WIDTH, 0:KV_WIDTH], kg_ref[...])
    k_ref[...] = _rope(k, cos, s1, s2)
    v_ref[...] = p[:, OFF_V:OFF_QI]
    qi = _rope(p[:, OFF_QI:OFF_KI], cos, s1, s2)
    qi_ref[...] = (qi * IDX_DIM ** -0.5).astype(BF16)
    kiwi_ref[...] = _rope(p[:, OFF_KI:IN_COLS_PADDED], rope_ref[3], rope_ref[4], rope_ref[5])


def _rope_tables(pos):
    t = pos.shape[0]
    inv = ROPE_THETA ** (-jnp.arange(ROT_HALF, dtype=F32) / ROT_HALF)
    ang = pos.astype(F32)[:, None] * inv[None, :]
    cos, sin = jnp.cos(ang), jnp.sin(ang)
    rest = HEAD_DIM - 2 * ROT_HALF
    z8, zr, onesr = jnp.zeros((t, ROT_HALF), F32), jnp.zeros((t, rest), F32), jnp.ones((t, rest), F32)
    c64 = jnp.concatenate([cos, cos, onesr], axis=1)
    s1_64 = jnp.concatenate([-sin, z8, zr], axis=1)
    s2_64 = jnp.concatenate([z8, sin, zr], axis=1)
    z64 = jnp.zeros((t, HEAD_DIM), F32)
    wscale = jnp.concatenate([jnp.full((t, IDX_HEADS), IDX_HEADS ** -0.5, F32),
                              jnp.ones((t, HEAD_DIM - IDX_HEADS), F32)], axis=1)
    return jnp.stack([jnp.concatenate([c64, c64], axis=1), jnp.concatenate([s1_64, s1_64], axis=1),
                      jnp.concatenate([s2_64, s2_64], axis=1), jnp.concatenate([c64, wscale], axis=1),
                      jnp.concatenate([s1_64, z64], axis=1), jnp.concatenate([s2_64, z64], axis=1)])


def in_proj(x2, mods, grp, g_mix, w_in_p, q_gain, k_gain, mavg, rope):
    n = x2.shape[0]
    tm = grp.tile
    rope_tiles = rope.shape[1] // tm
    outs = pl.pallas_call(
        _in_proj_kernel,
        grid=(grp.n_tiles,),
        in_specs=[grp.row_spec(D_MODEL), grp.mod_spec(1), grp.mod_spec(0), _const_spec((1, D_MODEL)),
                  _const_spec((D_MODEL, IN_COLS_PADDED)), _const_spec((1, 512)), _const_spec((1, KV_WIDTH)),
                  _const_spec((512, 512)),
                  pl.BlockSpec((6, tm, V7X_LANES), lambda i: (0, i % rope_tiles, 0))],
        out_specs=[grp.row_spec(512), grp.row_spec(512), grp.row_spec(KV_WIDTH), grp.row_spec(KV_WIDTH),
                   grp.row_spec(512), grp.row_spec(V7X_LANES)],
        out_shape=[jax.ShapeDtypeStruct((n, 512), F32), jax.ShapeDtypeStruct((n, 512), BF16),
                   jax.ShapeDtypeStruct((n, KV_WIDTH), F32), jax.ShapeDtypeStruct((n, KV_WIDTH), F32),
                   jax.ShapeDtypeStruct((n, 512), BF16), jax.ShapeDtypeStruct((n, V7X_LANES), F32)],
        compiler_params=_params("arbitrary"),
    )(x2, mods, mods, g_mix, w_in_p, q_gain, k_gain, mavg, rope)
    return outs


S5_BATCH_BLOCK = V7X_SUBLANES
S5_HALF_IN = A_WIDTH // 2
S5_HALF_STATES = N_STATES // 2
S5_SCAN_COLS = 1024


def _gelu_tanh(x):
    cdf = 0.5 * (1.0 + jnp.tanh(math.sqrt(2.0 / math.pi) * (x + 0.044715 * (x * x * x))))
    return x * cdf


def _s5_kernel(u_ref, x0re_ref, x0im_ref, are_ref, aim_ref, bre_ref, bim_ref, cre_ref, cim_ref, d_ref,
               wglu_ref, bglu_ref, y_ref, sre_ref, sim_ref, st_re, st_im, xs_re, xs_im, *, tt):
    bb = S5_BATCH_BLOCK
    tb = pl.program_id(1)

    @pl.when(tb == 0)
    def _():
        st_re[...] = x0re_ref[...]
        st_im[...] = x0im_ref[...]

    u = u_ref[...].reshape(tt * bb, A_WIDTH)
    ub = u.astype(BF16)
    for half in range(2):
        rows = slice(half * S5_HALF_IN, (half + 1) * S5_HALF_IN)
        cols = slice(half * S5_HALF_STATES, (half + 1) * S5_HALF_STATES)
        xs_re[:, cols] = _dot(ub[:, rows], bre_ref[rows, cols])
        xs_im[:, cols] = _dot(ub[:, rows], bim_ref[rows, cols])

    for c in range(N_STATES // S5_SCAN_COLS):
        cs = slice(c * S5_SCAN_COLS, (c + 1) * S5_SCAN_COLS)
        a_re = are_ref[:, cs]
        a_im = aim_ref[:, cs]

        def step(t, carry, cs=cs, a_re=a_re, a_im=a_im):
            sr, si = carry
            r0 = pl.multiple_of(t * bb, bb)
            nr = a_re * sr - a_im * si + xs_re[pl.ds(r0, bb), cs]
            ni = a_re * si + a_im * sr + xs_im[pl.ds(r0, bb), cs]
            xs_re[pl.ds(r0, bb), cs] = nr
            xs_im[pl.ds(r0, bb), cs] = ni
            return nr, ni

        sr, si = lax.fori_loop(0, tt, step, (st_re[:, cs], st_im[:, cs]))
        st_re[:, cs] = sr
        st_im[:, cs] = si

    xr = xs_re[...].astype(BF16)
    xi = xs_im[...].astype(BF16)
    ys = []
    for half in range(2):
        rows = slice(half * S5_HALF_STATES, (half + 1) * S5_HALF_STATES)
        cols = slice(half * S5_HALF_IN, (half + 1) * S5_HALF_IN)
        ys.append(_dot(xr[:, rows], cre_ref[rows, cols]) - _dot(xi[:, rows], cim_ref[rows, cols]))
    y = jnp.concatenate(ys, axis=1) + d_ref[...] * u
    z = _gelu_tanh(y)
    o = z * jax.nn.sigmoid(_dot(z.astype(BF16), wglu_ref[...]) + bglu_ref[...])
    y_ref[...] = o.reshape(tt, bb, A_WIDTH)

    @pl.when(tb == pl.num_programs(1) - 1)
    def _():
        sre_ref[...] = st_re[...]
        sim_ref[...] = st_im[...]


def s5_discretise(lam_re, lam_im, log_dt, b_re, b_im, c_re, c_im):
    dt = jnp.exp(log_dt)[:, None]
    mag = jnp.exp(lam_re * dt)
    abar_re, abar_im = mag * jnp.cos(lam_im * dt), mag * jnp.sin(lam_im * dt)
    den = lam_re * lam_re + lam_im * lam_im
    ir, ii = lam_re / den, -lam_im / den
    cr = (abar_re - 1.0) * ir - abar_im * ii
    ci = (abar_re - 1.0) * ii + abar_im * ir
    bb_re = cr[..., None] * b_re - ci[..., None] * b_im
    bb_im = cr[..., None] * b_im + ci[..., None] * b_re
    eye = jnp.eye(SSM_GROUPS, dtype=F32)
    pack_b = lambda bb: jnp.einsum("gpn,gh->gnhp", bb, eye).reshape(A_WIDTH, N_STATES).astype(BF16)
    pack_c = lambda cc: jnp.einsum("gnp,gh->gphn", cc, eye).reshape(N_STATES, A_WIDTH).astype(BF16)
    bcast = lambda a: jnp.broadcast_to(a.reshape(1, N_STATES), (S5_BATCH_BLOCK, N_STATES))
    return bcast(abar_re), bcast(abar_im), pack_b(bb_re), pack_b(bb_im), pack_c(c_re), pack_c(c_im)


def s5_mixer(u3, x0_re, x0_im, disc, d_skip, w_glu, b_glu, tt):
    t, b, _ = u3.shape
    bb = S5_BATCH_BLOCK
    a_re, a_im, bre, bim, cre, cim = disc
    state_spec = pl.BlockSpec((bb, N_STATES), lambda i, j: (i, 0))
    seq_spec = pl.BlockSpec((tt, bb, A_WIDTH), lambda i, j: (j, i, 0))
    return pl.pallas_call(
        functools.partial(_s5_kernel, tt=tt),
        grid=(b // bb, t // tt),
        in_specs=[seq_spec, state_spec, state_spec, _const_spec((bb, N_STATES)), _const_spec((bb, N_STATES)),
                  _const_spec((A_WIDTH, N_STATES)), _const_spec((A_WIDTH, N_STATES)),
                  _const_spec((N_STATES, A_WIDTH)), _const_spec((N_STATES, A_WIDTH)),
                  _const_spec((1, A_WIDTH)), _const_spec((A_WIDTH, A_WIDTH)), _const_spec((1, A_WIDTH))],
        out_specs=[seq_spec, state_spec, state_spec],
        out_shape=[jax.ShapeDtypeStruct((t, b, A_WIDTH), F32), jax.ShapeDtypeStruct((b, N_STATES), F32),
                   jax.ShapeDtypeStruct((b, N_STATES), F32)],
        scratch_shapes=[pltpu.VMEM((bb, N_STATES), F32), pltpu.VMEM((bb, N_STATES), F32),
                        pltpu.VMEM((tt * bb, N_STATES), F32), pltpu.VMEM((tt * bb, N_STATES), F32)],
        compiler_params=_params("arbitrary", "arbitrary"),
    )(u3, x0_re, x0_im, a_re, a_im, bre, bim, cre, cim, d_skip, w_glu, b_glu)


def _count(mask):
    return jnp.sum(jnp.where(mask, 1.0, 0.0), axis=1, keepdims=True)


def _topk_bias(sc, keep, qpos0, key_ref, bias_ref):
    m, s_keys = sc.shape
    keep = float(keep)
    kpos = lax.broadcasted_iota(I32, (m, s_keys), 1)
    qpos = qpos0 + lax.broadcasted_iota(I32, (m, 1), 0)
    causal = kpos <= qpos
    bits = lax.bitcast_convert_type(sc, I32)
    mag = bits & 0x7FFFFFFF
    key_ref[...] = jnp.where(causal, jnp.where(bits < 0, -mag, mag), INT_MIN)

    def bit_step(it, thr):
        cand = thr + lax.shift_left(jnp.int32(1), 31 - it)
        return jnp.where(_count(key_ref[...] >= cand) >= keep, cand, thr)

    thr = lax.fori_loop(0, 32, bit_step, jnp.full((m, 1), INT_MIN, I32))
    key = key_ref[...]
    gt = key > thr
    eq = key == thr
    need = keep - _count(gt)
    bias_ref[...] = jnp.where((gt | eq) & causal, 0.0, NEG_INF)
    tie = (_count(eq) > need) & (thr > INT_MIN)

    @pl.when(jnp.max(jnp.where(tie, 1.0, 0.0)) > 0.0)
    def _():
        def idx_step(_, lohi):
            lo, hi = lohi
            mid = (lo + hi) >> 1
            ok = _count((key_ref[...] == thr) & (kpos <= mid)) >= need
            return jnp.where(ok, lo, mid + 1), jnp.where(ok, mid, hi)

        n_steps = max(1, (s_keys - 1).bit_length())
        last, _ = lax.fori_loop(0, n_steps, idx_step,
                                (jnp.zeros((m, 1), I32), jnp.full((m, 1), s_keys - 1, I32)))
        k2 = key_ref[...]
        sel = (k2 > thr) | ((k2 == thr) & (kpos <= last))
        bias_ref[...] = jnp.where(sel & causal, 0.0, NEG_INF)


def _select_attend(s_keys, keep, qpos0, q, qi, wi, ki, k, v, key_ref, bias_ref, o_ref):
    m = q.shape[0]
    sc = jnp.zeros((m, s_keys), F32)
    for h in range(IDX_HEADS):
        d = _dot_t(qi[:, h * IDX_DIM:(h + 1) * IDX_DIM], ki)
        sc = sc + jnp.maximum(d, 0.0) * wi[:, h:h + 1]
    _topk_bias(sc, keep, qpos0, key_ref, bias_ref)
    bias = bias_ref[...]
    group = N_HEADS // N_KV_HEADS
    for g in range(N_KV_HEADS):
        kg = k[:, g * HEAD_DIM:(g + 1) * HEAD_DIM]
        vg = v[:, g * HEAD_DIM:(g + 1) * HEAD_DIM]
        for hh in range(g * group, (g + 1) * group):
            s = _dot_t(q[:, hh * HEAD_DIM:(hh + 1) * HEAD_DIM], kg) + bias
            p = jnp.exp(s - jnp.max(s, axis=1, keepdims=True))
            l = jnp.sum(p, axis=1, keepdims=True)
            o = _dot(p.astype(BF16), vg) / l
            o_ref[:, hh * HEAD_DIM:(hh + 1) * HEAD_DIM] = o.astype(o_ref.dtype)


def _attn_prompt_kernel(q_ref, qi_ref, wi_ref, k_ref, v_ref, ki_ref, o_ref, key_ref, bias_ref, *, tq, buckets,
                        keep):
    i = pl.program_id(1)
    need_keys = (i + 1) * tq
    lo = 0
    for s_keys in buckets:
        @pl.when((need_keys > lo) & (need_keys <= s_keys))
        def _(s_keys=s_keys):
            _select_attend(s_keys, keep, i * tq, q_ref[...], qi_ref[...], wi_ref[:, IDX_DIM:IDX_DIM + IDX_HEADS],
                           ki_ref[0:s_keys, 0:IDX_DIM].astype(BF16), k_ref[0:s_keys, :].astype(BF16),
                           v_ref[0:s_keys, :].astype(BF16), key_ref.at[:, 0:s_keys], bias_ref.at[:, 0:s_keys],
                           o_ref)
        lo = s_keys


def attend_prompt(q, qi, kiwi, k, v, batch, seq, tq=128, n_buckets=4):
    n = q.shape[0]
    nq = seq // tq
    buckets = tuple(seq * (j + 1) // n_buckets for j in range(n_buckets))
    q_spec = lambda w: pl.BlockSpec((tq, w), lambda b, i: (b * nq + i, 0))
    kv_spec = pl.BlockSpec((seq, KV_WIDTH), lambda b, i: (b, 0))
    return pl.pallas_call(
        functools.partial(_attn_prompt_kernel, tq=tq, buckets=buckets, keep=min(TOPK_MAX, seq // 4)),
        grid=(batch, nq),
        in_specs=[q_spec(512), q_spec(512), q_spec(V7X_LANES), kv_spec, kv_spec, kv_spec],
        out_specs=q_spec(512),
        out_shape=jax.ShapeDtypeStruct((n, 512), BF16),
        scratch_shapes=[pltpu.VMEM((tq, seq), I32), pltpu.VMEM((tq, seq), F32)],
        compiler_params=_params("arbitrary", "arbitrary"),
    )(q, qi, kiwi, k, v, kiwi)


SAMPLE_Q_ROWS = V7X_SUBLANES


def _attn_sample_kernel(pt_ref, q_ref, qi_ref, wi_ref, knew_ref, vnew_ref, kinew_ref, ck_hbm, cv_hbm, cki_hbm,
                        o_ref, k_all, v_all, ki_all, sems, key_ref, bias_ref, *, layer, n_pages, keep):
    b = pl.program_id(0)
    slot = b % 2
    past = n_pages * PAGE_SIZE

    def page_copies(seq, sl):
        copies = []
        for p in range(n_pages):
            phys = pt_ref[seq, p]
            window = pl.ds(p * PAGE_SIZE, PAGE_SIZE)
            copies.append(pltpu.make_async_copy(ck_hbm.at[layer, phys], k_all.at[sl, :, window], sems.at[sl, 0]))
            copies.append(pltpu.make_async_copy(cv_hbm.at[layer, phys], v_all.at[sl, :, window], sems.at[sl, 1]))
            copies.append(pltpu.make_async_copy(cki_hbm.at[layer, phys], ki_all.at[sl, :, window], sems.at[sl, 2]))
        return copies

    @pl.when(b == 0)
    def _():
        for c in page_copies(0, 0):
            c.start()

    @pl.when(b + 1 < pl.num_programs(0))
    def _():
        for c in page_copies(b + 1, 1 - slot):
            c.start()

    k_all[slot, :, past:past + PAGE_SIZE] = knew_ref[...]
    v_all[slot, :, past:past + PAGE_SIZE] = vnew_ref[...]
    ki_all[slot, :, past:past + PAGE_SIZE] = kinew_ref[...]
    for c in page_copies(b, slot):
        c.wait()
    rows = SAMPLE_Q_ROWS
    d = _dot(qi_ref[...], ki_all[slot].astype(BF16))
    wi = wi_ref[:, IDX_DIM:IDX_DIM + IDX_HEADS]
    sc = jnp.zeros((rows, past + PAGE_SIZE), F32)
    for h in range(IDX_HEADS):
        sc = sc + jnp.maximum(d[h * rows:(h + 1) * rows, :], 0.0) * wi[:, h:h + 1]
    _topk_bias(sc, keep, past, key_ref, bias_ref)
    group = N_HEADS // N_KV_HEADS
    bias = jnp.tile(bias_ref[...], (group, 1))
    q = q_ref[...]
    for g in range(N_KV_HEADS):
        feats = slice(g * HEAD_DIM, (g + 1) * HEAD_DIM)
        qrows = slice(g * group * rows, (g + 1) * group * rows)
        s = _dot(q[qrows, :], k_all[slot, feats, :].astype(BF16)) + bias
        p = jnp.exp(s - jnp.max(s, axis=1, keepdims=True))
        l = jnp.sum(p, axis=1, keepdims=True)
        o_ref[qrows, :] = _dot_t(p.astype(BF16), v_all[slot, feats, :].astype(BF16)) / l


def attend_sample(q_st, qi_st, kiwi3, k_new_t, v_new_t, ki_new_t, cache_k_t, cache_v_t, cache_ki_t, page_table,
                  layer, seq):
    b = q_st.shape[0]
    n_pages = page_table.shape[1]
    s_keys = (n_pages + 1) * PAGE_SIZE
    rows = SAMPLE_Q_ROWS
    st_spec = pl.BlockSpec((None, N_HEADS * rows, HEAD_DIM), lambda i, pt: (i, 0, 0))
    q_spec = lambda w: pl.BlockSpec((None, rows, w), lambda i, pt: (i, 0, 0))
    new_spec = lambda f: pl.BlockSpec((None, f, PAGE_SIZE), lambda i, pt: (i, 0, 0))
    hbm = pl.BlockSpec(memory_space=pl.ANY)
    grid_spec = pltpu.PrefetchScalarGridSpec(
        num_scalar_prefetch=1,
        grid=(b,),
        in_specs=[st_spec, st_spec, q_spec(V7X_LANES), new_spec(KV_WIDTH), new_spec(KV_WIDTH),
                  new_spec(IDX_DIM), hbm, hbm, hbm],
        out_specs=st_spec,
        scratch_shapes=[pltpu.VMEM((2, KV_WIDTH, s_keys), F32), pltpu.VMEM((2, KV_WIDTH, s_keys), F32),
                        pltpu.VMEM((2, IDX_DIM, s_keys), F32), pltpu.SemaphoreType.DMA((2, 3)),
                        pltpu.VMEM((rows, s_keys), I32), pltpu.VMEM((rows, s_keys), F32)],
    )
    keep = min(TOPK_MAX, (n_pages * PAGE_SIZE + seq) // 4)
    return pl.pallas_call(
        functools.partial(_attn_sample_kernel, layer=layer, n_pages=n_pages, keep=keep),
        grid_spec=grid_spec,
        out_shape=jax.ShapeDtypeStruct((b, N_HEADS * rows, HEAD_DIM), F32),
        compiler_params=_params("arbitrary"),
    )(page_table, q_st, qi_st, kiwi3, k_new_t, v_new_t, ki_new_t, cache_k_t, cache_v_t, cache_ki_t)


def _out_proj_kernel(x_ref, ya_ref, yb_ref, gate_ref, w_ref, o_ref):
    y = _dot(ya_ref[...].astype(BF16), w_ref[0:A_WIDTH, :]) + _dot(yb_ref[...], w_ref[A_WIDTH:2 * A_WIDTH, :])
    o_ref[...] = x_ref[...] + gate_ref[...] * y


def out_proj(x2, y_a, y_b, mods, grp, w_out):
    return pl.pallas_call(
        _out_proj_kernel,
        grid=(grp.n_tiles,),
        in_specs=[grp.row_spec(D_MODEL), grp.row_spec(A_WIDTH), grp.row_spec(512), grp.mod_spec(2),
                  _const_spec((2 * A_WIDTH, D_MODEL))],
        out_specs=grp.row_spec(D_MODEL),
        out_shape=jax.ShapeDtypeStruct(x2.shape, F32),
        compiler_params=_params("arbitrary"),
    )(x2, y_a, y_b, mods, w_out)


def _pool_kernel(x_ref, hist_ref, sc_ref, sh_ref, gate_ref, g_ref, w_ref, ps_ref, o_ref, st_ref, *,
                 tiles_per_seq, hist_is_x, pos_base, state_rows):
    i = pl.program_id(0)
    tm = x_ref.shape[0]
    x = x_ref[...]
    h = _norm_mod(x, g_ref[...], sc_ref[...], sh_ref[...])
    if hist_is_x:
        hist = _norm_mod(hist_ref[...], g_ref[...], sc_ref[...], sh_ref[...])
        hist = jnp.where(i % tiles_per_seq == 0, 0.0, hist)
    else:
        hist = hist_ref[...]
    ext = jnp.concatenate([hist, h], axis=0)
    pos = pos_base + (i % tiles_per_seq) * tm + lax.broadcasted_iota(I32, (tm, 1), 0)
    ys = []
    for g, win in enumerate(POOL_WINDOWS):
        cols = slice(g * POOL_GROUP_DIM, (g + 1) * POOL_GROUP_DIM)
        s = ext[:, cols]
        shift = 1
        while shift < win:
            s = s + pltpu.roll(s, shift, 0)
            shift *= 2
        cnt = jnp.minimum(pos + 1, win).astype(F32)
        pooled = s[HIST_ROWS:, :] / cnt - h[:, cols]
        ys.append(_dot(pooled.astype(BF16), w_ref[g]))
    y = jnp.concatenate(ys, axis=1) * ps_ref[...]
    o_ref[...] = x + gate_ref[...] * y
    st_ref[...] = ext[HIST_ROWS + tm - state_rows:, :]


def pool_mixer(x2, hist, mods, grp, g_mix, w_pool, pool_scale, *, tiles_per_seq, hist_is_x, pos_base, state_rows):
    tm = grp.tile
    n_seq = grp.n_tiles // tiles_per_seq
    if hist_is_x:
        per_tile = tm // HIST_ROWS
        hist_spec = pl.BlockSpec((HIST_ROWS, D_MODEL), lambda i: (jnp.maximum(i * per_tile - 1, 0), 0))
    else:
        hist_spec = pl.BlockSpec((None, HIST_ROWS, D_MODEL), lambda i: (i, 0, 0))
    return pl.pallas_call(
        functools.partial(_pool_kernel, tiles_per_seq=tiles_per_seq, hist_is_x=hist_is_x, pos_base=pos_base,
                          state_rows=state_rows),
        grid=(grp.n_tiles,),
        in_specs=[grp.row_spec(D_MODEL), hist_spec, grp.mod_spec(1), grp.mod_spec(0), grp.mod_spec(2),
                  _const_spec((1, D_MODEL)), _const_spec((len(POOL_WINDOWS), POOL_GROUP_DIM, POOL_GROUP_DIM)),
                  _const_spec((1, D_MODEL))],
        out_specs=[grp.row_spec(D_MODEL),
                   pl.BlockSpec((None, state_rows, D_MODEL), lambda i: (i // tiles_per_seq, 0, 0))],
        out_shape=[jax.ShapeDtypeStruct(x2.shape, F32), jax.ShapeDtypeStruct((n_seq, state_rows, D_MODEL), F32)],
        compiler_params=_params("arbitrary"),
    )(x2, hist, mods, mods, mods, g_mix, w_pool, pool_scale)


ROUTER_COLS = V7X_LANES


MOE_GROUP_LANE = EXPERTS_PER_GROUP
MOE_CHUNK = 256
MOE_SEG_ALIGN = V7X_SUBLANES


def _router_gates(logits):
    col = lambda j: logits[:, j:j + 1]
    lc = [col(j) for j in range(MOE_GROUPS)]
    mc = functools.reduce(jnp.maximum, lc)
    pg = 1.0 / functools.reduce(lambda a, b: a + b, [jnp.exp(l - mc) for l in lc])
    grp = jnp.where(lc[0] == mc, 0, jnp.where(lc[1] == mc, 1, jnp.where(lc[2] == mc, 2, 3)))
    fl = []
    for j in range(EXPERTS_PER_GROUP):
        cands = [col(MOE_GROUPS + g * EXPERTS_PER_GROUP + j) for g in range(MOE_GROUPS)]
        fl.append(jnp.where(grp == 0, cands[0], jnp.where(grp == 1, cands[1],
                                                          jnp.where(grp == 2, cands[2], cands[3]))))
    first = lambda vals, mx: jnp.where(vals[0] == mx, 0, jnp.where(vals[1] == mx, 1, jnp.where(vals[2] == mx, 2, 3)))
    m1 = functools.reduce(jnp.maximum, fl)
    i1 = first(fl, m1)
    rest = [jnp.where(i1 == j, NEG_INF, fl[j]) for j in range(EXPERTS_PER_GROUP)]
    m2 = functools.reduce(jnp.maximum, rest)
    i2 = first(rest, m2)
    e2 = jnp.exp(m2 - m1)
    w1 = pg / (1.0 + e2)
    w2 = pg * e2 / (1.0 + e2)
    return grp, [jnp.where(i1 == j, w1, jnp.where(i2 == j, w2, 0.0)) for j in range(EXPERTS_PER_GROUP)]


def _route_kernel(x_ref, sc_ref, sh_ref, g_ref, wr_ref, br_ref, o_ref):
    h = _norm_mod(x_ref[...], g_ref[...], sc_ref[...], sh_ref[...])
    logits = jnp.dot(h, wr_ref[...], preferred_element_type=F32, precision=lax.Precision.HIGHEST) + br_ref[...]
    grp, gates = _router_gates(logits)
    lane = lax.broadcasted_iota(I32, o_ref.shape, 1)
    out = jnp.where(lane == MOE_GROUP_LANE, grp.astype(F32), 0.0)
    for j, gj in enumerate(gates):
        out = jnp.where(lane == j, gj, out)
    o_ref[...] = out


def moe_route(x2, mods, grp, g_ffn, w_router, b_router):
    return pl.pallas_call(
        _route_kernel,
        grid=(grp.n_tiles,),
        in_specs=[grp.row_spec(D_MODEL), grp.mod_spec(4), grp.mod_spec(3), _const_spec((1, D_MODEL)),
                  _const_spec((D_MODEL, ROUTER_COLS)), _const_spec((1, ROUTER_COLS))],
        out_specs=grp.row_spec(ROUTER_COLS),
        out_shape=jax.ShapeDtypeStruct((x2.shape[0], ROUTER_COLS), F32),
        compiler_params=_params("arbitrary"),
    )(x2, mods, mods, g_ffn, w_router, b_router)


def _route_plan(rout, tm):
    n_tiles = rout.shape[0] // tm
    grp = rout[:, MOE_GROUP_LANE].astype(I32).reshape(n_tiles, tm)
    onehot = (grp[:, :, None] == jnp.arange(MOE_GROUPS, dtype=I32)[None, None, :]).astype(I32)
    cnt = jnp.sum(onehot, axis=1)
    padded = (cnt + MOE_SEG_ALIGN - 1) // MOE_SEG_ALIGN * MOE_SEG_ALIGN
    start = jnp.cumsum(padded, axis=1) - padded
    rank = jnp.take_along_axis(jnp.cumsum(onehot, axis=1) - onehot, grp[:, :, None], axis=2)[:, :, 0]
    dest = jnp.take_along_axis(start, grp, axis=1) + rank
    return dest.reshape(-1), start.reshape(-1), cnt.reshape(-1)


def _moe_kernel(dest_ref, start_ref, cnt_ref, x_ref, sc_ref, sh_ref, gate_ref, g_ref, rout_ref, wg_ref, wu_ref,
                wd_ref, o_ref, hs_ref, gs_ref, ys_ref, *, chunk):
    i = pl.program_id(0)
    g = pl.program_id(1)
    tm = x_ref.shape[0]
    base = i * tm

    @pl.when((i == 0) & (g == 0))
    def _():
        hs_ref[...] = jnp.zeros_like(hs_ref)
        gs_ref[...] = jnp.zeros_like(gs_ref)

    @pl.when(g == 0)
    def _():
        o_ref[...] = _norm_mod(x_ref[...], g_ref[...], sc_ref[...], sh_ref[...])

        def move(r, carry):
            d = dest_ref[base + r]
            hs_ref[pl.ds(d, 1), :] = o_ref[pl.ds(r, 1), :]
            gs_ref[pl.ds(d, 1), :] = rout_ref[pl.ds(r, 1), :]
            return carry

        lax.fori_loop(0, tm, move, 0, unroll=8)

    start = start_ref[i * MOE_GROUPS + g]
    cnt = cnt_ref[i * MOE_GROUPS + g]

    def run_chunk(c, carry):
        r0 = pl.multiple_of(start + c * chunk, MOE_SEG_ALIGN)
        xs = hs_ref[pl.ds(r0, chunk), :].astype(BF16)
        gates = gs_ref[pl.ds(r0, chunk), :]
        valid = c * chunk + lax.broadcasted_iota(I32, (chunk, 1), 0) < cnt
        y = jnp.zeros((chunk, D_MODEL), F32)
        for e in range(EXPERTS_PER_GROUP):
            hid = _silu(_dot(xs, wg_ref[e])) * _dot(xs, wu_ref[e]) * gates[:, e:e + 1]
            y = y + _dot(jnp.where(valid, hid, 0.0).astype(BF16), wd_ref[e])
        ys_ref[pl.ds(r0, chunk), :] = y
        return carry

    lax.fori_loop(0, (cnt + chunk - 1) // chunk, run_chunk, 0)

    @pl.when(g == pl.num_programs(1) - 1)
    def _():
        def move_back(r, carry):
            d = dest_ref[base + r]
            o_ref[pl.ds(r, 1), :] = ys_ref[pl.ds(d, 1), :]
            return carry

        lax.fori_loop(0, tm, move_back, 0, unroll=8)
        o_ref[...] = x_ref[...] + gate_ref[...] * o_ref[...]


def hier_moe(x2, mods, grp, g_ffn, w_router, b_router, w_gate, w_up, w_down):
    tm = grp.tile
    chunk = min(MOE_CHUNK, tm)
    rout = moe_route(x2, mods, grp, g_ffn, w_router, b_router)
    dest, start, cnt = _route_plan(rout, tm)
    buf_rows = tm + chunk + MOE_GROUPS * MOE_SEG_ALIGN
    row = lambda w: pl.BlockSpec((tm, w), lambda i, g, *_: (i, 0))
    tpm = grp.tiles_per_mod
    mod_spec = lambda piece: pl.BlockSpec((None, grp.mod_rows, D_MODEL), lambda i, g, *_: (i // tpm, 0, piece))
    epg = EXPERTS_PER_GROUP
    grid_spec = pltpu.PrefetchScalarGridSpec(
        num_scalar_prefetch=3,
        grid=(grp.n_tiles, MOE_GROUPS),
        in_specs=[row(D_MODEL), mod_spec(4), mod_spec(3), mod_spec(5),
                  pl.BlockSpec((1, D_MODEL), lambda i, g, *_: (0, 0)), row(ROUTER_COLS),
                  pl.BlockSpec((epg, D_MODEL, D_FF_EXPERT), lambda i, g, *_: (g, 0, 0)),
                  pl.BlockSpec((epg, D_MODEL, D_FF_EXPERT), lambda i, g, *_: (g, 0, 0)),
                  pl.BlockSpec((epg, D_FF_EXPERT, D_MODEL), lambda i, g, *_: (g, 0, 0))],
        out_specs=row(D_MODEL),
        scratch_shapes=[pltpu.VMEM((buf_rows, D_MODEL), F32), pltpu.VMEM((buf_rows, ROUTER_COLS), F32),
                        pltpu.VMEM((buf_rows, D_MODEL), F32)],
    )
    return pl.pallas_call(
        functools.partial(_moe_kernel, chunk=chunk),
        grid_spec=grid_spec,
        out_shape=jax.ShapeDtypeStruct(x2.shape, F32),
        compiler_params=_params("arbitrary", "arbitrary"),
    )(dest, start, cnt, x2, mods, mods, mods, g_ffn, rout, w_gate, w_up, w_down)


def _prep_weights(w):
    depth = w["w_ada"].shape[0]
    n_ab = w["w_in"].shape[0]
    prep = {"layers": [], "ab": [], "c": []}
    blk = jnp.arange(512) // HEAD_DIM
    prep["mavg"] = jnp.where(blk[:, None] == blk[None, :], 1.0 / HEAD_DIM, 0.0).astype(BF16)
    for l in range(depth):
        wr = jnp.concatenate([w["w_coarse"][l], w["w_fine"][l]], axis=1)
        br = jnp.concatenate([w["b_coarse"][l], w["b_fine"][l]])
        pad = ROUTER_COLS - wr.shape[1]
        prep["layers"].append(dict(
            g_mix=w["g_mix"][l][None, :], g_ffn=w["g_ffn"][l][None, :],
            w_router=jnp.pad(wr, ((0, 0), (0, pad))), b_router=jnp.pad(br, (0, pad))[None, :],
            w_gate=w["w_gate"][l].astype(BF16), w_up=w["w_up"][l].astype(BF16), w_down=w["w_down"][l].astype(BF16)))
    for i in range(n_ab):
        prep["ab"].append(dict(
            w_in=jnp.pad(w["w_in"][i], ((0, 0), (0, IN_COLS_PADDED - IN_COLS))).astype(BF16),
            q_gain=jnp.tile(w["q_gain"][i], N_HEADS)[None, :], k_gain=jnp.tile(w["k_gain"][i], N_KV_HEADS)[None, :],
            disc=s5_discretise(w["lam_re"][i], w["lam_im"][i], w["log_dt"][i], w["ssm_b_re"][i], w["ssm_b_im"][i],
                               w["ssm_c_re"][i], w["ssm_c_im"][i]),
            d_skip=w["ssm_d"][i][None, :], w_glu=w["w_glu"][i].astype(BF16), b_glu=w["b_glu"][i][None, :],
            w_out=w["w_out"][i].astype(BF16)))
    for i in range(w["w_pool"].shape[0]):
        prep["c"].append(dict(w_pool=w["w_pool"][i].astype(BF16), pool_scale=w["pool_scale"][i][None, :]))
    return prep


def _run_prompt(x, mod_all, prep):
    batch, seq, _ = x.shape
    n = batch * seq
    x2 = x.reshape(n, D_MODEL)
    rope = _rope_tables(jnp.arange(seq, dtype=I32))
    tile = 512
    grp = TokenGroup(n, seq, tile)
    grp_moe = TokenGroup(n, seq, 1024)
    zero_state = jnp.zeros((batch, N_STATES), F32)
    ks, vs, kis, sres, sims, pools = [], [], [], [], [], []
    for layer in range(mod_all.shape[0]):
        i = layer // 2
        lw = prep["layers"][layer]
        mods = grp.mods(mod_all[layer])
        if layer % 2 == 0:
            ab = prep["ab"][i]
            u, q, k, v, qi, kiwi = in_proj(x2, mods, grp, lw["g_mix"], ab["w_in"], ab["q_gain"], ab["k_gain"],
                                           prep["mavg"], rope)
            u3 = jnp.swapaxes(u.reshape(batch, seq, A_WIDTH), 0, 1)
            y3, s_re, s_im = s5_mixer(u3, zero_state, zero_state, ab["disc"], ab["d_skip"], ab["w_glu"],
                                      ab["b_glu"], tt=64)
            y_a = jnp.swapaxes(y3, 0, 1).reshape(n, A_WIDTH)
            y_b = attend_prompt(q, qi, kiwi, k, v, batch, seq)
            x2 = out_proj(x2, y_a, y_b, mods, grp, ab["w_out"])
            ks.append(k.reshape(batch, seq, N_KV_HEADS, HEAD_DIM))
            vs.append(v.reshape(batch, seq, N_KV_HEADS, HEAD_DIM))
            kis.append(kiwi[:, :IDX_DIM].reshape(batch, seq, IDX_DIM))
            sres.append(s_re.reshape(batch, SSM_GROUPS, SSM_STATE))
            sims.append(s_im.reshape(batch, SSM_GROUPS, SSM_STATE))
        else:
            c = prep["c"][i]
            x2, st = pool_mixer(x2, x2, mods, grp, lw["g_mix"], c["w_pool"], c["pool_scale"],
                                tiles_per_seq=seq // tile, hist_is_x=True, pos_base=0, state_rows=HIST_ROWS)
            pools.append(st[:, HIST_ROWS - POOL_STATE:, :])
        x2 = hier_moe(x2, grp_moe.mods(mod_all[layer]), grp_moe, lw["g_ffn"], lw["w_router"], lw["b_router"],
                      lw["w_gate"], lw["w_up"], lw["w_down"])
    return (x2.reshape(batch, seq, D_MODEL), jnp.stack(ks), jnp.stack(vs), jnp.stack(kis), jnp.stack(sres),
            jnp.stack(sims), jnp.stack(pools))


def _run_sample(x, mod_all, prep, cache_k, cache_v, cache_kidx, state_re, state_im, state_pool, page_table):
    batch, seq, _ = x.shape
    n = batch * seq
    past_len = page_table.shape[1] * PAGE_SIZE
    x2 = x.reshape(n, D_MODEL)
    rope = _rope_tables(jnp.tile(past_len + jnp.arange(seq, dtype=I32), batch))
    grp = TokenGroup(n, seq, n)
    rows = SAMPLE_Q_ROWS
    grp_pool = TokenGroup(batch * rows, rows, rows)
    pad_q = lambda a: jnp.pad(a.reshape(batch, seq, a.shape[-1]), ((0, 0), (0, rows - seq), (0, 0)))
    page_t = lambda a: jnp.pad(jnp.swapaxes(a.reshape(batch, seq, a.shape[-1]), 1, 2),
                               ((0, 0), (0, 0), (0, PAGE_SIZE - seq)))
    stack_heads = lambda a: jnp.swapaxes(pad_q(a).reshape(batch, rows, N_HEADS, HEAD_DIM), 1, 2).reshape(
        batch, N_HEADS * rows, HEAD_DIM)
    n_layers_ab, n_phys = cache_k.shape[:2]
    cache_k_t = jnp.transpose(cache_k, (0, 1, 3, 4, 2)).reshape(n_layers_ab, n_phys, KV_WIDTH, PAGE_SIZE)
    cache_v_t = jnp.transpose(cache_v, (0, 1, 3, 4, 2)).reshape(n_layers_ab, n_phys, KV_WIDTH, PAGE_SIZE)
    cache_ki_t = jnp.transpose(cache_kidx, (0, 1, 3, 2))
    ks, vs, kis, sres, sims, pools = [], [], [], [], [], []
    for layer in range(mod_all.shape[0]):
        i = layer // 2
        lw = prep["layers"][layer]
        mods = grp.mods(mod_all[layer])
        if layer % 2 == 0:
            ab = prep["ab"][i]
            u, q, k, v, qi, kiwi = in_proj(x2, mods, grp, lw["g_mix"], ab["w_in"], ab["q_gain"], ab["k_gain"],
                                           prep["mavg"], rope)
            u3 = jnp.swapaxes(u.reshape(batch, seq, A_WIDTH), 0, 1)
            y3, s_re, s_im = s5_mixer(u3, state_re[i].reshape(batch, N_STATES), state_im[i].reshape(batch, N_STATES),
                                      ab["disc"], ab["d_skip"], ab["w_glu"], ab["b_glu"], tt=seq)
            y_a = jnp.swapaxes(y3, 0, 1).reshape(n, A_WIDTH)
            o_st = attend_sample(stack_heads(q), stack_heads(qi), pad_q(kiwi), page_t(k), page_t(v),
                                 page_t(kiwi[:, :IDX_DIM]), cache_k_t, cache_v_t, cache_ki_t, page_table, i, seq)
            y_b = jnp.swapaxes(o_st.reshape(batch, N_HEADS, rows, HEAD_DIM), 1, 2)[:, :seq]
            x2 = out_proj(x2, y_a, y_b.reshape(n, 512).astype(BF16), mods, grp, ab["w_out"])
            ks.append(k.reshape(batch, seq, N_KV_HEADS, HEAD_DIM))
            vs.append(v.reshape(batch, seq, N_KV_HEADS, HEAD_DIM))
            kis.append(kiwi[:, :IDX_DIM].reshape(batch, seq, IDX_DIM))
            sres.append(s_re.reshape(batch, SSM_GROUPS, SSM_STATE))
            sims.append(s_im.reshape(batch, SSM_GROUPS, SSM_STATE))
        else:
            c = prep["c"][i]
            xp = jnp.pad(x2.reshape(batch, seq, D_MODEL), ((0, 0), (0, rows - seq), (0, 0)))
            hist = jnp.pad(state_pool[i], ((0, 0), (HIST_ROWS - POOL_STATE, 0), (0, 0)))
            xo, st = pool_mixer(xp.reshape(batch * rows, D_MODEL), hist, grp_pool.mods(mod_all[layer]), grp_pool,
                                lw["g_mix"], c["w_pool"], c["pool_scale"], tiles_per_seq=1, hist_is_x=False,
                                pos_base=past_len, state_rows=HIST_ROWS + rows)
            x2 = xo.reshape(batch, rows, D_MODEL)[:, :seq].reshape(n, D_MODEL)
            pools.append(st[:, seq + HIST_ROWS - POOL_STATE:seq + HIST_ROWS, :])
        x2 = hier_moe(x2, mods, grp, lw["g_ffn"], lw["w_router"], lw["b_router"], lw["w_gate"], lw["w_up"],
                      lw["w_down"])
    return (x2.reshape(batch, seq, D_MODEL), jnp.stack(ks), jnp.stack(vs), jnp.stack(kis), jnp.stack(sres),
            jnp.stack(sims), jnp.stack(pools))


def kernel(x_prompt, x_sample, c_prompt, c_sample, cache_k, cache_v, cache_kidx, state_ssm_re, state_ssm_im,
           state_pool, page_table, w_ada, b_ada, g_mix, g_ffn, w_in, q_gain, k_gain, lam_re, lam_im, log_dt,
           ssm_b_re, ssm_b_im, ssm_c_re, ssm_c_im, ssm_d, w_glu, b_glu, w_out, w_pool, pool_scale, w_coarse,
           b_coarse, w_fine, b_fine, w_gate, w_up, w_down):
    weights = dict(w_ada=w_ada, g_mix=g_mix, g_ffn=g_ffn, w_in=w_in, q_gain=q_gain, k_gain=k_gain, lam_re=lam_re,
                   lam_im=lam_im, log_dt=log_dt, ssm_b_re=ssm_b_re, ssm_b_im=ssm_b_im, ssm_c_re=ssm_c_re,
                   ssm_c_im=ssm_c_im, ssm_d=ssm_d, w_glu=w_glu, b_glu=b_glu, w_out=w_out, w_pool=w_pool,
                   pool_scale=pool_scale, w_coarse=w_coarse, b_coarse=b_coarse, w_fine=w_fine, b_fine=b_fine,
                   w_gate=w_gate, w_up=w_up, w_down=w_down)
    prep = _prep_weights(weights)
    n_prompt = c_prompt.shape[0]
    mod_all = ada_params_all(jnp.concatenate([c_prompt, c_sample], axis=0), w_ada, b_ada)
    out_p = _run_prompt(x_prompt, mod_all[:, :n_prompt], prep)
    out_s = _run_sample(x_sample, mod_all[:, n_prompt:], prep, cache_k, cache_v, cache_kidx, state_ssm_re,
                        state_ssm_im, state_pool, page_table)
    return (out_p[0], out_s[0]) + out_p[1:] + out_s[1:]
```

```python
import functools
import math

import jax
import jax.numpy as jnp
from jax import lax
from jax.experimental import pallas as pl
from jax.experimental.pallas import tpu as pltpu

F32 = jnp.float32
BF16 = jnp.bfloat16
I32 = jnp.int32

D_MODEL = 1024
EPS = 1e-6
A_WIDTH = 512
SSM_GROUP = 16
SSM_GROUPS = 32
SSM_STATE = 64
N_STATES = SSM_GROUPS * SSM_STATE
HEAD_DIM = 64
N_HEADS = 8
N_KV_HEADS = 2
KV_WIDTH = N_KV_HEADS * HEAD_DIM
IDX_HEADS = 8
IDX_DIM = 64
TOPK_MAX = 256
ROPE_THETA = 500000.0
ROT_HALF = 8
PAGE_SIZE = 128
POOL_WINDOWS = (2, 4, 8, 16)
POOL_GROUP_DIM = 256
POOL_STATE = 15
HIST_ROWS = 16
MOE_GROUPS = 4
EXPERTS_PER_GROUP = 4
N_EXPERTS = 16
D_FF_EXPERT = 256

OFF_Q, OFF_K, OFF_V, OFF_QI, OFF_KI, OFF_WI, IN_COLS = 512, 1024, 1152, 1280, 1792, 1856, 1864
IN_COLS_PADDED = 1920

V7X_LANES = 128
V7X_SUBLANES = 8
V7X_VMEM_BYTES = 64 * 2**20
VMEM_LIMIT_BYTES = (V7X_VMEM_BYTES * 3) // 4
INT_MIN = -(2**31)
NEG_INF = float("-inf")


def _params(*semantics):
    return pltpu.CompilerParams(dimension_semantics=semantics, vmem_limit_bytes=VMEM_LIMIT_BYTES)


def _norm_mod(x, gain, scale, shift):
    var = jnp.mean(x * x, axis=-1, keepdims=True)
    return (x * lax.rsqrt(var + EPS)) * gain * (1.0 + scale) + shift


def _silu(x):
    return x * jax.nn.sigmoid(x)


def _dot(a, b):
    return jnp.dot(a, b, preferred_element_type=F32)


def _dot_t(a, b):
    return lax.dot_general(a, b, (((1,), (1,)), ((), ())), preferred_element_type=F32)


class TokenGroup:
    def __init__(self, n_rows, rows_per_mod, tile):
        assert n_rows % rows_per_mod == 0 and (rows_per_mod % tile == 0 or tile % rows_per_mod == 0)
        self.n_rows, self.tile = n_rows, tile
        self.n_tiles = n_rows // tile
        if rows_per_mod >= tile:
            self.mod_rows = 1
            self.tiles_per_mod = rows_per_mod // tile
        else:
            self.mod_rows = tile
            self.tiles_per_mod = 1

    def mods(self, mod):
        if self.mod_rows == 1:
            return mod[:, None, :]
        rep = self.n_rows // mod.shape[0]
        return jnp.repeat(mod, rep, axis=0).reshape(self.n_tiles, self.tile, mod.shape[1])

    def mod_spec(self, piece):
        tpm = self.tiles_per_mod
        return pl.BlockSpec((None, self.mod_rows, D_MODEL), lambda i: (i // tpm, 0, piece))

    def row_spec(self, width):
        return pl.BlockSpec((self.tile, width), lambda i: (i, 0))


def _const_spec(shape):
    nd = len(shape)
    return pl.BlockSpec(shape, lambda *_: (0,) * nd)


def _ada_kernel(c_ref, w_ref, b_ref, o_ref):
    s = _silu(c_ref[...]).astype(BF16)
    o_ref[...] = _dot(s, w_ref[...].astype(BF16)) + b_ref[...]


def ada_params_all(c_all, w_ada, b_ada):
    n_layers, d, n6 = w_ada.shape
    m = c_all.shape[0]
    tn = 1536
    return pl.pallas_call(
        _ada_kernel,
        grid=(n_layers, n6 // tn),
        in_specs=[pl.BlockSpec((m, d), lambda l, j: (0, 0)),
                  pl.BlockSpec((None, d, tn), lambda l, j: (l, 0, j)),
                  pl.BlockSpec((None, 1, tn), lambda l, j: (l, 0, j))],
        out_specs=pl.BlockSpec((None, m, tn), lambda l, j: (l, 0, j)),
        out_shape=jax.ShapeDtypeStruct((n_layers, m, n6), F32),
        compiler_params=_params("arbitrary", "arbitrary"),
    )(c_all, w_ada, b_ada.reshape(n_layers, 1, n6))


def _rope(x, cos, s1, s2):
    w = x.shape[1]
    reps = w // V7X_LANES
    if reps > 1:
        cos, s1, s2 = (jnp.tile(t, (1, reps)) for t in (cos, s1, s2))
    return x * cos + pltpu.roll(x, w - ROT_HALF, 1) * s1 + pltpu.roll(x, ROT_HALF, 1) * s2


def _head_norm(x, mavg, gain):
    sq = x * x
    hi = sq.astype(BF16)
    lo = (sq - hi.astype(F32)).astype(BF16)
    var = _dot(hi, mavg) + _dot(lo, mavg)
    return (x * lax.rsqrt(var + EPS)) * gain


def _in_proj_kernel(x_ref, sc_ref, sh_ref, g_ref, w_ref, qg_ref, kg_ref, mavg_ref, rope_ref,
                    u_ref, q_ref, k_ref, v_ref, qi_ref, kiwi_ref, kb_ref, kib_ref, vt_ref):
    h = _norm_mod(x_ref[...], g_ref[...], sc_ref[...], sh_ref[...])
    p = _dot(h.astype(BF16), w_ref[...])
    cos, s1, s2 = rope_ref[0], rope_ref[1], rope_ref[2]
    mavg = mavg_ref[...]
    u_ref[...] = p[:, 0:OFF_Q]
    q = _rope(_head_norm(p[:, OFF_Q:OFF_K], mavg, qg_ref[...]), cos, s1, s2)
    q_ref[...] = (q * HEAD_DIM ** -0.5).astype(BF16)
    k = _head_norm(p[:, OFF_K:OFF_V], mavg[0:KV_WIDTH, 0:KV_WIDTH], kg_ref[...])
    k = _rope(k, cos, s1, s2)
    k_ref[...] = k
    kb_ref[...] = k.astype(BF16)
    v = p[:, OFF_V:OFF_QI]
    v_ref[...] = v
    vt_ref[...] = v.T.astype(BF16)
    qi = _rope(p[:, OFF_QI:OFF_KI], cos, s1, s2)
    qi_ref[...] = (qi * IDX_DIM ** -0.5).astype(BF16)
    kiwi = _rope(p[:, OFF_KI:IN_COLS_PADDED], rope_ref[3], rope_ref[4], rope_ref[5])
    kiwi_ref[...] = kiwi
    kib_ref[...] = kiwi.astype(BF16)


def _rope_tables(pos):
    t = pos.shape[0]
    inv = ROPE_THETA ** (-jnp.arange(ROT_HALF, dtype=F32) / ROT_HALF)
    ang = pos.astype(F32)[:, None] * inv[None, :]
    cos, sin = jnp.cos(ang), jnp.sin(ang)
    rest = HEAD_DIM - 2 * ROT_HALF
    z8, zr, onesr = jnp.zeros((t, ROT_HALF), F32), jnp.zeros((t, rest), F32), jnp.ones((t, rest), F32)
    c64 = jnp.concatenate([cos, cos, onesr], axis=1)
    s1_64 = jnp.concatenate([-sin, z8, zr], axis=1)
    s2_64 = jnp.concatenate([z8, sin, zr], axis=1)
    z64 = jnp.zeros((t, HEAD_DIM), F32)
    wscale = jnp.concatenate([jnp.full((t, IDX_HEADS), IDX_HEADS ** -0.5, F32),
                              jnp.ones((t, HEAD_DIM - IDX_HEADS), F32)], axis=1)
    return jnp.stack([jnp.concatenate([c64, c64], axis=1), jnp.concatenate([s1_64, s1_64], axis=1),
                      jnp.concatenate([s2_64, s2_64], axis=1), jnp.concatenate([c64, wscale], axis=1),
                      jnp.concatenate([s1_64, z64], axis=1), jnp.concatenate([s2_64, z64], axis=1)])


def in_proj(x2, mods, grp, g_mix, w_in_p, q_gain, k_gain, mavg, rope, seq_rows):
    n = x2.shape[0]
    tm = grp.tile
    rope_tiles = rope.shape[1] // tm
    tiles_per_seq = seq_rows // tm
    outs = pl.pallas_call(
        _in_proj_kernel,
        grid=(grp.n_tiles,),
        in_specs=[grp.row_spec(D_MODEL), grp.mod_spec(1), grp.mod_spec(0), _const_spec((1, D_MODEL)),
                  _const_spec((D_MODEL, IN_COLS_PADDED)), _const_spec((1, 512)), _const_spec((1, KV_WIDTH)),
                  _const_spec((512, 512)),
                  pl.BlockSpec((6, tm, V7X_LANES), lambda i: (0, i % rope_tiles, 0))],
        out_specs=[grp.row_spec(512), grp.row_spec(512), grp.row_spec(KV_WIDTH), grp.row_spec(KV_WIDTH),
                   grp.row_spec(512), grp.row_spec(V7X_LANES), grp.row_spec(KV_WIDTH), grp.row_spec(V7X_LANES),
                   pl.BlockSpec((None, KV_WIDTH, tm), lambda i: (i // tiles_per_seq, 0, i % tiles_per_seq))],
        out_shape=[jax.ShapeDtypeStruct((n, 512), F32), jax.ShapeDtypeStruct((n, 512), BF16),
                   jax.ShapeDtypeStruct((n, KV_WIDTH), F32), jax.ShapeDtypeStruct((n, KV_WIDTH), F32),
                   jax.ShapeDtypeStruct((n, 512), BF16), jax.ShapeDtypeStruct((n, V7X_LANES), F32),
                   jax.ShapeDtypeStruct((n, KV_WIDTH), BF16), jax.ShapeDtypeStruct((n, V7X_LANES), BF16),
                   jax.ShapeDtypeStruct((n // seq_rows, KV_WIDTH, seq_rows), BF16)],
        compiler_params=_params("arbitrary"),
    )(x2, mods, mods, g_mix, w_in_p, q_gain, k_gain, mavg, rope)
    return outs


S5_BATCH_BLOCK = V7X_SUBLANES
S5_HALF_IN = A_WIDTH // 2
S5_HALF_STATES = N_STATES // 2
S5_SCAN_COLS = 1024


def _gelu_tanh(x):
    cdf = 0.5 * (1.0 + jnp.tanh(math.sqrt(2.0 / math.pi) * (x + 0.044715 * (x * x * x))))
    return x * cdf


def _s5_kernel(u_ref, x0re_ref, x0im_ref, are_ref, aim_ref, bre_ref, bim_ref, cre_ref, cim_ref, d_ref,
               wglu_ref, bglu_ref, y_ref, sre_ref, sim_ref, st_re, st_im, xs_re, xs_im, *, tt):
    bb = S5_BATCH_BLOCK
    tb = pl.program_id(1)

    @pl.when(tb == 0)
    def _():
        st_re[...] = x0re_ref[...]
        st_im[...] = x0im_ref[...]

    u = u_ref[...].reshape(tt * bb, A_WIDTH)
    ub = u.astype(BF16)
    for half in range(2):
        rows = slice(half * S5_HALF_IN, (half + 1) * S5_HALF_IN)
        cols = slice(half * S5_HALF_STATES, (half + 1) * S5_HALF_STATES)
        xs_re[:, cols] = _dot(ub[:, rows], bre_ref[rows, cols])
        xs_im[:, cols] = _dot(ub[:, rows], bim_ref[rows, cols])

    for c in range(N_STATES // S5_SCAN_COLS):
        cs = slice(c * S5_SCAN_COLS, (c + 1) * S5_SCAN_COLS)
        a_re = are_ref[:, cs]
        a_im = aim_ref[:, cs]

        def step(t, carry, cs=cs, a_re=a_re, a_im=a_im):
            sr, si = carry
            r0 = pl.multiple_of(t * bb, bb)
            nr = a_re * sr - a_im * si + xs_re[pl.ds(r0, bb), cs]
            ni = a_re * si + a_im * sr + xs_im[pl.ds(r0, bb), cs]
            xs_re[pl.ds(r0, bb), cs] = nr
            xs_im[pl.ds(r0, bb), cs] = ni
            return nr, ni

        sr, si = lax.fori_loop(0, tt, step, (st_re[:, cs], st_im[:, cs]))
        st_re[:, cs] = sr
        st_im[:, cs] = si

    xr = xs_re[...].astype(BF16)
    xi = xs_im[...].astype(BF16)
    ys = []
    for half in range(2):
        rows = slice(half * S5_HALF_STATES, (half + 1) * S5_HALF_STATES)
        cols = slice(half * S5_HALF_IN, (half + 1) * S5_HALF_IN)
        ys.append(_dot(xr[:, rows], cre_ref[rows, cols]) - _dot(xi[:, rows], cim_ref[rows, cols]))
    y = jnp.concatenate(ys, axis=1) + d_ref[...] * u
    z = _gelu_tanh(y)
    o = z * jax.nn.sigmoid(_dot(z.astype(BF16), wglu_ref[...]) + bglu_ref[...])
    y_ref[...] = o.reshape(tt, bb, A_WIDTH)

    @pl.when(tb == pl.num_programs(1) - 1)
    def _():
        sre_ref[...] = st_re[...]
        sim_ref[...] = st_im[...]


def s5_discretise(lam_re, lam_im, log_dt, b_re, b_im, c_re, c_im):
    dt = jnp.exp(log_dt)[:, None]
    mag = jnp.exp(lam_re * dt)
    abar_re, abar_im = mag * jnp.cos(lam_im * dt), mag * jnp.sin(lam_im * dt)
    den = lam_re * lam_re + lam_im * lam_im
    ir, ii = lam_re / den, -lam_im / den
    cr = (abar_re - 1.0) * ir - abar_im * ii
    ci = (abar_re - 1.0) * ii + abar_im * ir
    bb_re = cr[..., None] * b_re - ci[..., None] * b_im
    bb_im = cr[..., None] * b_im + ci[..., None] * b_re
    eye = jnp.eye(SSM_GROUPS, dtype=F32)
    pack_b = lambda bb: jnp.einsum("gpn,gh->gnhp", bb, eye).reshape(A_WIDTH, N_STATES).astype(BF16)
    pack_c = lambda cc: jnp.einsum("gnp,gh->gphn", cc, eye).reshape(N_STATES, A_WIDTH).astype(BF16)
    bcast = lambda a: jnp.broadcast_to(a.reshape(1, N_STATES), (S5_BATCH_BLOCK, N_STATES))
    return bcast(abar_re), bcast(abar_im), pack_b(bb_re), pack_b(bb_im), pack_c(c_re), pack_c(c_im)


def s5_mixer(u3, x0_re, x0_im, disc, d_skip, w_glu, b_glu, tt):
    t, b, _ = u3.shape
    bb = S5_BATCH_BLOCK
    a_re, a_im, bre, bim, cre, cim = disc
    state_spec = pl.BlockSpec((bb, N_STATES), lambda i, j: (i, 0))
    seq_spec = pl.BlockSpec((tt, bb, A_WIDTH), lambda i, j: (j, i, 0))
    return pl.pallas_call(
        functools.partial(_s5_kernel, tt=tt),
        grid=(b // bb, t // tt),
        in_specs=[seq_spec, state_spec, state_spec, _const_spec((bb, N_STATES)), _const_spec((bb, N_STATES)),
                  _const_spec((A_WIDTH, N_STATES)), _const_spec((A_WIDTH, N_STATES)),
                  _const_spec((N_STATES, A_WIDTH)), _const_spec((N_STATES, A_WIDTH)),
                  _const_spec((1, A_WIDTH)), _const_spec((A_WIDTH, A_WIDTH)), _const_spec((1, A_WIDTH))],
        out_specs=[seq_spec, state_spec, state_spec],
        out_shape=[jax.ShapeDtypeStruct((t, b, A_WIDTH), F32), jax.ShapeDtypeStruct((b, N_STATES), F32),
                   jax.ShapeDtypeStruct((b, N_STATES), F32)],
        scratch_shapes=[pltpu.VMEM((bb, N_STATES), F32), pltpu.VMEM((bb, N_STATES), F32),
                        pltpu.VMEM((tt * bb, N_STATES), F32), pltpu.VMEM((tt * bb, N_STATES), F32)],
        compiler_params=_params("arbitrary", "arbitrary"),
    )(u3, x0_re, x0_im, a_re, a_im, bre, bim, cre, cim, d_skip, w_glu, b_glu)


def _count(mask):
    return jnp.sum(jnp.where(mask, 1.0, 0.0), axis=1, keepdims=True)


def _topk_bias(sc, keep, qpos0, key_ref, bias_ref):
    m, s_keys = sc.shape
    keep = float(keep)
    kpos = lax.broadcasted_iota(I32, (m, s_keys), 1)
    qpos = qpos0 + lax.broadcasted_iota(I32, (m, 1), 0)
    causal = kpos <= qpos
    bits = lax.bitcast_convert_type(sc, I32)
    mag = bits & 0x7FFFFFFF
    key_ref[...] = jnp.where(causal, jnp.where(bits < 0, -mag, mag), INT_MIN)

    def bit_step(it, thr):
        cand = thr + lax.shift_left(jnp.int32(1), 31 - it)
        return jnp.where(_count(key_ref[...] >= cand) >= keep, cand, thr)

    thr = lax.fori_loop(0, 32, bit_step, jnp.full((m, 1), INT_MIN, I32))
    key = key_ref[...]
    gt = key > thr
    eq = key == thr
    need = keep - _count(gt)
    bias_ref[...] = jnp.where((gt | eq) & causal, 0.0, NEG_INF)
    tie = (_count(eq) > need) & (thr > INT_MIN)

    @pl.when(jnp.max(jnp.where(tie, 1.0, 0.0)) > 0.0)
    def _():
        def idx_step(_, lohi):
            lo, hi = lohi
            mid = (lo + hi) >> 1
            ok = _count((key_ref[...] == thr) & (kpos <= mid)) >= need
            return jnp.where(ok, lo, mid + 1), jnp.where(ok, mid, hi)

        n_steps = max(1, (s_keys - 1).bit_length())
        last, _ = lax.fori_loop(0, n_steps, idx_step,
                                (jnp.zeros((m, 1), I32), jnp.full((m, 1), s_keys - 1, I32)))
        k2 = key_ref[...]
        sel = (k2 > thr) | ((k2 == thr) & (kpos <= last))
        bias_ref[...] = jnp.where(sel & causal, 0.0, NEG_INF)


def _select_attend_t(s_keys, keep, qpos0, q_hm, qi_hm, wi_t, ki_ref, k_ref, vt_ref, key_ref, bias_ref, o_ref):
    tq = o_ref.shape[0]
    ck = ATTN_KEY_CHUNK if s_keys % ATTN_KEY_CHUNK == 0 else tq
    assert s_keys % ck == 0
    chunks = [slice(c * ck, (c + 1) * ck) for c in range(s_keys // ck)]
    group = N_HEADS // N_KV_HEADS
    keep = float(keep)
    qpos = qpos0 + lax.broadcasted_iota(I32, (1, tq), 1)
    lanes = lambda a, j: a[:, j * tq:(j + 1) * tq]
    fold8 = lambda a, op: op(a.reshape(a.shape[0] // V7X_SUBLANES, V7X_SUBLANES, a.shape[1]), axis=0)

    for c, rows in enumerate(chunks):
        ki_c = ki_ref[rows, 0:IDX_DIM]
        sc = jnp.zeros((ck, tq), F32)
        for h in range(IDX_HEADS):
            sc = sc + jnp.maximum(_dot_t(ki_c, qi_hm[h * tq:(h + 1) * tq, :]), 0.0) * wi_t[h:h + 1, :]
        kpos = c * ck + lax.broadcasted_iota(I32, (ck, tq), 0)
        bits = lax.bitcast_convert_type(sc, I32)
        mag = bits & 0x7FFFFFFF
        key_ref[rows, :] = jnp.where(kpos <= qpos, jnp.where(bits < 0, -mag, mag), INT_MIN)

    ones = jnp.ones((V7X_SUBLANES, s_keys), BF16)

    def counts(conds):
        mask = jnp.concatenate([jnp.where(m, 1.0, 0.0).astype(BF16) for m in conds], axis=1)
        n = _dot(ones, mask)[0:1, :]
        return [lanes(n, j) for j in range(len(conds))]

    def two_bits(it, carry):
        thr, cnt = carry
        unit = lax.shift_left(jnp.int32(1), 30 - 2 * it)
        cands = [thr + unit, thr + 2 * unit, thr + 3 * unit]
        key = key_ref[...]
        for cand, n in zip(cands, counts([key >= cand for cand in cands])):
            ok = n >= keep
            thr, cnt = jnp.where(ok, cand, thr), jnp.where(ok, n, cnt)
        return thr, cnt

    thr, cnt = lax.fori_loop(0, 16, two_bits,
                             (jnp.full((1, tq), INT_MIN, I32), jnp.full((1, tq), s_keys, F32)))
    bias_ref[...] = jnp.where(key_ref[...] >= jnp.maximum(thr, INT_MIN + 1), 0.0, NEG_INF)
    tie = (cnt > keep) & (thr > INT_MIN)

    @pl.when(jnp.max(jnp.where(tie, 1.0, 0.0)) > 0.0)
    def _():
        kpos = lax.broadcasted_iota(I32, (s_keys, tq), 0)
        need = keep - counts([key_ref[...] > thr])[0]

        def idx_step(_, lohi):
            lo, hi = lohi
            mid = (lo + hi) >> 1
            ok = counts([(key_ref[...] == thr) & (kpos <= mid)])[0] >= need
            return jnp.where(ok, lo, mid + 1), jnp.where(ok, mid, hi)

        n_steps = max(1, (s_keys - 1).bit_length())
        last, _ = lax.fori_loop(0, n_steps, idx_step,
                                (jnp.zeros((1, tq), I32), jnp.full((1, tq), s_keys - 1, I32)))
        k2 = key_ref[...]
        sel = (k2 > thr) | ((k2 == thr) & (kpos <= last))
        bias_ref[...] = jnp.where(sel & (k2 > INT_MIN), 0.0, NEG_INF)

    outs = []
    for g in range(N_KV_HEADS):
        q4 = q_hm[g * group * tq:(g + 1) * group * tq, :]
        feats = slice(g * HEAD_DIM, (g + 1) * HEAD_DIM)
        score = lambda rows: _dot_t(k_ref[rows, feats], q4) + jnp.tile(bias_ref[rows, :], (1, group))
        m8 = None
        for rows in chunks:
            mc = fold8(score(rows), jnp.max)
            m8 = mc if m8 is None else jnp.maximum(m8, mc)
        m = jnp.max(m8, axis=0, keepdims=True)
        l8 = jnp.zeros((V7X_SUBLANES, group * tq), F32)
        ot = jnp.zeros((HEAD_DIM, group * tq), F32)
        for rows in chunks:
            p = jnp.exp(score(rows) - m)
            l8 = l8 + fold8(p, jnp.sum)
            ot = ot + _dot(vt_ref[feats, rows], p.astype(BF16))
        ot = ot / jnp.sum(l8, axis=0, keepdims=True)
        outs.extend(lanes(ot, h) for h in range(group))
    o_ref[...] = jnp.concatenate(outs, axis=0).T.astype(o_ref.dtype)


def _attn_prompt_kernel(q_ref, qi_ref, wit_ref, k_ref, vt_ref, ki_ref, o_ref, key_ref, bias_ref, *, tq, buckets,
                        keep):
    i = pl.program_id(1)
    need_keys = (i + 1) * tq
    lo = 0
    for s_keys in buckets:
        @pl.when((need_keys > lo) & (need_keys <= s_keys))
        def _(s_keys=s_keys):
            _select_attend_t(s_keys, keep, i * tq, q_ref[...], qi_ref[...], wit_ref[...], ki_ref, k_ref, vt_ref,
                             key_ref.at[0:s_keys, :], bias_ref.at[0:s_keys, :], o_ref)
        lo = s_keys


def attend_prompt(q, qi, kiwi, kb, kib, vt, batch, seq, tq, n_buckets=4):
    n = q.shape[0]
    nq = seq // tq
    buckets = tuple(seq * (j + 1) // n_buckets for j in range(n_buckets))
    head_major = lambda a: jnp.swapaxes(a.reshape(n // tq, tq, N_HEADS, HEAD_DIM), 1, 2).reshape(
        n // tq, N_HEADS * tq, HEAD_DIM)
    wi_t = jnp.swapaxes(kiwi[:, IDX_DIM:IDX_DIM + IDX_HEADS].reshape(n // tq, tq, IDX_HEADS), 1, 2)
    hm_spec = pl.BlockSpec((None, N_HEADS * tq, HEAD_DIM), lambda b, i: (b * nq + i, 0, 0))
    kv_spec = pl.BlockSpec((seq, KV_WIDTH), lambda b, i: (b, 0))
    return pl.pallas_call(
        functools.partial(_attn_prompt_kernel, tq=tq, buckets=buckets, keep=min(TOPK_MAX, seq // 4)),
        grid=(batch, nq),
        in_specs=[hm_spec, hm_spec, pl.BlockSpec((None, IDX_HEADS, tq), lambda b, i: (b * nq + i, 0, 0)),
                  kv_spec, pl.BlockSpec((None, KV_WIDTH, seq), lambda b, i: (b, 0, 0)), kv_spec],
        out_specs=pl.BlockSpec((tq, 512), lambda b, i: (b * nq + i, 0)),
        out_shape=jax.ShapeDtypeStruct((n, 512), BF16),
        scratch_shapes=[pltpu.VMEM((seq, tq), I32), pltpu.VMEM((seq, tq), F32)],
        compiler_params=_params("arbitrary", "arbitrary"),
    )(head_major(q), head_major(qi), wi_t, kb, vt, kib)


ATTN_TQ = 128
ATTN_KEY_CHUNK = 256
SAMPLE_Q_ROWS = V7X_SUBLANES


def _attn_sample_kernel(pt_ref, q_ref, qi_ref, wi_ref, knew_ref, vnew_ref, kinew_ref, ck_hbm, cv_hbm, cki_hbm,
                        o_ref, k_all, v_all, ki_all, sems, key_ref, bias_ref, *, layer, n_pages, keep):
    b = pl.program_id(0)
    slot = b % 2
    past = n_pages * PAGE_SIZE

    def page_copies(seq, sl):
        copies = []
        for p in range(n_pages):
            phys = pt_ref[seq, p]
            window = pl.ds(p * PAGE_SIZE, PAGE_SIZE)
            copies.append(pltpu.make_async_copy(ck_hbm.at[layer, phys], k_all.at[sl, :, window], sems.at[sl, 0]))
            copies.append(pltpu.make_async_copy(cv_hbm.at[layer, phys], v_all.at[sl, :, window], sems.at[sl, 1]))
            copies.append(pltpu.make_async_copy(cki_hbm.at[layer, phys], ki_all.at[sl, :, window], sems.at[sl, 2]))
        return copies

    @pl.when(b == 0)
    def _():
        for c in page_copies(0, 0):
            c.start()

    @pl.when(b + 1 < pl.num_programs(0))
    def _():
        for c in page_copies(b + 1, 1 - slot):
            c.start()

    k_all[slot, :, past:past + PAGE_SIZE] = knew_ref[...]
    v_all[slot, :, past:past + PAGE_SIZE] = vnew_ref[...]
    ki_all[slot, :, past:past + PAGE_SIZE] = kinew_ref[...]
    for c in page_copies(b, slot):
        c.wait()
    rows = SAMPLE_Q_ROWS
    d = _dot(qi_ref[...], ki_all[slot].astype(BF16))
    wi = wi_ref[:, IDX_DIM:IDX_DIM + IDX_HEADS]
    sc = jnp.zeros((rows, past + PAGE_SIZE), F32)
    for h in range(IDX_HEADS):
        sc = sc + jnp.maximum(d[h * rows:(h + 1) * rows, :], 0.0) * wi[:, h:h + 1]
    _topk_bias(sc, keep, past, key_ref, bias_ref)
    group = N_HEADS // N_KV_HEADS
    bias = jnp.tile(bias_ref[...], (group, 1))
    q = q_ref[...]
    for g in range(N_KV_HEADS):
        feats = slice(g * HEAD_DIM, (g + 1) * HEAD_DIM)
        qrows = slice(g * group * rows, (g + 1) * group * rows)
        s = _dot(q[qrows, :], k_all[slot, feats, :].astype(BF16)) + bias
        p = jnp.exp(s - jnp.max(s, axis=1, keepdims=True))
        l = jnp.sum(p, axis=1, keepdims=True)
        o_ref[qrows, :] = _dot_t(p.astype(BF16), v_all[slot, feats, :].astype(BF16)) / l


def attend_sample(q_st, qi_st, kiwi3, k_new_t, v_new_t, ki_new_t, cache_k_t, cache_v_t, cache_ki_t, page_table,
                  layer, seq):
    b = q_st.shape[0]
    n_pages = page_table.shape[1]
    s_keys = (n_pages + 1) * PAGE_SIZE
    rows = SAMPLE_Q_ROWS
    st_spec = pl.BlockSpec((None, N_HEADS * rows, HEAD_DIM), lambda i, pt: (i, 0, 0))
    q_spec = lambda w: pl.BlockSpec((None, rows, w), lambda i, pt: (i, 0, 0))
    new_spec = lambda f: pl.BlockSpec((None, f, PAGE_SIZE), lambda i, pt: (i, 0, 0))
    hbm = pl.BlockSpec(memory_space=pl.ANY)
    grid_spec = pltpu.PrefetchScalarGridSpec(
        num_scalar_prefetch=1,
        grid=(b,),
        in_specs=[st_spec, st_spec, q_spec(V7X_LANES), new_spec(KV_WIDTH), new_spec(KV_WIDTH),
                  new_spec(IDX_DIM), hbm, hbm, hbm],
        out_specs=st_spec,
        scratch_shapes=[pltpu.VMEM((2, KV_WIDTH, s_keys), F32), pltpu.VMEM((2, KV_WIDTH, s_keys), F32),
                        pltpu.VMEM((2, IDX_DIM, s_keys), F32), pltpu.SemaphoreType.DMA((2, 3)),
                        pltpu.VMEM((rows, s_keys), I32), pltpu.VMEM((rows, s_keys), F32)],
    )
    keep = min(TOPK_MAX, (n_pages * PAGE_SIZE + seq) // 4)
    return pl.pallas_call(
        functools.partial(_attn_sample_kernel, layer=layer, n_pages=n_pages, keep=keep),
        grid_spec=grid_spec,
        out_shape=jax.ShapeDtypeStruct((b, N_HEADS * rows, HEAD_DIM), F32),
        compiler_params=_params("arbitrary"),
    )(page_table, q_st, qi_st, kiwi3, k_new_t, v_new_t, ki_new_t, cache_k_t, cache_v_t, cache_ki_t)


def _out_proj_kernel(x_ref, ya_ref, yb_ref, gate_ref, w_ref, o_ref):
    y = _dot(ya_ref[...].astype(BF16), w_ref[0:A_WIDTH, :]) + _dot(yb_ref[...], w_ref[A_WIDTH:2 * A_WIDTH, :])
    o_ref[...] = x_ref[...] + gate_ref[...] * y


def out_proj(x2, y_a, y_b, mods, grp, w_out):
    return pl.pallas_call(
        _out_proj_kernel,
        grid=(grp.n_tiles,),
        in_specs=[grp.row_spec(D_MODEL), grp.row_spec(A_WIDTH), grp.row_spec(512), grp.mod_spec(2),
                  _const_spec((2 * A_WIDTH, D_MODEL))],
        out_specs=grp.row_spec(D_MODEL),
        out_shape=jax.ShapeDtypeStruct(x2.shape, F32),
        compiler_params=_params("arbitrary"),
    )(x2, y_a, y_b, mods, w_out)


def _pool_kernel(x_ref, hist_ref, sc_ref, sh_ref, gate_ref, g_ref, w_ref, ps_ref, o_ref, st_ref, *,
                 tiles_per_seq, hist_is_x, pos_base, state_rows):
    i = pl.program_id(0)
    tm = x_ref.shape[0]
    x = x_ref[...]
    h = _norm_mod(x, g_ref[...], sc_ref[...], sh_ref[...])
    if hist_is_x:
        hist = _norm_mod(hist_ref[...], g_ref[...], sc_ref[...], sh_ref[...])
        hist = jnp.where(i % tiles_per_seq == 0, 0.0, hist)
    else:
        hist = hist_ref[...]
    ext = jnp.concatenate([hist, h], axis=0)
    pos = pos_base + (i % tiles_per_seq) * tm + lax.broadcasted_iota(I32, (tm, 1), 0)
    ys = []
    for g, win in enumerate(POOL_WINDOWS):
        cols = slice(g * POOL_GROUP_DIM, (g + 1) * POOL_GROUP_DIM)
        s = ext[:, cols]
        shift = 1
        while shift < win:
            s = s + pltpu.roll(s, shift, 0)
            shift *= 2
        cnt = jnp.minimum(pos + 1, win).astype(F32)
        pooled = s[HIST_ROWS:, :] / cnt - h[:, cols]
        ys.append(_dot(pooled.astype(BF16), w_ref[g]))
    y = jnp.concatenate(ys, axis=1) * ps_ref[...]
    o_ref[...] = x + gate_ref[...] * y
    st_ref[...] = ext[HIST_ROWS + tm - state_rows:, :]


def pool_mixer(x2, hist, mods, grp, g_mix, w_pool, pool_scale, *, tiles_per_seq, hist_is_x, pos_base, state_rows):
    tm = grp.tile
    n_seq = grp.n_tiles // tiles_per_seq
    if hist_is_x:
        per_tile = tm // HIST_ROWS
        hist_spec = pl.BlockSpec((HIST_ROWS, D_MODEL), lambda i: (jnp.maximum(i * per_tile - 1, 0), 0))
    else:
        hist_spec = pl.BlockSpec((None, HIST_ROWS, D_MODEL), lambda i: (i, 0, 0))
    return pl.pallas_call(
        functools.partial(_pool_kernel, tiles_per_seq=tiles_per_seq, hist_is_x=hist_is_x, pos_base=pos_base,
                          state_rows=state_rows),
        grid=(grp.n_tiles,),
        in_specs=[grp.row_spec(D_MODEL), hist_spec, grp.mod_spec(1), grp.mod_spec(0), grp.mod_spec(2),
                  _const_spec((1, D_MODEL)), _const_spec((len(POOL_WINDOWS), POOL_GROUP_DIM, POOL_GROUP_DIM)),
                  _const_spec((1, D_MODEL))],
        out_specs=[grp.row_spec(D_MODEL),
                   pl.BlockSpec((None, state_rows, D_MODEL), lambda i: (i // tiles_per_seq, 0, 0))],
        out_shape=[jax.ShapeDtypeStruct(x2.shape, F32), jax.ShapeDtypeStruct((n_seq, state_rows, D_MODEL), F32)],
        compiler_params=_params("arbitrary"),
    )(x2, hist, mods, mods, mods, g_mix, w_pool, pool_scale)


ROUTER_COLS = V7X_LANES


MOE_GROUP_ROW = EXPERTS_PER_GROUP
ROUTE_ROWS = V7X_SUBLANES
MOE_CHUNK = 256
MOE_SEG_ALIGN = V7X_SUBLANES


def _router_gates(logits_t):
    col = lambda j: logits_t[j:j + 1, :]
    lc = [col(j) for j in range(MOE_GROUPS)]
    mc = functools.reduce(jnp.maximum, lc)
    pg = 1.0 / functools.reduce(lambda a, b: a + b, [jnp.exp(l - mc) for l in lc])
    grp = jnp.where(lc[0] == mc, 0, jnp.where(lc[1] == mc, 1, jnp.where(lc[2] == mc, 2, 3)))
    fl = []
    for j in range(EXPERTS_PER_GROUP):
        cands = [col(MOE_GROUPS + g * EXPERTS_PER_GROUP + j) for g in range(MOE_GROUPS)]
        fl.append(jnp.where(grp == 0, cands[0], jnp.where(grp == 1, cands[1],
                                                          jnp.where(grp == 2, cands[2], cands[3]))))
    first = lambda vals, mx: jnp.where(vals[0] == mx, 0, jnp.where(vals[1] == mx, 1, jnp.where(vals[2] == mx, 2, 3)))
    m1 = functools.reduce(jnp.maximum, fl)
    i1 = first(fl, m1)
    rest = [jnp.where(i1 == j, NEG_INF, fl[j]) for j in range(EXPERTS_PER_GROUP)]
    m2 = functools.reduce(jnp.maximum, rest)
    i2 = first(rest, m2)
    e2 = jnp.exp(m2 - m1)
    w1 = pg / (1.0 + e2)
    w2 = pg * e2 / (1.0 + e2)
    return grp, [jnp.where(i1 == j, w1, jnp.where(i2 == j, w2, 0.0)) for j in range(EXPERTS_PER_GROUP)]


def _route_kernel(x_ref, sc_ref, sh_ref, g_ref, wr_ref, br_ref, o_ref):
    h = _norm_mod(x_ref[...], g_ref[...], sc_ref[...], sh_ref[...])
    logits = jnp.dot(h, wr_ref[...], preferred_element_type=F32, precision=lax.Precision.HIGHEST) + br_ref[...]
    grp, gates = _router_gates(logits.T)
    sub = lax.broadcasted_iota(I32, o_ref.shape, 0)
    out = jnp.where(sub == MOE_GROUP_ROW, grp.astype(F32), 0.0)
    for j, gj in enumerate(gates):
        out = jnp.where(sub == j, gj, out)
    o_ref[...] = out


def moe_route(x2, mods, grp, g_ffn, w_router, b_router):
    return pl.pallas_call(
        _route_kernel,
        grid=(grp.n_tiles,),
        in_specs=[grp.row_spec(D_MODEL), grp.mod_spec(4), grp.mod_spec(3), _const_spec((1, D_MODEL)),
                  _const_spec((D_MODEL, ROUTER_COLS)), _const_spec((1, ROUTER_COLS))],
        out_specs=pl.BlockSpec((ROUTE_ROWS, grp.tile), lambda i: (0, i)),
        out_shape=jax.ShapeDtypeStruct((ROUTE_ROWS, x2.shape[0]), F32),
        compiler_params=_params("arbitrary"),
    )(x2, mods, mods, g_ffn, w_router, b_router)


def _route_plan(rout_t, tm):
    n_tiles = rout_t.shape[1] // tm
    grp = rout_t[MOE_GROUP_ROW].astype(I32).reshape(n_tiles, tm)
    onehot = (grp[:, :, None] == jnp.arange(MOE_GROUPS, dtype=I32)[None, None, :]).astype(I32)
    cnt = jnp.sum(onehot, axis=1)
    padded = (cnt + MOE_SEG_ALIGN - 1) // MOE_SEG_ALIGN * MOE_SEG_ALIGN
    start = jnp.cumsum(padded, axis=1) - padded
    before = jnp.cumsum(onehot, axis=1) - onehot
    dest = jnp.sum(onehot * (before + start[:, None, :]), axis=2)
    return dest.reshape(-1), start.reshape(-1), cnt.reshape(-1)


def _moe_kernel(dest_ref, start_ref, cnt_ref, x_ref, sc_ref, sh_ref, gate_ref, g_ref, rout_ref, wg_ref, wu_ref,
                wd_ref, o_ref, hs_ref, gs_ref, ys_ref, *, chunk):
    i = pl.program_id(0)
    g = pl.program_id(1)
    tm = x_ref.shape[0]
    base = i * tm

    @pl.when((i == 0) & (g == 0))
    def _():
        hs_ref[...] = jnp.zeros_like(hs_ref)
        gs_ref[...] = jnp.zeros_like(gs_ref)

    @pl.when(g == 0)
    def _():
        o_ref[...] = _norm_mod(x_ref[...], g_ref[...], sc_ref[...], sh_ref[...])

        def move(r, carry):
            d = dest_ref[base + r]
            hs_ref[pl.ds(d, 1), :] = o_ref[pl.ds(r, 1), :]
            gs_ref[pl.ds(d, 1), :] = rout_ref[pl.ds(r, 1), :]
            return carry

        lax.fori_loop(0, tm, move, 0, unroll=8)

    start = start_ref[i * MOE_GROUPS + g]
    cnt = cnt_ref[i * MOE_GROUPS + g]

    def run_chunk(c, carry):
        r0 = pl.multiple_of(start + c * chunk, MOE_SEG_ALIGN)
        xs = hs_ref[pl.ds(r0, chunk), :].astype(BF16)
        gates = gs_ref[pl.ds(r0, chunk), :]
        valid = c * chunk + lax.broadcasted_iota(I32, (chunk, 1), 0) < cnt
        y = jnp.zeros((chunk, D_MODEL), F32)
        for e in range(EXPERTS_PER_GROUP):
            hid = _silu(_dot(xs, wg_ref[e])) * _dot(xs, wu_ref[e]) * gates[:, e:e + 1]
            y = y + _dot(jnp.where(valid, hid, 0.0).astype(BF16), wd_ref[e])
        ys_ref[pl.ds(r0, chunk), :] = y
        return carry

    lax.fori_loop(0, (cnt + chunk - 1) // chunk, run_chunk, 0)

    @pl.when(g == pl.num_programs(1) - 1)
    def _():
        def move_back(r, carry):
            d = dest_ref[base + r]
            o_ref[pl.ds(r, 1), :] = ys_ref[pl.ds(d, 1), :]
            return carry

        lax.fori_loop(0, tm, move_back, 0, unroll=8)
        o_ref[...] = x_ref[...] + gate_ref[...] * o_ref[...]


def hier_moe(x2, mods, grp, g_ffn, w_router, b_router, w_gate, w_up, w_down):
    tm = grp.tile
    chunk = min(MOE_CHUNK, tm)
    rout_t = moe_route(x2, mods, grp, g_ffn, w_router, b_router)
    dest, start, cnt = _route_plan(rout_t, tm)
    rout = rout_t.T
    buf_rows = tm + chunk + MOE_GROUPS * MOE_SEG_ALIGN
    row = lambda w: pl.BlockSpec((tm, w), lambda i, g, *_: (i, 0))
    tpm = grp.tiles_per_mod
    mod_spec = lambda piece: pl.BlockSpec((None, grp.mod_rows, D_MODEL), lambda i, g, *_: (i // tpm, 0, piece))
    epg = EXPERTS_PER_GROUP
    grid_spec = pltpu.PrefetchScalarGridSpec(
        num_scalar_prefetch=3,
        grid=(grp.n_tiles, MOE_GROUPS),
        in_specs=[row(D_MODEL), mod_spec(4), mod_spec(3), mod_spec(5),
                  pl.BlockSpec((1, D_MODEL), lambda i, g, *_: (0, 0)), row(ROUTE_ROWS),
                  pl.BlockSpec((epg, D_MODEL, D_FF_EXPERT), lambda i, g, *_: (g, 0, 0)),
                  pl.BlockSpec((epg, D_MODEL, D_FF_EXPERT), lambda i, g, *_: (g, 0, 0)),
                  pl.BlockSpec((epg, D_FF_EXPERT, D_MODEL), lambda i, g, *_: (g, 0, 0))],
        out_specs=row(D_MODEL),
        scratch_shapes=[pltpu.VMEM((buf_rows, D_MODEL), F32), pltpu.VMEM((buf_rows, ROUTE_ROWS), F32),
                        pltpu.VMEM((buf_rows, D_MODEL), F32)],
    )
    return pl.pallas_call(
        functools.partial(_moe_kernel, chunk=chunk),
        grid_spec=grid_spec,
        out_shape=jax.ShapeDtypeStruct(x2.shape, F32),
        compiler_params=_params("arbitrary", "arbitrary"),
    )(dest, start, cnt, x2, mods, mods, mods, g_ffn, rout, w_gate, w_up, w_down)


def _prep_weights(w):
    depth = w["w_ada"].shape[0]
    n_ab = w["w_in"].shape[0]
    prep = {"layers": [], "ab": [], "c": []}
    blk = jnp.arange(512) // HEAD_DIM
    prep["mavg"] = jnp.where(blk[:, None] == blk[None, :], 1.0 / HEAD_DIM, 0.0).astype(BF16)
    for l in range(depth):
        wr = jnp.concatenate([w["w_coarse"][l], w["w_fine"][l]], axis=1)
        br = jnp.concatenate([w["b_coarse"][l], w["b_fine"][l]])
        pad = ROUTER_COLS - wr.shape[1]
        prep["layers"].append(dict(
            g_mix=w["g_mix"][l][None, :], g_ffn=w["g_ffn"][l][None, :],
            w_router=jnp.pad(wr, ((0, 0), (0, pad))), b_router=jnp.pad(br, (0, pad))[None, :],
            w_gate=w["w_gate"][l].astype(BF16), w_up=w["w_up"][l].astype(BF16), w_down=w["w_down"][l].astype(BF16)))
    for i in range(n_ab):
        prep["ab"].append(dict(
            w_in=jnp.pad(w["w_in"][i], ((0, 0), (0, IN_COLS_PADDED - IN_COLS))).astype(BF16),
            q_gain=jnp.tile(w["q_gain"][i], N_HEADS)[None, :], k_gain=jnp.tile(w["k_gain"][i], N_KV_HEADS)[None, :],
            disc=s5_discretise(w["lam_re"][i], w["lam_im"][i], w["log_dt"][i], w["ssm_b_re"][i], w["ssm_b_im"][i],
                               w["ssm_c_re"][i], w["ssm_c_im"][i]),
            d_skip=w["ssm_d"][i][None, :], w_glu=w["w_glu"][i].astype(BF16), b_glu=w["b_glu"][i][None, :],
            w_out=w["w_out"][i].astype(BF16)))
    for i in range(w["w_pool"].shape[0]):
        prep["c"].append(dict(w_pool=w["w_pool"][i].astype(BF16), pool_scale=w["pool_scale"][i][None, :]))
    return prep


def _run_prompt(x, mod_all, prep):
    batch, seq, _ = x.shape
    n = batch * seq
    x2 = x.reshape(n, D_MODEL)
    rope = _rope_tables(jnp.arange(seq, dtype=I32))
    tile = 512
    grp = TokenGroup(n, seq, tile)
    grp_moe = TokenGroup(n, seq, 1024)
    zero_state = jnp.zeros((batch, N_STATES), F32)
    ks, vs, kis, sres, sims, pools = [], [], [], [], [], []
    for layer in range(mod_all.shape[0]):
        i = layer // 2
        lw = prep["layers"][layer]
        mods = grp.mods(mod_all[layer])
        if layer % 2 == 0:
            ab = prep["ab"][i]
            u, q, k, v, qi, kiwi, kb, kib, vt = in_proj(x2, mods, grp, lw["g_mix"], ab["w_in"], ab["q_gain"],
                                                        ab["k_gain"], prep["mavg"], rope, seq)
            u3 = jnp.swapaxes(u.reshape(batch, seq, A_WIDTH), 0, 1)
            y3, s_re, s_im = s5_mixer(u3, zero_state, zero_state, ab["disc"], ab["d_skip"], ab["w_glu"],
                                      ab["b_glu"], tt=64)
            y_a = jnp.swapaxes(y3, 0, 1).reshape(n, A_WIDTH)
            y_b = attend_prompt(q, qi, kiwi, kb, kib, vt, batch, seq, ATTN_TQ)
            x2 = out_proj(x2, y_a, y_b, mods, grp, ab["w_out"])
            ks.append(k.reshape(batch, seq, N_KV_HEADS, HEAD_DIM))
            vs.append(v.reshape(batch, seq, N_KV_HEADS, HEAD_DIM))
            kis.append(kiwi[:, :IDX_DIM].reshape(batch, seq, IDX_DIM))
            sres.append(s_re.reshape(batch, SSM_GROUPS, SSM_STATE))
            sims.append(s_im.reshape(batch, SSM_GROUPS, SSM_STATE))
        else:
            c = prep["c"][i]
            x2, st = pool_mixer(x2, x2, mods, grp, lw["g_mix"], c["w_pool"], c["pool_scale"],
                                tiles_per_seq=seq // tile, hist_is_x=True, pos_base=0, state_rows=HIST_ROWS)
            pools.append(st[:, HIST_ROWS - POOL_STATE:, :])
        x2 = hier_moe(x2, grp_moe.mods(mod_all[layer]), grp_moe, lw["g_ffn"], lw["w_router"], lw["b_router"],
                      lw["w_gate"], lw["w_up"], lw["w_down"])
    return (x2.reshape(batch, seq, D_MODEL), jnp.stack(ks), jnp.stack(vs), jnp.stack(kis), jnp.stack(sres),
            jnp.stack(sims), jnp.stack(pools))


def _run_sample(x, mod_all, prep, cache_k, cache_v, cache_kidx, state_re, state_im, state_pool, page_table):
    batch, seq, _ = x.shape
    n = batch * seq
    past_len = page_table.shape[1] * PAGE_SIZE
    x2 = x.reshape(n, D_MODEL)
    rope = _rope_tables(jnp.tile(past_len + jnp.arange(seq, dtype=I32), batch))
    grp = TokenGroup(n, seq, n)
    rows = SAMPLE_Q_ROWS
    grp_pool = TokenGroup(batch * rows, rows, rows)
    pad_q = lambda a: jnp.pad(a.reshape(batch, seq, a.shape[-1]), ((0, 0), (0, rows - seq), (0, 0)))
    page_t = lambda a: jnp.pad(jnp.swapaxes(a.reshape(batch, seq, a.shape[-1]), 1, 2),
                               ((0, 0), (0, 0), (0, PAGE_SIZE - seq)))
    stack_heads = lambda a: jnp.swapaxes(pad_q(a).reshape(batch, rows, N_HEADS, HEAD_DIM), 1, 2).reshape(
        batch, N_HEADS * rows, HEAD_DIM)
    n_layers_ab, n_phys = cache_k.shape[:2]
    cache_k_t = jnp.transpose(cache_k, (0, 1, 3, 4, 2)).reshape(n_layers_ab, n_phys, KV_WIDTH, PAGE_SIZE)
    cache_v_t = jnp.transpose(cache_v, (0, 1, 3, 4, 2)).reshape(n_layers_ab, n_phys, KV_WIDTH, PAGE_SIZE)
    cache_ki_t = jnp.transpose(cache_kidx, (0, 1, 3, 2))
    ks, vs, kis, sres, sims, pools = [], [], [], [], [], []
    for layer in range(mod_all.shape[0]):
        i = layer // 2
        lw = prep["layers"][layer]
        mods = grp.mods(mod_all[layer])
        if layer % 2 == 0:
            ab = prep["ab"][i]
            u, q, k, v, qi, kiwi, _, _, _ = in_proj(x2, mods, grp, lw["g_mix"], ab["w_in"], ab["q_gain"],
                                                    ab["k_gain"], prep["mavg"], rope, n)
            u3 = jnp.swapaxes(u.reshape(batch, seq, A_WIDTH), 0, 1)
            y3, s_re, s_im = s5_mixer(u3, state_re[i].reshape(batch, N_STATES), state_im[i].reshape(batch, N_STATES),
                                      ab["disc"], ab["d_skip"], ab["w_glu"], ab["b_glu"], tt=seq)
            y_a = jnp.swapaxes(y3, 0, 1).reshape(n, A_WIDTH)
            o_st = attend_sample(stack_heads(q), stack_heads(qi), pad_q(kiwi), page_t(k), page_t(v),
                                 page_t(kiwi[:, :IDX_DIM]), cache_k_t, cache_v_t, cache_ki_t, page_table, i, seq)
            y_b = jnp.swapaxes(o_st.reshape(batch, N_HEADS, rows, HEAD_DIM), 1, 2)[:, :seq]
            x2 = out_proj(x2, y_a, y_b.reshape(n, 512).astype(BF16), mods, grp, ab["w_out"])
            ks.append(k.reshape(batch, seq, N_KV_HEADS, HEAD_DIM))
            vs.append(v.reshape(batch, seq, N_KV_HEADS, HEAD_DIM))
            kis.append(kiwi[:, :IDX_DIM].reshape(batch, seq, IDX_DIM))
            sres.append(s_re.reshape(batch, SSM_GROUPS, SSM_STATE))
            sims.append(s_im.reshape(batch, SSM_GROUPS, SSM_STATE))
        else:
            c = prep["c"][i]
            xp = jnp.pad(x2.reshape(batch, seq, D_MODEL), ((0, 0), (0, rows - seq), (0, 0)))
            hist = jnp.pad(state_pool[i], ((0, 0), (HIST_ROWS - POOL_STATE, 0), (0, 0)))
            xo, st = pool_mixer(xp.reshape(batch * rows, D_MODEL), hist, grp_pool.mods(mod_all[layer]), grp_pool,
                                lw["g_mix"], c["w_pool"], c["pool_scale"], tiles_per_seq=1, hist_is_x=False,
                                pos_base=past_len, state_rows=HIST_ROWS + rows)
            x2 = xo.reshape(batch, rows, D_MODEL)[:, :seq].reshape(n, D_MODEL)
            pools.append(st[:, seq + HIST_ROWS - POOL_STATE:seq + HIST_ROWS, :])
        x2 = hier_moe(x2, mods, grp, lw["g_ffn"], lw["w_router"], lw["b_router"], lw["w_gate"], lw["w_up"],
                      lw["w_down"])
    return (x2.reshape(batch, seq, D_MODEL), jnp.stack(ks), jnp.stack(vs), jnp.stack(kis), jnp.stack(sres),
            jnp.stack(sims), jnp.stack(pools))


def kernel(x_prompt, x_sample, c_prompt, c_sample, cache_k, cache_v, cache_kidx, state_ssm_re, state_ssm_im,
           state_pool, page_table, w_ada, b_ada, g_mix, g_ffn, w_in, q_gain, k_gain, lam_re, lam_im, log_dt,
           ssm_b_re, ssm_b_im, ssm_c_re, ssm_c_im, ssm_d, w_glu, b_glu, w_out, w_pool, pool_scale, w_coarse,
           b_coarse, w_fine, b_fine, w_gate, w_up, w_down):
    weights = dict(w_ada=w_ada, g_mix=g_mix, g_ffn=g_ffn, w_in=w_in, q_gain=q_gain, k_gain=k_gain, lam_re=lam_re,
                   lam_im=lam_im, log_dt=log_dt, ssm_b_re=ssm_b_re, ssm_b_im=ssm_b_im, ssm_c_re=ssm_c_re,
                   ssm_c_im=ssm_c_im, ssm_d=ssm_d, w_glu=w_glu, b_glu=b_glu, w_out=w_out, w_pool=w_pool,
                   pool_scale=pool_scale, w_coarse=w_coarse, b_coarse=b_coarse, w_fine=w_fine, b_fine=b_fine,
                   w_gate=w_gate, w_up=w_up, w_down=w_down)
    prep = _prep_weights(weights)
    n_prompt = c_prompt.shape[0]
    mod_all = ada_params_all(jnp.concatenate([c_prompt, c_sample], axis=0), w_ada, b_ada)
    out_p = _run_prompt(x_prompt, mod_all[:, :n_prompt], prep)
    out_s = _run_sample(x_sample, mod_all[:, n_prompt:], prep, cache_k, cache_v, cache_kidx, state_ssm_re,
                        state_ssm_im, state_pool, page_table)
    return (out_p[0], out_s[0]) + out_p[1:] + out_s[1:]
```

```python
import functools
import math

import jax
import jax.numpy as jnp
from jax import lax
from jax.experimental import pallas as pl
from jax.experimental.pallas import tpu as pltpu

F32 = jnp.float32
BF16 = jnp.bfloat16
I32 = jnp.int32

D_MODEL = 1024
EPS = 1e-6
A_WIDTH = 512
SSM_GROUP = 16
SSM_GROUPS = 32
SSM_STATE = 64
N_STATES = SSM_GROUPS * SSM_STATE
HEAD_DIM = 64
N_HEADS = 8
N_KV_HEADS = 2
KV_WIDTH = N_KV_HEADS * HEAD_DIM
IDX_HEADS = 8
IDX_DIM = 64
TOPK_MAX = 256
ROPE_THETA = 500000.0
ROT_HALF = 8
PAGE_SIZE = 128
POOL_WINDOWS = (2, 4, 8, 16)
POOL_GROUP_DIM = 256
POOL_STATE = 15
HIST_ROWS = 16
MOE_GROUPS = 4
EXPERTS_PER_GROUP = 4
N_EXPERTS = 16
D_FF_EXPERT = 256

OFF_Q, OFF_K, OFF_V, OFF_QI, OFF_KI, OFF_WI, IN_COLS = 512, 1024, 1152, 1280, 1792, 1856, 1864
IN_COLS_PADDED = 1920

V7X_LANES = 128
V7X_SUBLANES = 8
V7X_VMEM_BYTES = 64 * 2**20
VMEM_LIMIT_BYTES = (V7X_VMEM_BYTES * 3) // 4
INT_MIN = -(2**31)
NEG_INF = float("-inf")


def _params(*semantics):
    return pltpu.CompilerParams(dimension_semantics=semantics, vmem_limit_bytes=VMEM_LIMIT_BYTES)


def _norm_mod(x, gain, scale, shift):
    var = jnp.mean(x * x, axis=-1, keepdims=True)
    return (x * lax.rsqrt(var + EPS)) * gain * (1.0 + scale) + shift


def _silu(x):
    return x * jax.nn.sigmoid(x)


def _dot(a, b):
    return jnp.dot(a, b, preferred_element_type=F32)


def _dot_t(a, b):
    return lax.dot_general(a, b, (((1,), (1,)), ((), ())), preferred_element_type=F32)


class TokenGroup:
    def __init__(self, n_rows, rows_per_mod, tile):
        assert n_rows % rows_per_mod == 0 and (rows_per_mod % tile == 0 or tile % rows_per_mod == 0)
        self.n_rows, self.tile = n_rows, tile
        self.n_tiles = n_rows // tile
        if rows_per_mod >= tile:
            self.mod_rows = 1
            self.tiles_per_mod = rows_per_mod // tile
        else:
            self.mod_rows = tile
            self.tiles_per_mod = 1

    def mods(self, mod):
        if self.mod_rows == 1:
            return mod[:, None, :]
        rep = self.n_rows // mod.shape[0]
        return jnp.repeat(mod, rep, axis=0).reshape(self.n_tiles, self.tile, mod.shape[1])

    def mod_spec(self, piece):
        tpm = self.tiles_per_mod
        return pl.BlockSpec((None, self.mod_rows, D_MODEL), lambda i: (i // tpm, 0, piece))

    def row_spec(self, width):
        return pl.BlockSpec((self.tile, width), lambda i: (i, 0))


def _const_spec(shape):
    nd = len(shape)
    return pl.BlockSpec(shape, lambda *_: (0,) * nd)


def _ada_kernel(c_ref, w_ref, b_ref, o_ref):
    s = _silu(c_ref[...]).astype(BF16)
    o_ref[...] = _dot(s, w_ref[...].astype(BF16)) + b_ref[...]


def ada_params_all(c_all, w_ada, b_ada):
    n_layers, d, n6 = w_ada.shape
    m = c_all.shape[0]
    tn = 1536
    return pl.pallas_call(
        _ada_kernel,
        grid=(n_layers, n6 // tn),
        in_specs=[pl.BlockSpec((m, d), lambda l, j: (0, 0)),
                  pl.BlockSpec((None, d, tn), lambda l, j: (l, 0, j)),
                  pl.BlockSpec((None, 1, tn), lambda l, j: (l, 0, j))],
        out_specs=pl.BlockSpec((None, m, tn), lambda l, j: (l, 0, j)),
        out_shape=jax.ShapeDtypeStruct((n_layers, m, n6), F32),
        compiler_params=_params("arbitrary", "arbitrary"),
    )(c_all, w_ada, b_ada.reshape(n_layers, 1, n6))


def _rope(x, cos, s1, s2):
    w = x.shape[1]
    reps = w // V7X_LANES
    if reps > 1:
        cos, s1, s2 = (jnp.tile(t, (1, reps)) for t in (cos, s1, s2))
    return x * cos + pltpu.roll(x, w - ROT_HALF, 1) * s1 + pltpu.roll(x, ROT_HALF, 1) * s2


def _head_norm(x, mavg, gain):
    sq = x * x
    hi = sq.astype(BF16)
    lo = (sq - hi.astype(F32)).astype(BF16)
    var = _dot(hi, mavg) + _dot(lo, mavg)
    return (x * lax.rsqrt(var + EPS)) * gain


def _in_proj_kernel(x_ref, sc_ref, sh_ref, g_ref, w_ref, qg_ref, kg_ref, mavg_ref, rope_ref,
                    u_ref, q_ref, k_ref, v_ref, qi_ref, kiwi_ref, kb_ref, kib_ref, vt_ref):
    h = _norm_mod(x_ref[...], g_ref[...], sc_ref[...], sh_ref[...])
    p = _dot(h.astype(BF16), w_ref[...])
    cos, s1, s2 = rope_ref[0], rope_ref[1], rope_ref[2]
    mavg = mavg_ref[...]
    u_ref[...] = p[:, 0:OFF_Q]
    q = _rope(_head_norm(p[:, OFF_Q:OFF_K], mavg, qg_ref[...]), cos, s1, s2)
    q_ref[...] = (q * HEAD_DIM ** -0.5).astype(BF16)
    k = _head_norm(p[:, OFF_K:OFF_V], mavg[0:KV_WIDTH, 0:KV_WIDTH], kg_ref[...])
    k = _rope(k, cos, s1, s2)
    k_ref[...] = k
    kb_ref[...] = k.astype(BF16)
    v = p[:, OFF_V:OFF_QI]
    v_ref[...] = v
    vt_ref[...] = v.T.astype(BF16)
    qi = _rope(p[:, OFF_QI:OFF_KI], cos, s1, s2)
    qi_ref[...] = (qi * IDX_DIM ** -0.5).astype(BF16)
    kiwi = _rope(p[:, OFF_KI:IN_COLS_PADDED], rope_ref[3], rope_ref[4], rope_ref[5])
    kiwi_ref[...] = kiwi
    kib_ref[...] = kiwi.astype(BF16)


def _rope_tables(pos):
    t = pos.shape[0]
    inv = ROPE_THETA ** (-jnp.arange(ROT_HALF, dtype=F32) / ROT_HALF)
    ang = pos.astype(F32)[:, None] * inv[None, :]
    cos, sin = jnp.cos(ang), jnp.sin(ang)
    rest = HEAD_DIM - 2 * ROT_HALF
    z8, zr, onesr = jnp.zeros((t, ROT_HALF), F32), jnp.zeros((t, rest), F32), jnp.ones((t, rest), F32)
    c64 = jnp.concatenate([cos, cos, onesr], axis=1)
    s1_64 = jnp.concatenate([-sin, z8, zr], axis=1)
    s2_64 = jnp.concatenate([z8, sin, zr], axis=1)
    z64 = jnp.zeros((t, HEAD_DIM), F32)
    wscale = jnp.concatenate([jnp.full((t, IDX_HEADS), IDX_HEADS ** -0.5, F32),
                              jnp.ones((t, HEAD_DIM - IDX_HEADS), F32)], axis=1)
    return jnp.stack([jnp.concatenate([c64, c64], axis=1), jnp.concatenate([s1_64, s1_64], axis=1),
                      jnp.concatenate([s2_64, s2_64], axis=1), jnp.concatenate([c64, wscale], axis=1),
                      jnp.concatenate([s1_64, z64], axis=1), jnp.concatenate([s2_64, z64], axis=1)])


def in_proj(x2, mods, grp, g_mix, w_in_p, q_gain, k_gain, mavg, rope, seq_rows):
    n = x2.shape[0]
    tm = grp.tile
    rope_tiles = rope.shape[1] // tm
    tiles_per_seq = seq_rows // tm
    outs = pl.pallas_call(
        _in_proj_kernel,
        grid=(grp.n_tiles,),
        in_specs=[grp.row_spec(D_MODEL), grp.mod_spec(1), grp.mod_spec(0), _const_spec((1, D_MODEL)),
                  _const_spec((D_MODEL, IN_COLS_PADDED)), _const_spec((1, 512)), _const_spec((1, KV_WIDTH)),
                  _const_spec((512, 512)),
                  pl.BlockSpec((6, tm, V7X_LANES), lambda i: (0, i % rope_tiles, 0))],
        out_specs=[grp.row_spec(512), grp.row_spec(512), grp.row_spec(KV_WIDTH), grp.row_spec(KV_WIDTH),
                   grp.row_spec(512), grp.row_spec(V7X_LANES), grp.row_spec(KV_WIDTH), grp.row_spec(V7X_LANES),
                   pl.BlockSpec((None, KV_WIDTH, tm), lambda i: (i // tiles_per_seq, 0, i % tiles_per_seq))],
        out_shape=[jax.ShapeDtypeStruct((n, 512), F32), jax.ShapeDtypeStruct((n, 512), BF16),
                   jax.ShapeDtypeStruct((n, KV_WIDTH), F32), jax.ShapeDtypeStruct((n, KV_WIDTH), F32),
                   jax.ShapeDtypeStruct((n, 512), BF16), jax.ShapeDtypeStruct((n, V7X_LANES), F32),
                   jax.ShapeDtypeStruct((n, KV_WIDTH), BF16), jax.ShapeDtypeStruct((n, V7X_LANES), BF16),
                   jax.ShapeDtypeStruct((n // seq_rows, KV_WIDTH, seq_rows), BF16)],
        compiler_params=_params("arbitrary"),
    )(x2, mods, mods, g_mix, w_in_p, q_gain, k_gain, mavg, rope)
    return outs


S5_BATCH_BLOCK = V7X_SUBLANES
S5_HALF_IN = A_WIDTH // 2
S5_HALF_STATES = N_STATES // 2
S5_SCAN_COLS = 1024


def _gelu_tanh(x):
    cdf = 0.5 * (1.0 + jnp.tanh(math.sqrt(2.0 / math.pi) * (x + 0.044715 * (x * x * x))))
    return x * cdf


def _s5_kernel(u_ref, x0re_ref, x0im_ref, are_ref, aim_ref, bre_ref, bim_ref, cre_ref, cim_ref, d_ref,
               wglu_ref, bglu_ref, y_ref, sre_ref, sim_ref, st_re, st_im, xs_re, xs_im, *, tt):
    bb = S5_BATCH_BLOCK
    tb = pl.program_id(1)

    @pl.when(tb == 0)
    def _():
        st_re[...] = x0re_ref[...]
        st_im[...] = x0im_ref[...]

    u = u_ref[...].reshape(tt * bb, A_WIDTH)
    ub = u.astype(BF16)
    for half in range(2):
        rows = slice(half * S5_HALF_IN, (half + 1) * S5_HALF_IN)
        cols = slice(half * S5_HALF_STATES, (half + 1) * S5_HALF_STATES)
        xs_re[:, cols] = _dot(ub[:, rows], bre_ref[rows, cols])
        xs_im[:, cols] = _dot(ub[:, rows], bim_ref[rows, cols])

    for c in range(N_STATES // S5_SCAN_COLS):
        cs = slice(c * S5_SCAN_COLS, (c + 1) * S5_SCAN_COLS)
        a_re = are_ref[:, cs]
        a_im = aim_ref[:, cs]

        def step(t, carry, cs=cs, a_re=a_re, a_im=a_im):
            sr, si = carry
            r0 = pl.multiple_of(t * bb, bb)
            nr = a_re * sr - a_im * si + xs_re[pl.ds(r0, bb), cs]
            ni = a_re * si + a_im * sr + xs_im[pl.ds(r0, bb), cs]
            xs_re[pl.ds(r0, bb), cs] = nr
            xs_im[pl.ds(r0, bb), cs] = ni
            return nr, ni

        sr, si = lax.fori_loop(0, tt, step, (st_re[:, cs], st_im[:, cs]))
        st_re[:, cs] = sr
        st_im[:, cs] = si

    xr = xs_re[...].astype(BF16)
    xi = xs_im[...].astype(BF16)
    ys = []
    for half in range(2):
        rows = slice(half * S5_HALF_STATES, (half + 1) * S5_HALF_STATES)
        cols = slice(half * S5_HALF_IN, (half + 1) * S5_HALF_IN)
        ys.append(_dot(xr[:, rows], cre_ref[rows, cols]) - _dot(xi[:, rows], cim_ref[rows, cols]))
    y = jnp.concatenate(ys, axis=1) + d_ref[...] * u
    z = _gelu_tanh(y)
    o = z * jax.nn.sigmoid(_dot(z.astype(BF16), wglu_ref[...]) + bglu_ref[...])
    y_ref[...] = o.reshape(tt, bb, A_WIDTH)

    @pl.when(tb == pl.num_programs(1) - 1)
    def _():
        sre_ref[...] = st_re[...]
        sim_ref[...] = st_im[...]


def s5_discretise(lam_re, lam_im, log_dt, b_re, b_im, c_re, c_im):
    dt = jnp.exp(log_dt)[:, None]
    mag = jnp.exp(lam_re * dt)
    abar_re, abar_im = mag * jnp.cos(lam_im * dt), mag * jnp.sin(lam_im * dt)
    den = lam_re * lam_re + lam_im * lam_im
    ir, ii = lam_re / den, -lam_im / den
    cr = (abar_re - 1.0) * ir - abar_im * ii
    ci = (abar_re - 1.0) * ii + abar_im * ir
    bb_re = cr[..., None] * b_re - ci[..., None] * b_im
    bb_im = cr[..., None] * b_im + ci[..., None] * b_re
    eye = jnp.eye(SSM_GROUPS, dtype=F32)
    pack_b = lambda bb: jnp.einsum("gpn,gh->gnhp", bb, eye).reshape(A_WIDTH, N_STATES).astype(BF16)
    pack_c = lambda cc: jnp.einsum("gnp,gh->gphn", cc, eye).reshape(N_STATES, A_WIDTH).astype(BF16)
    bcast = lambda a: jnp.broadcast_to(a.reshape(1, N_STATES), (S5_BATCH_BLOCK, N_STATES))
    return bcast(abar_re), bcast(abar_im), pack_b(bb_re), pack_b(bb_im), pack_c(c_re), pack_c(c_im)


def s5_mixer(u3, x0_re, x0_im, disc, d_skip, w_glu, b_glu, tt):
    t, b, _ = u3.shape
    bb = S5_BATCH_BLOCK
    a_re, a_im, bre, bim, cre, cim = disc
    state_spec = pl.BlockSpec((bb, N_STATES), lambda i, j: (i, 0))
    seq_spec = pl.BlockSpec((tt, bb, A_WIDTH), lambda i, j: (j, i, 0))
    return pl.pallas_call(
        functools.partial(_s5_kernel, tt=tt),
        grid=(b // bb, t // tt),
        in_specs=[seq_spec, state_spec, state_spec, _const_spec((bb, N_STATES)), _const_spec((bb, N_STATES)),
                  _const_spec((A_WIDTH, N_STATES)), _const_spec((A_WIDTH, N_STATES)),
                  _const_spec((N_STATES, A_WIDTH)), _const_spec((N_STATES, A_WIDTH)),
                  _const_spec((1, A_WIDTH)), _const_spec((A_WIDTH, A_WIDTH)), _const_spec((1, A_WIDTH))],
        out_specs=[seq_spec, state_spec, state_spec],
        out_shape=[jax.ShapeDtypeStruct((t, b, A_WIDTH), F32), jax.ShapeDtypeStruct((b, N_STATES), F32),
                   jax.ShapeDtypeStruct((b, N_STATES), F32)],
        scratch_shapes=[pltpu.VMEM((bb, N_STATES), F32), pltpu.VMEM((bb, N_STATES), F32),
                        pltpu.VMEM((tt * bb, N_STATES), F32), pltpu.VMEM((tt * bb, N_STATES), F32)],
        compiler_params=_params("arbitrary", "arbitrary"),
    )(u3, x0_re, x0_im, a_re, a_im, bre, bim, cre, cim, d_skip, w_glu, b_glu)


def _count(mask):
    return jnp.sum(jnp.where(mask, 1.0, 0.0), axis=1, keepdims=True)


def _topk_bias(sc, keep, qpos0, key_ref, bias_ref):
    m, s_keys = sc.shape
    keep = float(keep)
    kpos = lax.broadcasted_iota(I32, (m, s_keys), 1)
    qpos = qpos0 + lax.broadcasted_iota(I32, (m, 1), 0)
    causal = kpos <= qpos
    bits = lax.bitcast_convert_type(sc, I32)
    mag = bits & 0x7FFFFFFF
    key_ref[...] = jnp.where(causal, jnp.where(bits < 0, -mag, mag), INT_MIN)

    def two_bits(it, thr):
        unit = lax.shift_left(jnp.int32(1), 30 - 2 * it)
        key = key_ref[...]
        new = thr
        for cand in (thr + unit, thr + 2 * unit, thr + 3 * unit):
            new = jnp.where(_count(key >= cand) >= keep, cand, new)
        return new

    thr = lax.fori_loop(0, 16, two_bits, jnp.full((m, 1), INT_MIN, I32))
    key = key_ref[...]
    gt = key > thr
    eq = key == thr
    need = keep - _count(gt)
    bias_ref[...] = jnp.where((gt | eq) & causal, 0.0, NEG_INF)
    tie = (_count(eq) > need) & (thr > INT_MIN)

    @pl.when(jnp.max(jnp.where(tie, 1.0, 0.0)) > 0.0)
    def _():
        def idx_step(_, lohi):
            lo, hi = lohi
            mid = (lo + hi) >> 1
            ok = _count((key_ref[...] == thr) & (kpos <= mid)) >= need
            return jnp.where(ok, lo, mid + 1), jnp.where(ok, mid, hi)

        n_steps = max(1, (s_keys - 1).bit_length())
        last, _ = lax.fori_loop(0, n_steps, idx_step,
                                (jnp.zeros((m, 1), I32), jnp.full((m, 1), s_keys - 1, I32)))
        k2 = key_ref[...]
        sel = (k2 > thr) | ((k2 == thr) & (kpos <= last))
        bias_ref[...] = jnp.where(sel & causal, 0.0, NEG_INF)


def _select_attend_t(s_keys, keep, qpos0, q_hm, qi_hm, wi_t, ki_ref, k_ref, vt_ref, key_ref, bias_ref, o_ref):
    tq = o_ref.shape[0]
    ck = ATTN_KEY_CHUNK if s_keys % ATTN_KEY_CHUNK == 0 else tq
    assert s_keys % ck == 0
    chunks = [slice(c * ck, (c + 1) * ck) for c in range(s_keys // ck)]
    group = N_HEADS // N_KV_HEADS
    keep = float(keep)
    qpos = qpos0 + lax.broadcasted_iota(I32, (1, tq), 1)
    lanes = lambda a, j: a[:, j * tq:(j + 1) * tq]
    fold8 = lambda a, op: op(a.reshape(a.shape[0] // V7X_SUBLANES, V7X_SUBLANES, a.shape[1]), axis=0)

    for c, rows in enumerate(chunks):
        ki_c = ki_ref[rows, 0:IDX_DIM]
        sc = jnp.zeros((ck, tq), F32)
        for h in range(IDX_HEADS):
            sc = sc + jnp.maximum(_dot_t(ki_c, qi_hm[h * tq:(h + 1) * tq, :]), 0.0) * wi_t[h:h + 1, :]
        kpos = c * ck + lax.broadcasted_iota(I32, (ck, tq), 0)
        bits = lax.bitcast_convert_type(sc, I32)
        mag = bits & 0x7FFFFFFF
        key_ref[rows, :] = jnp.where(kpos <= qpos, jnp.where(bits < 0, -mag, mag), INT_MIN)

    ones = jnp.ones((V7X_SUBLANES, s_keys), BF16)

    def counts(conds):
        mask = jnp.concatenate([jnp.where(m, 1.0, 0.0).astype(BF16) for m in conds], axis=1)
        n = _dot(ones, mask)[0:1, :]
        return [lanes(n, j) for j in range(len(conds))]

    def two_bits(it, carry):
        thr, cnt = carry
        unit = lax.shift_left(jnp.int32(1), 30 - 2 * it)
        cands = [thr + unit, thr + 2 * unit, thr + 3 * unit]
        key = key_ref[...]
        for cand, n in zip(cands, counts([key >= cand for cand in cands])):
            ok = n >= keep
            thr, cnt = jnp.where(ok, cand, thr), jnp.where(ok, n, cnt)
        return thr, cnt

    start = (jnp.full((1, tq), INT_MIN, I32), jnp.full((1, tq), s_keys, F32))
    thr, cnt = lax.cond(qpos0 + tq > int(keep), lambda: lax.fori_loop(0, 16, two_bits, start), lambda: start)
    bias_ref[...] = jnp.where(key_ref[...] >= jnp.maximum(thr, INT_MIN + 1), 0.0, NEG_INF)
    tie = (cnt > keep) & (thr > INT_MIN)

    @pl.when(jnp.max(jnp.where(tie, 1.0, 0.0)) > 0.0)
    def _():
        kpos = lax.broadcasted_iota(I32, (s_keys, tq), 0)
        need = keep - counts([key_ref[...] > thr])[0]

        def idx_step(_, lohi):
            lo, hi = lohi
            mid = (lo + hi) >> 1
            ok = counts([(key_ref[...] == thr) & (kpos <= mid)])[0] >= need
            return jnp.where(ok, lo, mid + 1), jnp.where(ok, mid, hi)

        n_steps = max(1, (s_keys - 1).bit_length())
        last, _ = lax.fori_loop(0, n_steps, idx_step,
                                (jnp.zeros((1, tq), I32), jnp.full((1, tq), s_keys - 1, I32)))
        k2 = key_ref[...]
        sel = (k2 > thr) | ((k2 == thr) & (kpos <= last))
        bias_ref[...] = jnp.where(sel & (k2 > INT_MIN), 0.0, NEG_INF)

    outs = []
    for g in range(N_KV_HEADS):
        q4 = q_hm[g * group * tq:(g + 1) * group * tq, :]
        feats = slice(g * HEAD_DIM, (g + 1) * HEAD_DIM)
        score = lambda rows: _dot_t(k_ref[rows, feats], q4) + jnp.tile(bias_ref[rows, :], (1, group))
        m8 = None
        for rows in chunks:
            mc = fold8(score(rows), jnp.max)
            m8 = mc if m8 is None else jnp.maximum(m8, mc)
        m = jnp.max(m8, axis=0, keepdims=True)
        l8 = jnp.zeros((V7X_SUBLANES, group * tq), F32)
        ot = jnp.zeros((HEAD_DIM, group * tq), F32)
        for rows in chunks:
            p = jnp.exp(score(rows) - m)
            l8 = l8 + fold8(p, jnp.sum)
            ot = ot + _dot(vt_ref[feats, rows], p.astype(BF16))
        ot = ot / jnp.sum(l8, axis=0, keepdims=True)
        outs.extend(lanes(ot, h) for h in range(group))
    o_ref[...] = jnp.concatenate(outs, axis=0).T.astype(o_ref.dtype)


def _attn_prompt_kernel(q_ref, qi_ref, wit_ref, k_ref, vt_ref, ki_ref, o_ref, key_ref, bias_ref, *, tq, buckets,
                        keep):
    i = pl.program_id(1)
    need_keys = (i + 1) * tq
    lo = 0
    for s_keys in buckets:
        @pl.when((need_keys > lo) & (need_keys <= s_keys))
        def _(s_keys=s_keys):
            _select_attend_t(s_keys, keep, i * tq, q_ref[...], qi_ref[...], wit_ref[...], ki_ref, k_ref, vt_ref,
                             key_ref.at[0:s_keys, :], bias_ref.at[0:s_keys, :], o_ref)
        lo = s_keys


def attend_prompt(q, qi, kiwi, kb, kib, vt, batch, seq, tq, n_buckets=8):
    n = q.shape[0]
    nq = seq // tq
    step = max(tq, seq // n_buckets)
    buckets = tuple(range(step, seq + 1, step))
    head_major = lambda a: jnp.swapaxes(a.reshape(n // tq, tq, N_HEADS, HEAD_DIM), 1, 2).reshape(
        n // tq, N_HEADS * tq, HEAD_DIM)
    wi_t = jnp.swapaxes(kiwi[:, IDX_DIM:IDX_DIM + IDX_HEADS].reshape(n // tq, tq, IDX_HEADS), 1, 2)
    hm_spec = pl.BlockSpec((None, N_HEADS * tq, HEAD_DIM), lambda b, i: (b * nq + i, 0, 0))
    kv_spec = pl.BlockSpec((seq, KV_WIDTH), lambda b, i: (b, 0))
    return pl.pallas_call(
        functools.partial(_attn_prompt_kernel, tq=tq, buckets=buckets, keep=min(TOPK_MAX, seq // 4)),
        grid=(batch, nq),
        in_specs=[hm_spec, hm_spec, pl.BlockSpec((None, IDX_HEADS, tq), lambda b, i: (b * nq + i, 0, 0)),
                  kv_spec, pl.BlockSpec((None, KV_WIDTH, seq), lambda b, i: (b, 0, 0)), kv_spec],
        out_specs=pl.BlockSpec((tq, 512), lambda b, i: (b * nq + i, 0)),
        out_shape=jax.ShapeDtypeStruct((n, 512), BF16),
        scratch_shapes=[pltpu.VMEM((seq, tq), I32), pltpu.VMEM((seq, tq), F32)],
        compiler_params=_params("arbitrary", "arbitrary"),
    )(head_major(q), head_major(qi), wi_t, kb, vt, kib)


ATTN_TQ = 128
ATTN_KEY_CHUNK = 256
SAMPLE_Q_ROWS = V7X_SUBLANES


def _attn_sample_kernel(pt_ref, q_ref, qi_ref, wi_ref, knew_ref, vnew_ref, kinew_ref, ck_hbm, cv_hbm, cki_hbm,
                        o_ref, k_all, v_all, ki_all, sems, key_ref, bias_ref, *, layer, n_pages, keep):
    b = pl.program_id(0)
    slot = b % 2
    past = n_pages * PAGE_SIZE

    def page_copies(seq, sl):
        copies = []
        for p in range(n_pages):
            phys = pt_ref[seq, p]
            window = pl.ds(p * PAGE_SIZE, PAGE_SIZE)
            copies.append(pltpu.make_async_copy(ck_hbm.at[layer, phys], k_all.at[sl, :, window], sems.at[sl, 0]))
            copies.append(pltpu.make_async_copy(cv_hbm.at[layer, phys], v_all.at[sl, :, window], sems.at[sl, 1]))
            copies.append(pltpu.make_async_copy(cki_hbm.at[layer, phys], ki_all.at[sl, :, window], sems.at[sl, 2]))
        return copies

    @pl.when(b == 0)
    def _():
        for c in page_copies(0, 0):
            c.start()

    @pl.when(b + 1 < pl.num_programs(0))
    def _():
        for c in page_copies(b + 1, 1 - slot):
            c.start()

    k_all[slot, :, past:past + PAGE_SIZE] = knew_ref[...]
    v_all[slot, :, past:past + PAGE_SIZE] = vnew_ref[...]
    ki_all[slot, :, past:past + PAGE_SIZE] = kinew_ref[...]
    for c in page_copies(b, slot):
        c.wait()
    rows = SAMPLE_Q_ROWS
    d = _dot(qi_ref[...], ki_all[slot].astype(BF16))
    wi = wi_ref[:, IDX_DIM:IDX_DIM + IDX_HEADS]
    sc = jnp.zeros((rows, past + PAGE_SIZE), F32)
    for h in range(IDX_HEADS):
        sc = sc + jnp.maximum(d[h * rows:(h + 1) * rows, :], 0.0) * wi[:, h:h + 1]
    _topk_bias(sc, keep, past, key_ref, bias_ref)
    group = N_HEADS // N_KV_HEADS
    bias = jnp.tile(bias_ref[...], (group, 1))
    q = q_ref[...]
    for g in range(N_KV_HEADS):
        feats = slice(g * HEAD_DIM, (g + 1) * HEAD_DIM)
        qrows = slice(g * group * rows, (g + 1) * group * rows)
        s = _dot(q[qrows, :], k_all[slot, feats, :].astype(BF16)) + bias
        p = jnp.exp(s - jnp.max(s, axis=1, keepdims=True))
        l = jnp.sum(p, axis=1, keepdims=True)
        o_ref[qrows, :] = _dot_t(p.astype(BF16), v_all[slot, feats, :].astype(BF16)) / l


def attend_sample(q_st, qi_st, kiwi3, k_new_t, v_new_t, ki_new_t, cache_k_t, cache_v_t, cache_ki_t, page_table,
                  layer, seq):
    b = q_st.shape[0]
    n_pages = page_table.shape[1]
    s_keys = (n_pages + 1) * PAGE_SIZE
    rows = SAMPLE_Q_ROWS
    st_spec = pl.BlockSpec((None, N_HEADS * rows, HEAD_DIM), lambda i, pt: (i, 0, 0))
    q_spec = lambda w: pl.BlockSpec((None, rows, w), lambda i, pt: (i, 0, 0))
    new_spec = lambda f: pl.BlockSpec((None, f, PAGE_SIZE), lambda i, pt: (i, 0, 0))
    hbm = pl.BlockSpec(memory_space=pl.ANY)
    grid_spec = pltpu.PrefetchScalarGridSpec(
        num_scalar_prefetch=1,
        grid=(b,),
        in_specs=[st_spec, st_spec, q_spec(V7X_LANES), new_spec(KV_WIDTH), new_spec(KV_WIDTH),
                  new_spec(IDX_DIM), hbm, hbm, hbm],
        out_specs=st_spec,
        scratch_shapes=[pltpu.VMEM((2, KV_WIDTH, s_keys), F32), pltpu.VMEM((2, KV_WIDTH, s_keys), F32),
                        pltpu.VMEM((2, IDX_DIM, s_keys), F32), pltpu.SemaphoreType.DMA((2, 3)),
                        pltpu.VMEM((rows, s_keys), I32), pltpu.VMEM((rows, s_keys), F32)],
    )
    keep = min(TOPK_MAX, (n_pages * PAGE_SIZE + seq) // 4)
    return pl.pallas_call(
        functools.partial(_attn_sample_kernel, layer=layer, n_pages=n_pages, keep=keep),
        grid_spec=grid_spec,
        out_shape=jax.ShapeDtypeStruct((b, N_HEADS * rows, HEAD_DIM), F32),
        compiler_params=_params("arbitrary"),
    )(page_table, q_st, qi_st, kiwi3, k_new_t, v_new_t, ki_new_t, cache_k_t, cache_v_t, cache_ki_t)


def _out_proj_kernel(x_ref, ya_ref, yb_ref, gate_ref, w_ref, o_ref):
    y = _dot(ya_ref[...].astype(BF16), w_ref[0:A_WIDTH, :]) + _dot(yb_ref[...], w_ref[A_WIDTH:2 * A_WIDTH, :])
    o_ref[...] = x_ref[...] + gate_ref[...] * y


def out_proj(x2, y_a, y_b, mods, grp, w_out):
    return pl.pallas_call(
        _out_proj_kernel,
        grid=(grp.n_tiles,),
        in_specs=[grp.row_spec(D_MODEL), grp.row_spec(A_WIDTH), grp.row_spec(512), grp.mod_spec(2),
                  _const_spec((2 * A_WIDTH, D_MODEL))],
        out_specs=grp.row_spec(D_MODEL),
        out_shape=jax.ShapeDtypeStruct(x2.shape, F32),
        compiler_params=_params("arbitrary"),
    )(x2, y_a, y_b, mods, w_out)


def _pool_kernel(x_ref, hist_ref, sc_ref, sh_ref, gate_ref, g_ref, w_ref, ps_ref, o_ref, st_ref, *,
                 tiles_per_seq, hist_is_x, pos_base, state_rows):
    i = pl.program_id(0)
    tm = x_ref.shape[0]
    x = x_ref[...]
    h = _norm_mod(x, g_ref[...], sc_ref[...], sh_ref[...])
    if hist_is_x:
        hist = _norm_mod(hist_ref[...], g_ref[...], sc_ref[...], sh_ref[...])
        hist = jnp.where(i % tiles_per_seq == 0, 0.0, hist)
    else:
        hist = hist_ref[...]
    ext = jnp.concatenate([hist, h], axis=0)
    pos = pos_base + (i % tiles_per_seq) * tm + lax.broadcasted_iota(I32, (tm, 1), 0)
    ys = []
    for g, win in enumerate(POOL_WINDOWS):
        cols = slice(g * POOL_GROUP_DIM, (g + 1) * POOL_GROUP_DIM)
        s = ext[:, cols]
        shift = 1
        while shift < win:
            s = s + pltpu.roll(s, shift, 0)
            shift *= 2
        cnt = jnp.minimum(pos + 1, win).astype(F32)
        pooled = s[HIST_ROWS:, :] / cnt - h[:, cols]
        ys.append(_dot(pooled.astype(BF16), w_ref[g]))
    y = jnp.concatenate(ys, axis=1) * ps_ref[...]
    o_ref[...] = x + gate_ref[...] * y
    st_ref[...] = ext[HIST_ROWS + tm - state_rows:, :]


def pool_mixer(x2, hist, mods, grp, g_mix, w_pool, pool_scale, *, tiles_per_seq, hist_is_x, pos_base, state_rows):
    tm = grp.tile
    n_seq = grp.n_tiles // tiles_per_seq
    if hist_is_x:
        per_tile = tm // HIST_ROWS
        hist_spec = pl.BlockSpec((HIST_ROWS, D_MODEL), lambda i: (jnp.maximum(i * per_tile - 1, 0), 0))
    else:
        hist_spec = pl.BlockSpec((None, HIST_ROWS, D_MODEL), lambda i: (i, 0, 0))
    return pl.pallas_call(
        functools.partial(_pool_kernel, tiles_per_seq=tiles_per_seq, hist_is_x=hist_is_x, pos_base=pos_base,
                          state_rows=state_rows),
        grid=(grp.n_tiles,),
        in_specs=[grp.row_spec(D_MODEL), hist_spec, grp.mod_spec(1), grp.mod_spec(0), grp.mod_spec(2),
                  _const_spec((1, D_MODEL)), _const_spec((len(POOL_WINDOWS), POOL_GROUP_DIM, POOL_GROUP_DIM)),
                  _const_spec((1, D_MODEL))],
        out_specs=[grp.row_spec(D_MODEL),
                   pl.BlockSpec((None, state_rows, D_MODEL), lambda i: (i // tiles_per_seq, 0, 0))],
        out_shape=[jax.ShapeDtypeStruct(x2.shape, F32), jax.ShapeDtypeStruct((n_seq, state_rows, D_MODEL), F32)],
        compiler_params=_params("arbitrary"),
    )(x2, hist, mods, mods, mods, g_mix, w_pool, pool_scale)


ROUTER_COLS = V7X_LANES


MOE_GROUP_ROW = EXPERTS_PER_GROUP
ROUTE_ROWS = V7X_SUBLANES
MOE_CHUNK = 256
MOE_SEG_ALIGN = V7X_SUBLANES


def _router_gates(logits_t):
    col = lambda j: logits_t[j:j + 1, :]
    lc = [col(j) for j in range(MOE_GROUPS)]
    mc = functools.reduce(jnp.maximum, lc)
    pg = 1.0 / functools.reduce(lambda a, b: a + b, [jnp.exp(l - mc) for l in lc])
    grp = jnp.where(lc[0] == mc, 0, jnp.where(lc[1] == mc, 1, jnp.where(lc[2] == mc, 2, 3)))
    fl = []
    for j in range(EXPERTS_PER_GROUP):
        cands = [col(MOE_GROUPS + g * EXPERTS_PER_GROUP + j) for g in range(MOE_GROUPS)]
        fl.append(jnp.where(grp == 0, cands[0], jnp.where(grp == 1, cands[1],
                                                          jnp.where(grp == 2, cands[2], cands[3]))))
    first = lambda vals, mx: jnp.where(vals[0] == mx, 0, jnp.where(vals[1] == mx, 1, jnp.where(vals[2] == mx, 2, 3)))
    m1 = functools.reduce(jnp.maximum, fl)
    i1 = first(fl, m1)
    rest = [jnp.where(i1 == j, NEG_INF, fl[j]) for j in range(EXPERTS_PER_GROUP)]
    m2 = functools.reduce(jnp.maximum, rest)
    i2 = first(rest, m2)
    e2 = jnp.exp(m2 - m1)
    w1 = pg / (1.0 + e2)
    w2 = pg * e2 / (1.0 + e2)
    return grp, [jnp.where(i1 == j, w1, jnp.where(i2 == j, w2, 0.0)) for j in range(EXPERTS_PER_GROUP)]


def _route_kernel(x_ref, sc_ref, sh_ref, g_ref, wr_ref, br_ref, o_ref):
    h = _norm_mod(x_ref[...], g_ref[...], sc_ref[...], sh_ref[...])
    logits = jnp.dot(h, wr_ref[...], preferred_element_type=F32, precision=lax.Precision.HIGHEST) + br_ref[...]
    grp, gates = _router_gates(logits.T)
    sub = lax.broadcasted_iota(I32, o_ref.shape, 0)
    out = jnp.where(sub == MOE_GROUP_ROW, grp.astype(F32), 0.0)
    for j, gj in enumerate(gates):
        out = jnp.where(sub == j, gj, out)
    o_ref[...] = out


def moe_route(x2, mods, grp, g_ffn, w_router, b_router):
    return pl.pallas_call(
        _route_kernel,
        grid=(grp.n_tiles,),
        in_specs=[grp.row_spec(D_MODEL), grp.mod_spec(4), grp.mod_spec(3), _const_spec((1, D_MODEL)),
                  _const_spec((D_MODEL, ROUTER_COLS)), _const_spec((1, ROUTER_COLS))],
        out_specs=pl.BlockSpec((ROUTE_ROWS, grp.tile), lambda i: (0, i)),
        out_shape=jax.ShapeDtypeStruct((ROUTE_ROWS, x2.shape[0]), F32),
        compiler_params=_params("arbitrary"),
    )(x2, mods, mods, g_ffn, w_router, b_router)


def _route_plan(rout_t, tm):
    n_tiles = rout_t.shape[1] // tm
    grp = rout_t[MOE_GROUP_ROW].astype(I32).reshape(n_tiles, tm)
    onehot = (grp[:, :, None] == jnp.arange(MOE_GROUPS, dtype=I32)[None, None, :]).astype(I32)
    cnt = jnp.sum(onehot, axis=1)
    padded = (cnt + MOE_SEG_ALIGN - 1) // MOE_SEG_ALIGN * MOE_SEG_ALIGN
    start = jnp.cumsum(padded, axis=1) - padded
    before = jnp.cumsum(onehot, axis=1) - onehot
    dest = jnp.sum(onehot * (before + start[:, None, :]), axis=2)
    return dest.reshape(-1), start.reshape(-1), cnt.reshape(-1)


def _moe_kernel(dest_ref, start_ref, cnt_ref, x_ref, sc_ref, sh_ref, gate_ref, g_ref, rout_ref, wg_ref, wu_ref,
                wd_ref, o_ref, hs_ref, gs_ref, ys_ref, *, chunk):
    i = pl.program_id(0)
    g = pl.program_id(1)
    tm = x_ref.shape[0]
    base = i * tm

    @pl.when((i == 0) & (g == 0))
    def _():
        hs_ref[...] = jnp.zeros_like(hs_ref)
        gs_ref[...] = jnp.zeros_like(gs_ref)

    @pl.when(g == 0)
    def _():
        o_ref[...] = _norm_mod(x_ref[...], g_ref[...], sc_ref[...], sh_ref[...])

        def move(r, carry):
            d = dest_ref[base + r]
            hs_ref[pl.ds(d, 1), :] = o_ref[pl.ds(r, 1), :]
            gs_ref[pl.ds(d, 1), :] = rout_ref[pl.ds(r, 1), :]
            return carry

        lax.fori_loop(0, tm, move, 0, unroll=8)

    start = start_ref[i * MOE_GROUPS + g]
    cnt = cnt_ref[i * MOE_GROUPS + g]

    def run_chunk(c, carry):
        r0 = pl.multiple_of(start + c * chunk, MOE_SEG_ALIGN)
        xs = hs_ref[pl.ds(r0, chunk), :].astype(BF16)
        gates = gs_ref[pl.ds(r0, chunk), :]
        valid = c * chunk + lax.broadcasted_iota(I32, (chunk, 1), 0) < cnt
        y = jnp.zeros((chunk, D_MODEL), F32)
        for e in range(EXPERTS_PER_GROUP):
            hid = _silu(_dot(xs, wg_ref[e])) * _dot(xs, wu_ref[e]) * gates[:, e:e + 1]
            y = y + _dot(jnp.where(valid, hid, 0.0).astype(BF16), wd_ref[e])
        ys_ref[pl.ds(r0, chunk), :] = y
        return carry

    lax.fori_loop(0, (cnt + chunk - 1) // chunk, run_chunk, 0)

    @pl.when(g == pl.num_programs(1) - 1)
    def _():
        def move_back(r, carry):
            d = dest_ref[base + r]
            o_ref[pl.ds(r, 1), :] = ys_ref[pl.ds(d, 1), :]
            return carry

        lax.fori_loop(0, tm, move_back, 0, unroll=8)
        o_ref[...] = x_ref[...] + gate_ref[...] * o_ref[...]


def hier_moe(x2, mods, grp, g_ffn, w_router, b_router, w_gate, w_up, w_down):
    tm = grp.tile
    chunk = min(MOE_CHUNK, tm)
    rout_t = moe_route(x2, mods, grp, g_ffn, w_router, b_router)
    dest, start, cnt = _route_plan(rout_t, tm)
    rout = rout_t.T
    buf_rows = tm + chunk + MOE_GROUPS * MOE_SEG_ALIGN
    row = lambda w: pl.BlockSpec((tm, w), lambda i, g, *_: (i, 0))
    tpm = grp.tiles_per_mod
    mod_spec = lambda piece: pl.BlockSpec((None, grp.mod_rows, D_MODEL), lambda i, g, *_: (i // tpm, 0, piece))
    epg = EXPERTS_PER_GROUP
    grid_spec = pltpu.PrefetchScalarGridSpec(
        num_scalar_prefetch=3,
        grid=(grp.n_tiles, MOE_GROUPS),
        in_specs=[row(D_MODEL), mod_spec(4), mod_spec(3), mod_spec(5),
                  pl.BlockSpec((1, D_MODEL), lambda i, g, *_: (0, 0)), row(ROUTE_ROWS),
                  pl.BlockSpec((epg, D_MODEL, D_FF_EXPERT), lambda i, g, *_: (g, 0, 0)),
                  pl.BlockSpec((epg, D_MODEL, D_FF_EXPERT), lambda i, g, *_: (g, 0, 0)),
                  pl.BlockSpec((epg, D_FF_EXPERT, D_MODEL), lambda i, g, *_: (g, 0, 0))],
        out_specs=row(D_MODEL),
        scratch_shapes=[pltpu.VMEM((buf_rows, D_MODEL), F32), pltpu.VMEM((buf_rows, ROUTE_ROWS), F32),
                        pltpu.VMEM((buf_rows, D_MODEL), F32)],
    )
    return pl.pallas_call(
        functools.partial(_moe_kernel, chunk=chunk),
        grid_spec=grid_spec,
        out_shape=jax.ShapeDtypeStruct(x2.shape, F32),
        compiler_params=_params("arbitrary", "arbitrary"),
    )(dest, start, cnt, x2, mods, mods, mods, g_ffn, rout, w_gate, w_up, w_down)


def _prep_weights(w):
    depth = w["w_ada"].shape[0]
    n_ab = w["w_in"].shape[0]
    prep = {"layers": [], "ab": [], "c": []}
    blk = jnp.arange(512) // HEAD_DIM
    prep["mavg"] = jnp.where(blk[:, None] == blk[None, :], 1.0 / HEAD_DIM, 0.0).astype(BF16)
    for l in range(depth):
        wr = jnp.concatenate([w["w_coarse"][l], w["w_fine"][l]], axis=1)
        br = jnp.concatenate([w["b_coarse"][l], w["b_fine"][l]])
        pad = ROUTER_COLS - wr.shape[1]
        prep["layers"].append(dict(
            g_mix=w["g_mix"][l][None, :], g_ffn=w["g_ffn"][l][None, :],
            w_router=jnp.pad(wr, ((0, 0), (0, pad))), b_router=jnp.pad(br, (0, pad))[None, :],
            w_gate=w["w_gate"][l].astype(BF16), w_up=w["w_up"][l].astype(BF16), w_down=w["w_down"][l].astype(BF16)))
    for i in range(n_ab):
        prep["ab"].append(dict(
            w_in=jnp.pad(w["w_in"][i], ((0, 0), (0, IN_COLS_PADDED - IN_COLS))).astype(BF16),
            q_gain=jnp.tile(w["q_gain"][i], N_HEADS)[None, :], k_gain=jnp.tile(w["k_gain"][i], N_KV_HEADS)[None, :],
            disc=s5_discretise(w["lam_re"][i], w["lam_im"][i], w["log_dt"][i], w["ssm_b_re"][i], w["ssm_b_im"][i],
                               w["ssm_c_re"][i], w["ssm_c_im"][i]),
            d_skip=w["ssm_d"][i][None, :], w_glu=w["w_glu"][i].astype(BF16), b_glu=w["b_glu"][i][None, :],
            w_out=w["w_out"][i].astype(BF16)))
    for i in range(w["w_pool"].shape[0]):
        prep["c"].append(dict(w_pool=w["w_pool"][i].astype(BF16), pool_scale=w["pool_scale"][i][None, :]))
    return prep


def _run_prompt(x, mod_all, prep):
    batch, seq, _ = x.shape
    n = batch * seq
    x2 = x.reshape(n, D_MODEL)
    rope = _rope_tables(jnp.arange(seq, dtype=I32))
    tile = 512
    grp = TokenGroup(n, seq, tile)
    grp_moe = TokenGroup(n, seq, 1024)
    zero_state = jnp.zeros((batch, N_STATES), F32)
    ks, vs, kis, sres, sims, pools = [], [], [], [], [], []
    for layer in range(mod_all.shape[0]):
        i = layer // 2
        lw = prep["layers"][layer]
        mods = grp.mods(mod_all[layer])
        if layer % 2 == 0:
            ab = prep["ab"][i]
            u, q, k, v, qi, kiwi, kb, kib, vt = in_proj(x2, mods, grp, lw["g_mix"], ab["w_in"], ab["q_gain"],
                                                        ab["k_gain"], prep["mavg"], rope, seq)
            u3 = jnp.swapaxes(u.reshape(batch, seq, A_WIDTH), 0, 1)
            y3, s_re, s_im = s5_mixer(u3, zero_state, zero_state, ab["disc"], ab["d_skip"], ab["w_glu"],
                                      ab["b_glu"], tt=64)
            y_a = jnp.swapaxes(y3, 0, 1).reshape(n, A_WIDTH)
            y_b = attend_prompt(q, qi, kiwi, kb, kib, vt, batch, seq, ATTN_TQ)
            x2 = out_proj(x2, y_a, y_b, mods, grp, ab["w_out"])
            ks.append(k.reshape(batch, seq, N_KV_HEADS, HEAD_DIM))
            vs.append(v.reshape(batch, seq, N_KV_HEADS, HEAD_DIM))
            kis.append(kiwi[:, :IDX_DIM].reshape(batch, seq, IDX_DIM))
            sres.append(s_re.reshape(batch, SSM_GROUPS, SSM_STATE))
            sims.append(s_im.reshape(batch, SSM_GROUPS, SSM_STATE))
        else:
            c = prep["c"][i]
            x2, st = pool_mixer(x2, x2, mods, grp, lw["g_mix"], c["w_pool"], c["pool_scale"],
                                tiles_per_seq=seq // tile, hist_is_x=True, pos_base=0, state_rows=HIST_ROWS)
            pools.append(st[:, HIST_ROWS - POOL_STATE:, :])
        x2 = hier_moe(x2, grp_moe.mods(mod_all[layer]), grp_moe, lw["g_ffn"], lw["w_router"], lw["b_router"],
                      lw["w_gate"], lw["w_up"], lw["w_down"])
    return (x2.reshape(batch, seq, D_MODEL), jnp.stack(ks), jnp.stack(vs), jnp.stack(kis), jnp.stack(sres),
            jnp.stack(sims), jnp.stack(pools))


def _run_sample(x, mod_all, prep, cache_k, cache_v, cache_kidx, state_re, state_im, state_pool, page_table):
    batch, seq, _ = x.shape
    n = batch * seq
    past_len = page_table.shape[1] * PAGE_SIZE
    x2 = x.reshape(n, D_MODEL)
    rope = _rope_tables(jnp.tile(past_len + jnp.arange(seq, dtype=I32), batch))
    grp = TokenGroup(n, seq, n)
    rows = SAMPLE_Q_ROWS
    grp_pool = TokenGroup(batch * rows, rows, rows)
    pad_q = lambda a: jnp.pad(a.reshape(batch, seq, a.shape[-1]), ((0, 0), (0, rows - seq), (0, 0)))
    page_t = lambda a: jnp.pad(jnp.swapaxes(a.reshape(batch, seq, a.shape[-1]), 1, 2),
                               ((0, 0), (0, 0), (0, PAGE_SIZE - seq)))
    stack_heads = lambda a: jnp.swapaxes(pad_q(a).reshape(batch, rows, N_HEADS, HEAD_DIM), 1, 2).reshape(
        batch, N_HEADS * rows, HEAD_DIM)
    n_layers_ab, n_phys = cache_k.shape[:2]
    cache_k_t = jnp.transpose(cache_k, (0, 1, 3, 4, 2)).reshape(n_layers_ab, n_phys, KV_WIDTH, PAGE_SIZE)
    cache_v_t = jnp.transpose(cache_v, (0, 1, 3, 4, 2)).reshape(n_layers_ab, n_phys, KV_WIDTH, PAGE_SIZE)
    cache_ki_t = jnp.transpose(cache_kidx, (0, 1, 3, 2))
    ks, vs, kis, sres, sims, pools = [], [], [], [], [], []
    for layer in range(mod_all.shape[0]):
        i = layer // 2
        lw = prep["layers"][layer]
        mods = grp.mods(mod_all[layer])
        if layer % 2 == 0:
            ab = prep["ab"][i]
            u, q, k, v, qi, kiwi, _, _, _ = in_proj(x2, mods, grp, lw["g_mix"], ab["w_in"], ab["q_gain"],
                                                    ab["k_gain"], prep["mavg"], rope, n)
            u3 = jnp.swapaxes(u.reshape(batch, seq, A_WIDTH), 0, 1)
            y3, s_re, s_im = s5_mixer(u3, state_re[i].reshape(batch, N_STATES), state_im[i].reshape(batch, N_STATES),
                                      ab["disc"], ab["d_skip"], ab["w_glu"], ab["b_glu"], tt=seq)
            y_a = jnp.swapaxes(y3, 0, 1).reshape(n, A_WIDTH)
            o_st = attend_sample(stack_heads(q), stack_heads(qi), pad_q(kiwi), page_t(k), page_t(v),
                                 page_t(kiwi[:, :IDX_DIM]), cache_k_t, cache_v_t, cache_ki_t, page_table, i, seq)
            y_b = jnp.swapaxes(o_st.reshape(batch, N_HEADS, rows, HEAD_DIM), 1, 2)[:, :seq]
            x2 = out_proj(x2, y_a, y_b.reshape(n, 512).astype(BF16), mods, grp, ab["w_out"])
            ks.append(k.reshape(batch, seq, N_KV_HEADS, HEAD_DIM))
            vs.append(v.reshape(batch, seq, N_KV_HEADS, HEAD_DIM))
            kis.append(kiwi[:, :IDX_DIM].reshape(batch, seq, IDX_DIM))
            sres.append(s_re.reshape(batch, SSM_GROUPS, SSM_STATE))
            sims.append(s_im.reshape(batch, SSM_GROUPS, SSM_STATE))
        else:
            c = prep["c"][i]
            xp = jnp.pad(x2.reshape(batch, seq, D_MODEL), ((0, 0), (0, rows - seq), (0, 0)))
            hist = jnp.pad(state_pool[i], ((0, 0), (HIST_ROWS - POOL_STATE, 0), (0, 0)))
            xo, st = pool_mixer(xp.reshape(batch * rows, D_MODEL), hist, grp_pool.mods(mod_all[layer]), grp_pool,
                                lw["g_mix"], c["w_pool"], c["pool_scale"], tiles_per_seq=1, hist_is_x=False,
                                pos_base=past_len, state_rows=HIST_ROWS + rows)
            x2 = xo.reshape(batch, rows, D_MODEL)[:, :seq].reshape(n, D_MODEL)
            pools.append(st[:, seq + HIST_ROWS - POOL_STATE:seq + HIST_ROWS, :])
        x2 = hier_moe(x2, mods, grp, lw["g_ffn"], lw["w_router"], lw["b_router"], lw["w_gate"], lw["w_up"],
                      lw["w_down"])
    return (x2.reshape(batch, seq, D_MODEL), jnp.stack(ks), jnp.stack(vs), jnp.stack(kis), jnp.stack(sres),
            jnp.stack(sims), jnp.stack(pools))


def kernel(x_prompt, x_sample, c_prompt, c_sample, cache_k, cache_v, cache_kidx, state_ssm_re, state_ssm_im,
           state_pool, page_table, w_ada, b_ada, g_mix, g_ffn, w_in, q_gain, k_gain, lam_re, lam_im, log_dt,
           ssm_b_re, ssm_b_im, ssm_c_re, ssm_c_im, ssm_d, w_glu, b_glu, w_out, w_pool, pool_scale, w_coarse,
           b_coarse, w_fine, b_fine, w_gate, w_up, w_down):
    weights = dict(w_ada=w_ada, g_mix=g_mix, g_ffn=g_ffn, w_in=w_in, q_gain=q_gain, k_gain=k_gain, lam_re=lam_re,
                   lam_im=lam_im, log_dt=log_dt, ssm_b_re=ssm_b_re, ssm_b_im=ssm_b_im, ssm_c_re=ssm_c_re,
                   ssm_c_im=ssm_c_im, ssm_d=ssm_d, w_glu=w_glu, b_glu=b_glu, w_out=w_out, w_pool=w_pool,
                   pool_scale=pool_scale, w_coarse=w_coarse, b_coarse=b_coarse, w_fine=w_fine, b_fine=b_fine,
                   w_gate=w_gate, w_up=w_up, w_down=w_down)
    prep = _prep_weights(weights)
    n_prompt = c_prompt.shape[0]
    mod_all = ada_params_all(jnp.concatenate([c_prompt, c_sample], axis=0), w_ada, b_ada)
    out_p = _run_prompt(x_prompt, mod_all[:, :n_prompt], prep)
    out_s = _run_sample(x_sample, mod_all[:, n_prompt:], prep, cache_k, cache_v, cache_kidx, state_ssm_re,
                        state_ssm_im, state_pool, page_table)
    return (out_p[0], out_s[0]) + out_p[1:] + out_s[1:]
```

```python
import functools
import math

import jax
import jax.numpy as jnp
from jax import lax
from jax.experimental import pallas as pl
from jax.experimental.pallas import tpu as pltpu

F32 = jnp.float32
BF16 = jnp.bfloat16
I32 = jnp.int32

D_MODEL = 1024
EPS = 1e-6
A_WIDTH = 512
SSM_GROUP = 16
SSM_GROUPS = 32
SSM_STATE = 64
N_STATES = SSM_GROUPS * SSM_STATE
HEAD_DIM = 64
N_HEADS = 8
N_KV_HEADS = 2
KV_WIDTH = N_KV_HEADS * HEAD_DIM
IDX_HEADS = 8
IDX_DIM = 64
TOPK_MAX = 256
ROPE_THETA = 500000.0
ROT_HALF = 8
PAGE_SIZE = 128
POOL_WINDOWS = (2, 4, 8, 16)
POOL_GROUP_DIM = 256
POOL_STATE = 15
HIST_ROWS = 16
MOE_GROUPS = 4
EXPERTS_PER_GROUP = 4
N_EXPERTS = 16
D_FF_EXPERT = 256

OFF_Q, OFF_K, OFF_V, OFF_QI, OFF_KI, OFF_WI, IN_COLS = 512, 1024, 1152, 1280, 1792, 1856, 1864
IN_COLS_PADDED = 1920

V7X_LANES = 128
V7X_SUBLANES = 8
V7X_VMEM_BYTES = 64 * 2**20
VMEM_LIMIT_BYTES = (V7X_VMEM_BYTES * 3) // 4
INT_MIN = -(2**31)
NEG_INF = float("-inf")


def _params(*semantics):
    return pltpu.CompilerParams(dimension_semantics=semantics, vmem_limit_bytes=VMEM_LIMIT_BYTES)


def _norm_mod(x, gain, scale, shift):
    var = jnp.mean(x * x, axis=-1, keepdims=True)
    return (x * lax.rsqrt(var + EPS)) * gain * (1.0 + scale) + shift


def _silu(x):
    return x * jax.nn.sigmoid(x)


def _dot(a, b):
    return jnp.dot(a, b, preferred_element_type=F32)


def _dot_t(a, b):
    return lax.dot_general(a, b, (((1,), (1,)), ((), ())), preferred_element_type=F32)


class TokenGroup:
    def __init__(self, n_rows, rows_per_mod, tile):
        assert n_rows % rows_per_mod == 0 and (rows_per_mod % tile == 0 or tile % rows_per_mod == 0)
        self.n_rows, self.tile = n_rows, tile
        self.n_tiles = n_rows // tile
        if rows_per_mod >= tile:
            self.mod_rows = 1
            self.tiles_per_mod = rows_per_mod // tile
        else:
            self.mod_rows = tile
            self.tiles_per_mod = 1

    def mods(self, mod):
        if self.mod_rows == 1:
            return mod[:, None, :]
        rep = self.n_rows // mod.shape[0]
        return jnp.repeat(mod, rep, axis=0).reshape(self.n_tiles, self.tile, mod.shape[1])

    def mod_spec(self, piece):
        tpm = self.tiles_per_mod
        return pl.BlockSpec((None, self.mod_rows, D_MODEL), lambda i: (i // tpm, 0, piece))

    def row_spec(self, width):
        return pl.BlockSpec((self.tile, width), lambda i: (i, 0))


def _const_spec(shape):
    nd = len(shape)
    return pl.BlockSpec(shape, lambda *_: (0,) * nd)


def _ada_kernel(c_ref, w_ref, b_ref, o_ref):
    s = _silu(c_ref[...]).astype(BF16)
    o_ref[...] = _dot(s, w_ref[...].astype(BF16)) + b_ref[...]


def ada_params_all(c_all, w_ada, b_ada):
    n_layers, d, n6 = w_ada.shape
    m = c_all.shape[0]
    tn = 1536
    return pl.pallas_call(
        _ada_kernel,
        grid=(n_layers, n6 // tn),
        in_specs=[pl.BlockSpec((m, d), lambda l, j: (0, 0)),
                  pl.BlockSpec((None, d, tn), lambda l, j: (l, 0, j)),
                  pl.BlockSpec((None, 1, tn), lambda l, j: (l, 0, j))],
        out_specs=pl.BlockSpec((None, m, tn), lambda l, j: (l, 0, j)),
        out_shape=jax.ShapeDtypeStruct((n_layers, m, n6), F32),
        compiler_params=_params("arbitrary", "arbitrary"),
    )(c_all, w_ada, b_ada.reshape(n_layers, 1, n6))


def _rope(x, cos, s1, s2):
    w = x.shape[1]
    reps = w // V7X_LANES
    if reps > 1:
        cos, s1, s2 = (jnp.tile(t, (1, reps)) for t in (cos, s1, s2))
    return x * cos + pltpu.roll(x, w - ROT_HALF, 1) * s1 + pltpu.roll(x, ROT_HALF, 1) * s2


def _head_norm(x, mavg, gain):
    sq = x * x
    hi = sq.astype(BF16)
    lo = (sq - hi.astype(F32)).astype(BF16)
    var = _dot(hi, mavg) + _dot(lo, mavg)
    return (x * lax.rsqrt(var + EPS)) * gain


def _in_proj_kernel(x_ref, sc_ref, sh_ref, g_ref, w_ref, qg_ref, kg_ref, mavg_ref, rope_ref,
                    u_ref, q_ref, k_ref, v_ref, qi_ref, kiwi_ref, kb_ref, kib_ref, vt_ref):
    h = _norm_mod(x_ref[...], g_ref[...], sc_ref[...], sh_ref[...])
    p = _dot(h.astype(BF16), w_ref[...])
    cos, s1, s2 = rope_ref[0], rope_ref[1], rope_ref[2]
    mavg = mavg_ref[...]
    u_ref[...] = p[:, 0:OFF_Q]
    q = _rope(_head_norm(p[:, OFF_Q:OFF_K], mavg, qg_ref[...]), cos, s1, s2)
    q_ref[...] = (q * HEAD_DIM ** -0.5).astype(BF16)
    k = _head_norm(p[:, OFF_K:OFF_V], mavg[0:KV_WIDTH, 0:KV_WIDTH], kg_ref[...])
    k = _rope(k, cos, s1, s2)
    k_ref[...] = k
    kb_ref[...] = k.astype(BF16)
    v = p[:, OFF_V:OFF_QI]
    v_ref[...] = v
    vt_ref[...] = v.T.astype(BF16)
    qi = _rope(p[:, OFF_QI:OFF_KI], cos, s1, s2)
    qi_ref[...] = (qi * IDX_DIM ** -0.5).astype(BF16)
    kiwi = _rope(p[:, OFF_KI:IN_COLS_PADDED], rope_ref[3], rope_ref[4], rope_ref[5])
    kiwi_ref[...] = kiwi
    kib_ref[...] = kiwi.astype(BF16)


def _rope_tables(pos):
    t = pos.shape[0]
    inv = ROPE_THETA ** (-jnp.arange(ROT_HALF, dtype=F32) / ROT_HALF)
    ang = pos.astype(F32)[:, None] * inv[None, :]
    cos, sin = jnp.cos(ang), jnp.sin(ang)
    rest = HEAD_DIM - 2 * ROT_HALF
    z8, zr, onesr = jnp.zeros((t, ROT_HALF), F32), jnp.zeros((t, rest), F32), jnp.ones((t, rest), F32)
    c64 = jnp.concatenate([cos, cos, onesr], axis=1)
    s1_64 = jnp.concatenate([-sin, z8, zr], axis=1)
    s2_64 = jnp.concatenate([z8, sin, zr], axis=1)
    z64 = jnp.zeros((t, HEAD_DIM), F32)
    wscale = jnp.concatenate([jnp.full((t, IDX_HEADS), IDX_HEADS ** -0.5, F32),
                              jnp.ones((t, HEAD_DIM - IDX_HEADS), F32)], axis=1)
    return jnp.stack([jnp.concatenate([c64, c64], axis=1), jnp.concatenate([s1_64, s1_64], axis=1),
                      jnp.concatenate([s2_64, s2_64], axis=1), jnp.concatenate([c64, wscale], axis=1),
                      jnp.concatenate([s1_64, z64], axis=1), jnp.concatenate([s2_64, z64], axis=1)])


def in_proj(x2, mods, grp, g_mix, w_in_p, q_gain, k_gain, mavg, rope, seq_rows):
    n = x2.shape[0]
    tm = grp.tile
    rope_tiles = rope.shape[1] // tm
    tiles_per_seq = seq_rows // tm
    outs = pl.pallas_call(
        _in_proj_kernel,
        grid=(grp.n_tiles,),
        in_specs=[grp.row_spec(D_MODEL), grp.mod_spec(1), grp.mod_spec(0), _const_spec((1, D_MODEL)),
                  _const_spec((D_MODEL, IN_COLS_PADDED)), _const_spec((1, 512)), _const_spec((1, KV_WIDTH)),
                  _const_spec((512, 512)),
                  pl.BlockSpec((6, tm, V7X_LANES), lambda i: (0, i % rope_tiles, 0))],
        out_specs=[grp.row_spec(512), grp.row_spec(512), grp.row_spec(KV_WIDTH), grp.row_spec(KV_WIDTH),
                   grp.row_spec(512), grp.row_spec(V7X_LANES), grp.row_spec(KV_WIDTH), grp.row_spec(V7X_LANES),
                   pl.BlockSpec((None, KV_WIDTH, tm), lambda i: (i // tiles_per_seq, 0, i % tiles_per_seq))],
        out_shape=[jax.ShapeDtypeStruct((n, 512), F32), jax.ShapeDtypeStruct((n, 512), BF16),
                   jax.ShapeDtypeStruct((n, KV_WIDTH), F32), jax.ShapeDtypeStruct((n, KV_WIDTH), F32),
                   jax.ShapeDtypeStruct((n, 512), BF16), jax.ShapeDtypeStruct((n, V7X_LANES), F32),
                   jax.ShapeDtypeStruct((n, KV_WIDTH), BF16), jax.ShapeDtypeStruct((n, V7X_LANES), BF16),
                   jax.ShapeDtypeStruct((n // seq_rows, KV_WIDTH, seq_rows), BF16)],
        compiler_params=_params("arbitrary"),
    )(x2, mods, mods, g_mix, w_in_p, q_gain, k_gain, mavg, rope)
    return outs


S5_BATCH_BLOCK = V7X_SUBLANES
S5_HALF_IN = A_WIDTH // 2
S5_HALF_STATES = N_STATES // 2
S5_SCAN_COLS = 1024


def _gelu_tanh(x):
    cdf = 0.5 * (1.0 + jnp.tanh(math.sqrt(2.0 / math.pi) * (x + 0.044715 * (x * x * x))))
    return x * cdf


def _s5_kernel(u_ref, x0re_ref, x0im_ref, are_ref, aim_ref, bre_ref, bim_ref, cre_ref, cim_ref, d_ref,
               wglu_ref, bglu_ref, y_ref, sre_ref, sim_ref, st_re, st_im, xs_re, xs_im, *, tt):
    bb = S5_BATCH_BLOCK
    tb = pl.program_id(1)

    @pl.when(tb == 0)
    def _():
        st_re[...] = x0re_ref[...]
        st_im[...] = x0im_ref[...]

    u = u_ref[...].reshape(tt * bb, A_WIDTH)
    ub = u.astype(BF16)
    for half in range(2):
        rows = slice(half * S5_HALF_IN, (half + 1) * S5_HALF_IN)
        cols = slice(half * S5_HALF_STATES, (half + 1) * S5_HALF_STATES)
        xs_re[:, cols] = _dot(ub[:, rows], bre_ref[rows, cols])
        xs_im[:, cols] = _dot(ub[:, rows], bim_ref[rows, cols])

    for c in range(N_STATES // S5_SCAN_COLS):
        cs = slice(c * S5_SCAN_COLS, (c + 1) * S5_SCAN_COLS)
        a_re = are_ref[:, cs]
        a_im = aim_ref[:, cs]

        def step(t, carry, cs=cs, a_re=a_re, a_im=a_im):
            sr, si = carry
            r0 = pl.multiple_of(t * bb, bb)
            nr = a_re * sr - a_im * si + xs_re[pl.ds(r0, bb), cs]
            ni = a_re * si + a_im * sr + xs_im[pl.ds(r0, bb), cs]
            xs_re[pl.ds(r0, bb), cs] = nr
            xs_im[pl.ds(r0, bb), cs] = ni
            return nr, ni

        sr, si = lax.fori_loop(0, tt, step, (st_re[:, cs], st_im[:, cs]))
        st_re[:, cs] = sr
        st_im[:, cs] = si

    xr = xs_re[...].astype(BF16)
    xi = xs_im[...].astype(BF16)
    ys = []
    for half in range(2):
        rows = slice(half * S5_HALF_STATES, (half + 1) * S5_HALF_STATES)
        cols = slice(half * S5_HALF_IN, (half + 1) * S5_HALF_IN)
        ys.append(_dot(xr[:, rows], cre_ref[rows, cols]) - _dot(xi[:, rows], cim_ref[rows, cols]))
    y = jnp.concatenate(ys, axis=1) + d_ref[...] * u
    z = _gelu_tanh(y)
    o = z * jax.nn.sigmoid(_dot(z.astype(BF16), wglu_ref[...]) + bglu_ref[...])
    y_ref[...] = o.reshape(tt, bb, A_WIDTH)

    @pl.when(tb == pl.num_programs(1) - 1)
    def _():
        sre_ref[...] = st_re[...]
        sim_ref[...] = st_im[...]


def s5_discretise(lam_re, lam_im, log_dt, b_re, b_im, c_re, c_im):
    dt = jnp.exp(log_dt)[:, None]
    mag = jnp.exp(lam_re * dt)
    abar_re, abar_im = mag * jnp.cos(lam_im * dt), mag * jnp.sin(lam_im * dt)
    den = lam_re * lam_re + lam_im * lam_im
    ir, ii = lam_re / den, -lam_im / den
    cr = (abar_re - 1.0) * ir - abar_im * ii
    ci = (abar_re - 1.0) * ii + abar_im * ir
    bb_re = cr[..., None] * b_re - ci[..., None] * b_im
    bb_im = cr[..., None] * b_im + ci[..., None] * b_re
    eye = jnp.eye(SSM_GROUPS, dtype=F32)
    pack_b = lambda bb: jnp.einsum("gpn,gh->gnhp", bb, eye).reshape(A_WIDTH, N_STATES).astype(BF16)
    pack_c = lambda cc: jnp.einsum("gnp,gh->gphn", cc, eye).reshape(N_STATES, A_WIDTH).astype(BF16)
    bcast = lambda a: jnp.broadcast_to(a.reshape(1, N_STATES), (S5_BATCH_BLOCK, N_STATES))
    return bcast(abar_re), bcast(abar_im), pack_b(bb_re), pack_b(bb_im), pack_c(c_re), pack_c(c_im)


def s5_mixer(u3, x0_re, x0_im, disc, d_skip, w_glu, b_glu, tt):
    t, b, _ = u3.shape
    bb = S5_BATCH_BLOCK
    a_re, a_im, bre, bim, cre, cim = disc
    state_spec = pl.BlockSpec((bb, N_STATES), lambda i, j: (i, 0))
    seq_spec = pl.BlockSpec((tt, bb, A_WIDTH), lambda i, j: (j, i, 0))
    return pl.pallas_call(
        functools.partial(_s5_kernel, tt=tt),
        grid=(b // bb, t // tt),
        in_specs=[seq_spec, state_spec, state_spec, _const_spec((bb, N_STATES)), _const_spec((bb, N_STATES)),
                  _const_spec((A_WIDTH, N_STATES)), _const_spec((A_WIDTH, N_STATES)),
                  _const_spec((N_STATES, A_WIDTH)), _const_spec((N_STATES, A_WIDTH)),
                  _const_spec((1, A_WIDTH)), _const_spec((A_WIDTH, A_WIDTH)), _const_spec((1, A_WIDTH))],
        out_specs=[seq_spec, state_spec, state_spec],
        out_shape=[jax.ShapeDtypeStruct((t, b, A_WIDTH), F32), jax.ShapeDtypeStruct((b, N_STATES), F32),
                   jax.ShapeDtypeStruct((b, N_STATES), F32)],
        scratch_shapes=[pltpu.VMEM((bb, N_STATES), F32), pltpu.VMEM((bb, N_STATES), F32),
                        pltpu.VMEM((tt * bb, N_STATES), F32), pltpu.VMEM((tt * bb, N_STATES), F32)],
        compiler_params=_params("arbitrary", "arbitrary"),
    )(u3, x0_re, x0_im, a_re, a_im, bre, bim, cre, cim, d_skip, w_glu, b_glu)


def _count(mask):
    return jnp.sum(jnp.where(mask, 1.0, 0.0), axis=1, keepdims=True)


def _topk_bias(sc, keep, qpos0, key_ref, bias_ref):
    m, s_keys = sc.shape
    keep = float(keep)
    kpos = lax.broadcasted_iota(I32, (m, s_keys), 1)
    qpos = qpos0 + lax.broadcasted_iota(I32, (m, 1), 0)
    causal = kpos <= qpos
    bits = lax.bitcast_convert_type(sc, I32)
    mag = bits & 0x7FFFFFFF
    key_ref[...] = jnp.where(causal, jnp.where(bits < 0, -mag, mag), INT_MIN)

    def two_bits(it, thr):
        unit = lax.shift_left(jnp.int32(1), 30 - 2 * it)
        key = key_ref[...]
        new = thr
        for cand in (thr + unit, thr + 2 * unit, thr + 3 * unit):
            new = jnp.where(_count(key >= cand) >= keep, cand, new)
        return new

    thr = lax.fori_loop(0, 16, two_bits, jnp.full((m, 1), INT_MIN, I32))
    key = key_ref[...]
    gt = key > thr
    eq = key == thr
    need = keep - _count(gt)
    bias_ref[...] = jnp.where((gt | eq) & causal, 0.0, NEG_INF)
    tie = (_count(eq) > need) & (thr > INT_MIN)

    @pl.when(jnp.max(jnp.where(tie, 1.0, 0.0)) > 0.0)
    def _():
        def idx_step(_, lohi):
            lo, hi = lohi
            mid = (lo + hi) >> 1
            ok = _count((key_ref[...] == thr) & (kpos <= mid)) >= need
            return jnp.where(ok, lo, mid + 1), jnp.where(ok, mid, hi)

        n_steps = max(1, (s_keys - 1).bit_length())
        last, _ = lax.fori_loop(0, n_steps, idx_step,
                                (jnp.zeros((m, 1), I32), jnp.full((m, 1), s_keys - 1, I32)))
        k2 = key_ref[...]
        sel = (k2 > thr) | ((k2 == thr) & (kpos <= last))
        bias_ref[...] = jnp.where(sel & causal, 0.0, NEG_INF)


def _select_attend_t(s_keys, keep, qpos0, q_hm, qi_hm, wi_t, ki_ref, k_ref, vt_ref, key_ref, bias_ref, o_ref):
    tq = o_ref.shape[0]
    ck = ATTN_KEY_CHUNK if s_keys % ATTN_KEY_CHUNK == 0 else tq
    assert s_keys % ck == 0
    chunks = [slice(c * ck, (c + 1) * ck) for c in range(s_keys // ck)]
    group = N_HEADS // N_KV_HEADS
    keep = float(keep)
    qpos = qpos0 + lax.broadcasted_iota(I32, (1, tq), 1)
    lanes = lambda a, j: a[:, j * tq:(j + 1) * tq]
    fold8 = lambda a, op: op(a.reshape(a.shape[0] // V7X_SUBLANES, V7X_SUBLANES, a.shape[1]), axis=0)

    for c, rows in enumerate(chunks):
        ki_c = ki_ref[rows, 0:IDX_DIM]
        sc = jnp.zeros((ck, tq), F32)
        for h in range(IDX_HEADS):
            sc = sc + jnp.maximum(_dot_t(ki_c, qi_hm[h * tq:(h + 1) * tq, :]), 0.0) * wi_t[h:h + 1, :]
        kpos = c * ck + lax.broadcasted_iota(I32, (ck, tq), 0)
        bits = lax.bitcast_convert_type(sc, I32)
        mag = bits & 0x7FFFFFFF
        key_ref[rows, :] = jnp.where(kpos <= qpos, jnp.where(bits < 0, -mag, mag), INT_MIN)

    ones = jnp.ones((V7X_SUBLANES, s_keys), BF16)

    def counts(conds):
        mask = jnp.concatenate([jnp.where(m, 1.0, 0.0).astype(BF16) for m in conds], axis=1)
        n = _dot(ones, mask)[0:1, :]
        return [lanes(n, j) for j in range(len(conds))]

    def two_bits(it, carry):
        thr, cnt = carry
        unit = lax.shift_left(jnp.int32(1), 30 - 2 * it)
        cands = [thr + unit, thr + 2 * unit, thr + 3 * unit]
        key = key_ref[...]
        for cand, n in zip(cands, counts([key >= cand for cand in cands])):
            ok = n >= keep
            thr, cnt = jnp.where(ok, cand, thr), jnp.where(ok, n, cnt)
        return thr, cnt

    start = (jnp.full((1, tq), INT_MIN, I32), jnp.full((1, tq), s_keys, F32))
    thr, cnt = lax.cond(qpos0 + tq > int(keep), lambda: lax.fori_loop(0, 16, two_bits, start), lambda: start)
    bias_ref[...] = jnp.where(key_ref[...] >= jnp.maximum(thr, INT_MIN + 1), 0.0, NEG_INF)
    tie = (cnt > keep) & (thr > INT_MIN)

    @pl.when(jnp.max(jnp.where(tie, 1.0, 0.0)) > 0.0)
    def _():
        kpos = lax.broadcasted_iota(I32, (s_keys, tq), 0)
        need = keep - counts([key_ref[...] > thr])[0]

        def idx_step(_, lohi):
            lo, hi = lohi
            mid = (lo + hi) >> 1
            ok = counts([(key_ref[...] == thr) & (kpos <= mid)])[0] >= need
            return jnp.where(ok, lo, mid + 1), jnp.where(ok, mid, hi)

        n_steps = max(1, (s_keys - 1).bit_length())
        last, _ = lax.fori_loop(0, n_steps, idx_step,
                                (jnp.zeros((1, tq), I32), jnp.full((1, tq), s_keys - 1, I32)))
        k2 = key_ref[...]
        sel = (k2 > thr) | ((k2 == thr) & (kpos <= last))
        bias_ref[...] = jnp.where(sel & (k2 > INT_MIN), 0.0, NEG_INF)

    outs = []
    for g in range(N_KV_HEADS):
        q4 = q_hm[g * group * tq:(g + 1) * group * tq, :]
        feats = slice(g * HEAD_DIM, (g + 1) * HEAD_DIM)
        score = lambda rows: _dot_t(k_ref[rows, feats], q4) + jnp.tile(bias_ref[rows, :], (1, group))
        m8 = None
        for rows in chunks:
            mc = fold8(score(rows), jnp.max)
            m8 = mc if m8 is None else jnp.maximum(m8, mc)
        m = jnp.max(m8, axis=0, keepdims=True)
        l8 = jnp.zeros((V7X_SUBLANES, group * tq), F32)
        ot = jnp.zeros((HEAD_DIM, group * tq), F32)
        for rows in chunks:
            p = jnp.exp(score(rows) - m)
            l8 = l8 + fold8(p, jnp.sum)
            ot = ot + _dot(vt_ref[feats, rows], p.astype(BF16))
        ot = ot / jnp.sum(l8, axis=0, keepdims=True)
        outs.extend(lanes(ot, h) for h in range(group))
    o_ref[...] = jnp.concatenate(outs, axis=0).T.astype(o_ref.dtype)


def _attn_prompt_kernel(q_ref, qi_ref, wit_ref, k_ref, vt_ref, ki_ref, o_ref, key_ref, bias_ref, *, tq, buckets,
                        keep):
    i = pl.program_id(1)
    need_keys = (i + 1) * tq
    lo = 0
    for s_keys in buckets:
        @pl.when((need_keys > lo) & (need_keys <= s_keys))
        def _(s_keys=s_keys):
            _select_attend_t(s_keys, keep, i * tq, q_ref[...], qi_ref[...], wit_ref[...], ki_ref, k_ref, vt_ref,
                             key_ref.at[0:s_keys, :], bias_ref.at[0:s_keys, :], o_ref)
        lo = s_keys


def attend_prompt(q, qi, kiwi, kb, kib, vt, batch, seq, tq, n_buckets=8):
    n = q.shape[0]
    nq = seq // tq
    step = max(tq, seq // n_buckets)
    buckets = tuple(range(step, seq + 1, step))
    head_major = lambda a: jnp.swapaxes(a.reshape(n // tq, tq, N_HEADS, HEAD_DIM), 1, 2).reshape(
        n // tq, N_HEADS * tq, HEAD_DIM)
    wi_t = jnp.swapaxes(kiwi[:, IDX_DIM:IDX_DIM + IDX_HEADS].reshape(n // tq, tq, IDX_HEADS), 1, 2)
    hm_spec = pl.BlockSpec((None, N_HEADS * tq, HEAD_DIM), lambda b, i: (b * nq + i, 0, 0))
    kv_spec = pl.BlockSpec((seq, KV_WIDTH), lambda b, i: (b, 0))
    return pl.pallas_call(
        functools.partial(_attn_prompt_kernel, tq=tq, buckets=buckets, keep=min(TOPK_MAX, seq // 4)),
        grid=(batch, nq),
        in_specs=[hm_spec, hm_spec, pl.BlockSpec((None, IDX_HEADS, tq), lambda b, i: (b * nq + i, 0, 0)),
                  kv_spec, pl.BlockSpec((None, KV_WIDTH, seq), lambda b, i: (b, 0, 0)), kv_spec],
        out_specs=pl.BlockSpec((tq, 512), lambda b, i: (b * nq + i, 0)),
        out_shape=jax.ShapeDtypeStruct((n, 512), BF16),
        scratch_shapes=[pltpu.VMEM((seq, tq), I32), pltpu.VMEM((seq, tq), F32)],
        compiler_params=_params("arbitrary", "arbitrary"),
    )(head_major(q), head_major(qi), wi_t, kb, vt, kib)


ATTN_TQ = 128
ATTN_KEY_CHUNK = 256
SAMPLE_Q_ROWS = V7X_SUBLANES


def _attn_sample_kernel(pt_ref, q_ref, qi_ref, wi_ref, knew_ref, vnew_ref, kinew_ref, ck_hbm, cv_hbm, cki_hbm,
                        o_ref, k_all, v_all, ki_all, sems, key_ref, bias_ref, *, layer, n_pages, keep):
    b = pl.program_id(0)
    slot = b % 2
    past = n_pages * PAGE_SIZE

    def page_copies(seq, sl):
        copies = []
        for p in range(n_pages):
            phys = pt_ref[seq, p]
            window = pl.ds(p * PAGE_SIZE, PAGE_SIZE)
            copies.append(pltpu.make_async_copy(ck_hbm.at[layer, phys], k_all.at[sl, :, window], sems.at[sl, 0]))
            copies.append(pltpu.make_async_copy(cv_hbm.at[layer, phys], v_all.at[sl, :, window], sems.at[sl, 1]))
            copies.append(pltpu.make_async_copy(cki_hbm.at[layer, phys], ki_all.at[sl, :, window], sems.at[sl, 2]))
        return copies

    @pl.when(b == 0)
    def _():
        for c in page_copies(0, 0):
            c.start()

    @pl.when(b + 1 < pl.num_programs(0))
    def _():
        for c in page_copies(b + 1, 1 - slot):
            c.start()

    k_all[slot, :, past:past + PAGE_SIZE] = knew_ref[...]
    v_all[slot, :, past:past + PAGE_SIZE] = vnew_ref[...]
    ki_all[slot, :, past:past + PAGE_SIZE] = kinew_ref[...]
    for c in page_copies(b, slot):
        c.wait()
    rows = SAMPLE_Q_ROWS
    d = _dot(qi_ref[...], ki_all[slot].astype(BF16))
    wi = wi_ref[:, IDX_DIM:IDX_DIM + IDX_HEADS]
    sc = jnp.zeros((rows, past + PAGE_SIZE), F32)
    for h in range(IDX_HEADS):
        sc = sc + jnp.maximum(d[h * rows:(h + 1) * rows, :], 0.0) * wi[:, h:h + 1]
    _topk_bias(sc, keep, past, key_ref, bias_ref)
    group = N_HEADS // N_KV_HEADS
    bias = jnp.tile(bias_ref[...], (group, 1))
    q = q_ref[...]
    for g in range(N_KV_HEADS):
        feats = slice(g * HEAD_DIM, (g + 1) * HEAD_DIM)
        qrows = slice(g * group * rows, (g + 1) * group * rows)
        s = _dot(q[qrows, :], k_all[slot, feats, :].astype(BF16)) + bias
        p = jnp.exp(s - jnp.max(s, axis=1, keepdims=True))
        l = jnp.sum(p, axis=1, keepdims=True)
        o_ref[qrows, :] = _dot_t(p.astype(BF16), v_all[slot, feats, :].astype(BF16)) / l


def attend_sample(q_st, qi_st, kiwi3, k_new_t, v_new_t, ki_new_t, cache_k_t, cache_v_t, cache_ki_t, page_table,
                  layer, seq):
    b = q_st.shape[0]
    n_pages = page_table.shape[1]
    s_keys = (n_pages + 1) * PAGE_SIZE
    rows = SAMPLE_Q_ROWS
    st_spec = pl.BlockSpec((None, N_HEADS * rows, HEAD_DIM), lambda i, pt: (i, 0, 0))
    q_spec = lambda w: pl.BlockSpec((None, rows, w), lambda i, pt: (i, 0, 0))
    new_spec = lambda f: pl.BlockSpec((None, f, PAGE_SIZE), lambda i, pt: (i, 0, 0))
    hbm = pl.BlockSpec(memory_space=pl.ANY)
    grid_spec = pltpu.PrefetchScalarGridSpec(
        num_scalar_prefetch=1,
        grid=(b,),
        in_specs=[st_spec, st_spec, q_spec(V7X_LANES), new_spec(KV_WIDTH), new_spec(KV_WIDTH),
                  new_spec(IDX_DIM), hbm, hbm, hbm],
        out_specs=st_spec,
        scratch_shapes=[pltpu.VMEM((2, KV_WIDTH, s_keys), F32), pltpu.VMEM((2, KV_WIDTH, s_keys), F32),
                        pltpu.VMEM((2, IDX_DIM, s_keys), F32), pltpu.SemaphoreType.DMA((2, 3)),
                        pltpu.VMEM((rows, s_keys), I32), pltpu.VMEM((rows, s_keys), F32)],
    )
    keep = min(TOPK_MAX, (n_pages * PAGE_SIZE + seq) // 4)
    return pl.pallas_call(
        functools.partial(_attn_sample_kernel, layer=layer, n_pages=n_pages, keep=keep),
        grid_spec=grid_spec,
        out_shape=jax.ShapeDtypeStruct((b, N_HEADS * rows, HEAD_DIM), F32),
        compiler_params=_params("arbitrary"),
    )(page_table, q_st, qi_st, kiwi3, k_new_t, v_new_t, ki_new_t, cache_k_t, cache_v_t, cache_ki_t)


def _out_proj_kernel(x_ref, ya_ref, yb_ref, gate_ref, w_ref, sc2_ref, sh2_ref, g2_ref, wr_ref, br_ref,
                     o_ref, rout_ref):
    y = _dot(ya_ref[...].astype(BF16), w_ref[0:A_WIDTH, :]) + _dot(yb_ref[...], w_ref[A_WIDTH:2 * A_WIDTH, :])
    x_new = x_ref[...] + gate_ref[...] * y
    o_ref[...] = x_new
    rout_ref[...] = _route_rows(x_new, g2_ref[...], sc2_ref[...], sh2_ref[...], wr_ref[...], br_ref[...])


def out_proj(x2, y_a, y_b, mods, grp, w_out, g_ffn, w_router, b_router):
    return pl.pallas_call(
        _out_proj_kernel,
        grid=(grp.n_tiles,),
        in_specs=[grp.row_spec(D_MODEL), grp.row_spec(A_WIDTH), grp.row_spec(512), grp.mod_spec(2),
                  _const_spec((2 * A_WIDTH, D_MODEL)), grp.mod_spec(4), grp.mod_spec(3), _const_spec((1, D_MODEL)),
                  _const_spec((D_MODEL, ROUTER_COLS)), _const_spec((1, ROUTER_COLS))],
        out_specs=[grp.row_spec(D_MODEL), pl.BlockSpec((ROUTE_ROWS, grp.tile), lambda i: (0, i))],
        out_shape=[jax.ShapeDtypeStruct(x2.shape, F32), jax.ShapeDtypeStruct((ROUTE_ROWS, x2.shape[0]), F32)],
        compiler_params=_params("arbitrary"),
    )(x2, y_a, y_b, mods, w_out, mods, mods, g_ffn, w_router, b_router)


def _pool_kernel(x_ref, hist_ref, sc_ref, sh_ref, gate_ref, g_ref, w_ref, ps_ref, *rest, tiles_per_seq, hist_is_x,
                 pos_base, state_rows, emit_route):
    if emit_route:
        sc2_ref, sh2_ref, g2_ref, wr_ref, br_ref, o_ref, st_ref, rout_ref = rest
    else:
        o_ref, st_ref = rest
    i = pl.program_id(0)
    tm = x_ref.shape[0]
    x = x_ref[...]
    h = _norm_mod(x, g_ref[...], sc_ref[...], sh_ref[...])
    if hist_is_x:
        hist = _norm_mod(hist_ref[...], g_ref[...], sc_ref[...], sh_ref[...])
        hist = jnp.where(i % tiles_per_seq == 0, 0.0, hist)
    else:
        hist = hist_ref[...]
    ext = jnp.concatenate([hist, h], axis=0)
    pos = pos_base + (i % tiles_per_seq) * tm + lax.broadcasted_iota(I32, (tm, 1), 0)
    ys = []
    for g, win in enumerate(POOL_WINDOWS):
        cols = slice(g * POOL_GROUP_DIM, (g + 1) * POOL_GROUP_DIM)
        s = ext[:, cols]
        shift = 1
        while shift < win:
            s = s + pltpu.roll(s, shift, 0)
            shift *= 2
        cnt = jnp.minimum(pos + 1, win).astype(F32)
        pooled = s[HIST_ROWS:, :] / cnt - h[:, cols]
        ys.append(_dot(pooled.astype(BF16), w_ref[g]))
    y = jnp.concatenate(ys, axis=1) * ps_ref[...]
    x_new = x + gate_ref[...] * y
    o_ref[...] = x_new
    st_ref[...] = ext[HIST_ROWS + tm - state_rows:, :]
    if emit_route:
        rout_ref[...] = _route_rows(x_new, g2_ref[...], sc2_ref[...], sh2_ref[...], wr_ref[...], br_ref[...])


def pool_mixer(x2, hist, mods, grp, g_mix, w_pool, pool_scale, route_weights=None, *, tiles_per_seq, hist_is_x,
               pos_base, state_rows):
    tm = grp.tile
    n_seq = grp.n_tiles // tiles_per_seq
    if hist_is_x:
        per_tile = tm // HIST_ROWS
        hist_spec = pl.BlockSpec((HIST_ROWS, D_MODEL), lambda i: (jnp.maximum(i * per_tile - 1, 0), 0))
    else:
        hist_spec = pl.BlockSpec((None, HIST_ROWS, D_MODEL), lambda i: (i, 0, 0))
    emit_route = route_weights is not None
    in_specs = [grp.row_spec(D_MODEL), hist_spec, grp.mod_spec(1), grp.mod_spec(0), grp.mod_spec(2),
                _const_spec((1, D_MODEL)), _const_spec((len(POOL_WINDOWS), POOL_GROUP_DIM, POOL_GROUP_DIM)),
                _const_spec((1, D_MODEL))]
    out_specs = [grp.row_spec(D_MODEL),
                 pl.BlockSpec((None, state_rows, D_MODEL), lambda i: (i // tiles_per_seq, 0, 0))]
    out_shape = [jax.ShapeDtypeStruct(x2.shape, F32), jax.ShapeDtypeStruct((n_seq, state_rows, D_MODEL), F32)]
    args = [x2, hist, mods, mods, mods, g_mix, w_pool, pool_scale]
    if emit_route:
        in_specs += [grp.mod_spec(4), grp.mod_spec(3), _const_spec((1, D_MODEL)),
                     _const_spec((D_MODEL, ROUTER_COLS)), _const_spec((1, ROUTER_COLS))]
        out_specs.append(pl.BlockSpec((ROUTE_ROWS, tm), lambda i: (0, i)))
        out_shape.append(jax.ShapeDtypeStruct((ROUTE_ROWS, x2.shape[0]), F32))
        args += [mods, mods, *route_weights]
    return pl.pallas_call(
        functools.partial(_pool_kernel, tiles_per_seq=tiles_per_seq, hist_is_x=hist_is_x, pos_base=pos_base,
                          state_rows=state_rows, emit_route=emit_route),
        grid=(grp.n_tiles,),
        in_specs=in_specs,
        out_specs=out_specs,
        out_shape=out_shape,
        compiler_params=_params("arbitrary"),
    )(*args)


ROUTER_COLS = V7X_LANES


MOE_GROUP_ROW = EXPERTS_PER_GROUP
ROUTE_ROWS = V7X_SUBLANES
MOE_CHUNK = 256
MOE_SEG_ALIGN = V7X_SUBLANES


def _router_gates(logits_t):
    col = lambda j: logits_t[j:j + 1, :]
    lc = [col(j) for j in range(MOE_GROUPS)]
    mc = functools.reduce(jnp.maximum, lc)
    pg = 1.0 / functools.reduce(lambda a, b: a + b, [jnp.exp(l - mc) for l in lc])
    grp = jnp.where(lc[0] == mc, 0, jnp.where(lc[1] == mc, 1, jnp.where(lc[2] == mc, 2, 3)))
    fl = []
    for j in range(EXPERTS_PER_GROUP):
        cands = [col(MOE_GROUPS + g * EXPERTS_PER_GROUP + j) for g in range(MOE_GROUPS)]
        fl.append(jnp.where(grp == 0, cands[0], jnp.where(grp == 1, cands[1],
                                                          jnp.where(grp == 2, cands[2], cands[3]))))
    first = lambda vals, mx: jnp.where(vals[0] == mx, 0, jnp.where(vals[1] == mx, 1, jnp.where(vals[2] == mx, 2, 3)))
    m1 = functools.reduce(jnp.maximum, fl)
    i1 = first(fl, m1)
    rest = [jnp.where(i1 == j, NEG_INF, fl[j]) for j in range(EXPERTS_PER_GROUP)]
    m2 = functools.reduce(jnp.maximum, rest)
    i2 = first(rest, m2)
    e2 = jnp.exp(m2 - m1)
    w1 = pg / (1.0 + e2)
    w2 = pg * e2 / (1.0 + e2)
    return grp, [jnp.where(i1 == j, w1, jnp.where(i2 == j, w2, 0.0)) for j in range(EXPERTS_PER_GROUP)]


def _route_rows(x, g_ffn, scale, shift, w_router, b_router):
    h = _norm_mod(x, g_ffn, scale, shift)
    logits = jnp.dot(h, w_router, preferred_element_type=F32, precision=lax.Precision.HIGHEST) + b_router
    grp, gates = _router_gates(logits.T)
    sub = lax.broadcasted_iota(I32, (ROUTE_ROWS, x.shape[0]), 0)
    out = jnp.where(sub == MOE_GROUP_ROW, grp.astype(F32), 0.0)
    for j, gj in enumerate(gates):
        out = jnp.where(sub == j, gj, out)
    return out


def _route_kernel(x_ref, sc_ref, sh_ref, g_ref, wr_ref, br_ref, o_ref):
    o_ref[...] = _route_rows(x_ref[...], g_ref[...], sc_ref[...], sh_ref[...], wr_ref[...], br_ref[...])


def moe_route(x2, mods, grp, g_ffn, w_router, b_router):
    return pl.pallas_call(
        _route_kernel,
        grid=(grp.n_tiles,),
        in_specs=[grp.row_spec(D_MODEL), grp.mod_spec(4), grp.mod_spec(3), _const_spec((1, D_MODEL)),
                  _const_spec((D_MODEL, ROUTER_COLS)), _const_spec((1, ROUTER_COLS))],
        out_specs=pl.BlockSpec((ROUTE_ROWS, grp.tile), lambda i: (0, i)),
        out_shape=jax.ShapeDtypeStruct((ROUTE_ROWS, x2.shape[0]), F32),
        compiler_params=_params("arbitrary"),
    )(x2, mods, mods, g_ffn, w_router, b_router)


def _route_plan(rout_t, tm):
    n_tiles = rout_t.shape[1] // tm
    grp = rout_t[MOE_GROUP_ROW].astype(I32).reshape(n_tiles, tm)
    onehot = (grp[:, :, None] == jnp.arange(MOE_GROUPS, dtype=I32)[None, None, :]).astype(I32)
    cnt = jnp.sum(onehot, axis=1)
    padded = (cnt + MOE_SEG_ALIGN - 1) // MOE_SEG_ALIGN * MOE_SEG_ALIGN
    start = jnp.cumsum(padded, axis=1) - padded
    before = jnp.cumsum(onehot, axis=1) - onehot
    dest = jnp.sum(onehot * (before + start[:, None, :]), axis=2)
    return dest.reshape(-1), start.reshape(-1), cnt.reshape(-1)


def _moe_kernel(dest_ref, start_ref, cnt_ref, x_ref, sc_ref, sh_ref, gate_ref, g_ref, rout_ref, wg_ref, wu_ref,
                wd_ref, o_ref, hs_ref, gs_ref, ys_ref, *, chunk):
    i = pl.program_id(0)
    g = pl.program_id(1)
    tm = x_ref.shape[0]
    base = i * tm

    @pl.when((i == 0) & (g == 0))
    def _():
        hs_ref[...] = jnp.zeros_like(hs_ref)
        gs_ref[...] = jnp.zeros_like(gs_ref)

    @pl.when(g == 0)
    def _():
        o_ref[...] = _norm_mod(x_ref[...], g_ref[...], sc_ref[...], sh_ref[...])

        def move(r, carry):
            d = dest_ref[base + r]
            hs_ref[pl.ds(d, 1), :] = o_ref[pl.ds(r, 1), :]
            gs_ref[pl.ds(d, 1), :] = rout_ref[pl.ds(r, 1), :]
            return carry

        lax.fori_loop(0, tm, move, 0, unroll=8)

    start = start_ref[i * MOE_GROUPS + g]
    cnt = cnt_ref[i * MOE_GROUPS + g]

    def expert_rows(r0, size, n_valid):
        r0 = pl.multiple_of(r0, MOE_SEG_ALIGN)
        xs = hs_ref[pl.ds(r0, size), :].astype(BF16)
        gates = gs_ref[pl.ds(r0, size), :]
        valid = None if n_valid is None else lax.broadcasted_iota(I32, (size, 1), 0) < n_valid
        y = jnp.zeros((size, D_MODEL), F32)
        for e in range(EXPERTS_PER_GROUP):
            hid = _silu(_dot(xs, wg_ref[e])) * _dot(xs, wu_ref[e]) * gates[:, e:e + 1]
            if valid is not None:
                hid = jnp.where(valid, hid, 0.0)
            y = y + _dot(hid.astype(BF16), wd_ref[e])
        ys_ref[pl.ds(r0, size), :] = y

    def full_chunk(c, carry):
        expert_rows(start + c * chunk, chunk, None)
        return carry

    n_full = cnt // chunk
    lax.fori_loop(0, n_full, full_chunk, 0)
    rem = cnt - n_full * chunk
    tail = start + n_full * chunk
    half = chunk // 2

    @pl.when(rem > half)
    def _():
        expert_rows(tail, chunk, rem)

    @pl.when((rem > 0) & (rem <= half))
    def _():
        expert_rows(tail, half, rem)

    @pl.when(g == pl.num_programs(1) - 1)
    def _():
        def move_back(r, carry):
            d = dest_ref[base + r]
            o_ref[pl.ds(r, 1), :] = ys_ref[pl.ds(d, 1), :]
            return carry

        lax.fori_loop(0, tm, move_back, 0, unroll=8)
        o_ref[...] = x_ref[...] + gate_ref[...] * o_ref[...]


def hier_moe(x2, mods, grp, g_ffn, w_router, b_router, w_gate, w_up, w_down, rout_t=None):
    tm = grp.tile
    chunk = min(MOE_CHUNK, tm)
    if rout_t is None:
        rout_t = moe_route(x2, mods, grp, g_ffn, w_router, b_router)
    dest, start, cnt = _route_plan(rout_t, tm)
    rout = rout_t.T
    buf_rows = tm + chunk + MOE_GROUPS * MOE_SEG_ALIGN
    row = lambda w: pl.BlockSpec((tm, w), lambda i, g, *_: (i, 0))
    tpm = grp.tiles_per_mod
    mod_spec = lambda piece: pl.BlockSpec((None, grp.mod_rows, D_MODEL), lambda i, g, *_: (i // tpm, 0, piece))
    epg = EXPERTS_PER_GROUP
    grid_spec = pltpu.PrefetchScalarGridSpec(
        num_scalar_prefetch=3,
        grid=(grp.n_tiles, MOE_GROUPS),
        in_specs=[row(D_MODEL), mod_spec(4), mod_spec(3), mod_spec(5),
                  pl.BlockSpec((1, D_MODEL), lambda i, g, *_: (0, 0)), row(ROUTE_ROWS),
                  pl.BlockSpec((epg, D_MODEL, D_FF_EXPERT), lambda i, g, *_: (g, 0, 0)),
                  pl.BlockSpec((epg, D_MODEL, D_FF_EXPERT), lambda i, g, *_: (g, 0, 0)),
                  pl.BlockSpec((epg, D_FF_EXPERT, D_MODEL), lambda i, g, *_: (g, 0, 0))],
        out_specs=row(D_MODEL),
        scratch_shapes=[pltpu.VMEM((buf_rows, D_MODEL), F32), pltpu.VMEM((buf_rows, ROUTE_ROWS), F32),
                        pltpu.VMEM((buf_rows, D_MODEL), F32)],
    )
    return pl.pallas_call(
        functools.partial(_moe_kernel, chunk=chunk),
        grid_spec=grid_spec,
        out_shape=jax.ShapeDtypeStruct(x2.shape, F32),
        compiler_params=_params("arbitrary", "arbitrary"),
    )(dest, start, cnt, x2, mods, mods, mods, g_ffn, rout, w_gate, w_up, w_down)


def _prep_weights(w):
    depth = w["w_ada"].shape[0]
    n_ab = w["w_in"].shape[0]
    prep = {"layers": [], "ab": [], "c": []}
    blk = jnp.arange(512) // HEAD_DIM
    prep["mavg"] = jnp.where(blk[:, None] == blk[None, :], 1.0 / HEAD_DIM, 0.0).astype(BF16)
    for l in range(depth):
        wr = jnp.concatenate([w["w_coarse"][l], w["w_fine"][l]], axis=1)
        br = jnp.concatenate([w["b_coarse"][l], w["b_fine"][l]])
        pad = ROUTER_COLS - wr.shape[1]
        prep["layers"].append(dict(
            g_mix=w["g_mix"][l][None, :], g_ffn=w["g_ffn"][l][None, :],
            w_router=jnp.pad(wr, ((0, 0), (0, pad))), b_router=jnp.pad(br, (0, pad))[None, :],
            w_gate=w["w_gate"][l].astype(BF16), w_up=w["w_up"][l].astype(BF16), w_down=w["w_down"][l].astype(BF16)))
    for i in range(n_ab):
        prep["ab"].append(dict(
            w_in=jnp.pad(w["w_in"][i], ((0, 0), (0, IN_COLS_PADDED - IN_COLS))).astype(BF16),
            q_gain=jnp.tile(w["q_gain"][i], N_HEADS)[None, :], k_gain=jnp.tile(w["k_gain"][i], N_KV_HEADS)[None, :],
            disc=s5_discretise(w["lam_re"][i], w["lam_im"][i], w["log_dt"][i], w["ssm_b_re"][i], w["ssm_b_im"][i],
                               w["ssm_c_re"][i], w["ssm_c_im"][i]),
            d_skip=w["ssm_d"][i][None, :], w_glu=w["w_glu"][i].astype(BF16), b_glu=w["b_glu"][i][None, :],
            w_out=w["w_out"][i].astype(BF16)))
    for i in range(w["w_pool"].shape[0]):
        prep["c"].append(dict(w_pool=w["w_pool"][i].astype(BF16), pool_scale=w["pool_scale"][i][None, :]))
    return prep


def _run_prompt(x, mod_all, prep):
    batch, seq, _ = x.shape
    n = batch * seq
    x2 = x.reshape(n, D_MODEL)
    rope = _rope_tables(jnp.arange(seq, dtype=I32))
    tile = 512
    grp = TokenGroup(n, seq, tile)
    grp_moe = TokenGroup(n, seq, 1024)
    zero_state = jnp.zeros((batch, N_STATES), F32)
    ks, vs, kis, sres, sims, pools = [], [], [], [], [], []
    for layer in range(mod_all.shape[0]):
        i = layer // 2
        lw = prep["layers"][layer]
        route_w = (lw["g_ffn"], lw["w_router"], lw["b_router"])
        mods = grp.mods(mod_all[layer])
        if layer % 2 == 0:
            ab = prep["ab"][i]
            u, q, k, v, qi, kiwi, kb, kib, vt = in_proj(x2, mods, grp, lw["g_mix"], ab["w_in"], ab["q_gain"],
                                                        ab["k_gain"], prep["mavg"], rope, seq)
            u3 = jnp.swapaxes(u.reshape(batch, seq, A_WIDTH), 0, 1)
            y3, s_re, s_im = s5_mixer(u3, zero_state, zero_state, ab["disc"], ab["d_skip"], ab["w_glu"],
                                      ab["b_glu"], tt=64)
            y_a = jnp.swapaxes(y3, 0, 1).reshape(n, A_WIDTH)
            y_b = attend_prompt(q, qi, kiwi, kb, kib, vt, batch, seq, ATTN_TQ)
            x2, rout_t = out_proj(x2, y_a, y_b, mods, grp, ab["w_out"], *route_w)
            ks.append(k.reshape(batch, seq, N_KV_HEADS, HEAD_DIM))
            vs.append(v.reshape(batch, seq, N_KV_HEADS, HEAD_DIM))
            kis.append(kiwi[:, :IDX_DIM].reshape(batch, seq, IDX_DIM))
            sres.append(s_re.reshape(batch, SSM_GROUPS, SSM_STATE))
            sims.append(s_im.reshape(batch, SSM_GROUPS, SSM_STATE))
        else:
            c = prep["c"][i]
            x2, st, rout_t = pool_mixer(x2, x2, mods, grp, lw["g_mix"], c["w_pool"], c["pool_scale"], route_w,
                                        tiles_per_seq=seq // tile, hist_is_x=True, pos_base=0,
                                        state_rows=HIST_ROWS)
            pools.append(st[:, HIST_ROWS - POOL_STATE:, :])
        x2 = hier_moe(x2, grp_moe.mods(mod_all[layer]), grp_moe, *route_w, lw["w_gate"], lw["w_up"], lw["w_down"],
                      rout_t=rout_t)
    return (x2.reshape(batch, seq, D_MODEL), jnp.stack(ks), jnp.stack(vs), jnp.stack(kis), jnp.stack(sres),
            jnp.stack(sims), jnp.stack(pools))


def _run_sample(x, mod_all, prep, cache_k, cache_v, cache_kidx, state_re, state_im, state_pool, page_table):
    batch, seq, _ = x.shape
    n = batch * seq
    past_len = page_table.shape[1] * PAGE_SIZE
    x2 = x.reshape(n, D_MODEL)
    rope = _rope_tables(jnp.tile(past_len + jnp.arange(seq, dtype=I32), batch))
    grp = TokenGroup(n, seq, n)
    rows = SAMPLE_Q_ROWS
    grp_pool = TokenGroup(batch * rows, rows, rows)
    pad_q = lambda a: jnp.pad(a.reshape(batch, seq, a.shape[-1]), ((0, 0), (0, rows - seq), (0, 0)))
    page_t = lambda a: jnp.pad(jnp.swapaxes(a.reshape(batch, seq, a.shape[-1]), 1, 2),
                               ((0, 0), (0, 0), (0, PAGE_SIZE - seq)))
    stack_heads = lambda a: jnp.swapaxes(pad_q(a).reshape(batch, rows, N_HEADS, HEAD_DIM), 1, 2).reshape(
        batch, N_HEADS * rows, HEAD_DIM)
    n_layers_ab, n_phys = cache_k.shape[:2]
    cache_k_t = jnp.transpose(cache_k, (0, 1, 3, 4, 2)).reshape(n_layers_ab, n_phys, KV_WIDTH, PAGE_SIZE)
    cache_v_t = jnp.transpose(cache_v, (0, 1, 3, 4, 2)).reshape(n_layers_ab, n_phys, KV_WIDTH, PAGE_SIZE)
    cache_ki_t = jnp.transpose(cache_kidx, (0, 1, 3, 2))
    ks, vs, kis, sres, sims, pools = [], [], [], [], [], []
    for layer in range(mod_all.shape[0]):
        i = layer // 2
        lw = prep["layers"][layer]
        route_w = (lw["g_ffn"], lw["w_router"], lw["b_router"])
        mods = grp.mods(mod_all[layer])
        if layer % 2 == 0:
            ab = prep["ab"][i]
            u, q, k, v, qi, kiwi, _, _, _ = in_proj(x2, mods, grp, lw["g_mix"], ab["w_in"], ab["q_gain"],
                                                    ab["k_gain"], prep["mavg"], rope, n)
            u3 = jnp.swapaxes(u.reshape(batch, seq, A_WIDTH), 0, 1)
            y3, s_re, s_im = s5_mixer(u3, state_re[i].reshape(batch, N_STATES), state_im[i].reshape(batch, N_STATES),
                                      ab["disc"], ab["d_skip"], ab["w_glu"], ab["b_glu"], tt=seq)
            y_a = jnp.swapaxes(y3, 0, 1).reshape(n, A_WIDTH)
            o_st = attend_sample(stack_heads(q), stack_heads(qi), pad_q(kiwi), page_t(k), page_t(v),
                                 page_t(kiwi[:, :IDX_DIM]), cache_k_t, cache_v_t, cache_ki_t, page_table, i, seq)
            y_b = jnp.swapaxes(o_st.reshape(batch, N_HEADS, rows, HEAD_DIM), 1, 2)[:, :seq]
            x2, rout_t = out_proj(x2, y_a, y_b.reshape(n, 512).astype(BF16), mods, grp, ab["w_out"], *route_w)
            ks.append(k.reshape(batch, seq, N_KV_HEADS, HEAD_DIM))
            vs.append(v.reshape(batch, seq, N_KV_HEADS, HEAD_DIM))
            kis.append(kiwi[:, :IDX_DIM].reshape(batch, seq, IDX_DIM))
            sres.append(s_re.reshape(batch, SSM_GROUPS, SSM_STATE))
            sims.append(s_im.reshape(batch, SSM_GROUPS, SSM_STATE))
        else:
            c = prep["c"][i]
            xp = jnp.pad(x2.reshape(batch, seq, D_MODEL), ((0, 0), (0, rows - seq), (0, 0)))
            hist = jnp.pad(state_pool[i], ((0, 0), (HIST_ROWS - POOL_STATE, 0), (0, 0)))
            xo, st = pool_mixer(xp.reshape(batch * rows, D_MODEL), hist, grp_pool.mods(mod_all[layer]), grp_pool,
                                lw["g_mix"], c["w_pool"], c["pool_scale"], tiles_per_seq=1, hist_is_x=False,
                                pos_base=past_len, state_rows=HIST_ROWS + rows)
            x2 = xo.reshape(batch, rows, D_MODEL)[:, :seq].reshape(n, D_MODEL)
            pools.append(st[:, seq + HIST_ROWS - POOL_STATE:seq + HIST_ROWS, :])
            rout_t = None
        x2 = hier_moe(x2, mods, grp, *route_w, lw["w_gate"], lw["w_up"], lw["w_down"], rout_t=rout_t)
    return (x2.reshape(batch, seq, D_MODEL), jnp.stack(ks), jnp.stack(vs), jnp.stack(kis), jnp.stack(sres),
            jnp.stack(sims), jnp.stack(pools))


def kernel(x_prompt, x_sample, c_prompt, c_sample, cache_k, cache_v, cache_kidx, state_ssm_re, state_ssm_im,
           state_pool, page_table, w_ada, b_ada, g_mix, g_ffn, w_in, q_gain, k_gain, lam_re, lam_im, log_dt,
           ssm_b_re, ssm_b_im, ssm_c_re, ssm_c_im, ssm_d, w_glu, b_glu, w_out, w_pool, pool_scale, w_coarse,
           b_coarse, w_fine, b_fine, w_gate, w_up, w_down):
    weights = dict(w_ada=w_ada, g_mix=g_mix, g_ffn=g_ffn, w_in=w_in, q_gain=q_gain, k_gain=k_gain, lam_re=lam_re,
                   lam_im=lam_im, log_dt=log_dt, ssm_b_re=ssm_b_re, ssm_b_im=ssm_b_im, ssm_c_re=ssm_c_re,
                   ssm_c_im=ssm_c_im, ssm_d=ssm_d, w_glu=w_glu, b_glu=b_glu, w_out=w_out, w_pool=w_pool,
                   pool_scale=pool_scale, w_coarse=w_coarse, b_coarse=b_coarse, w_fine=w_fine, b_fine=b_fine,
                   w_gate=w_gate, w_up=w_up, w_down=w_down)
    prep = _prep_weights(weights)
    n_prompt = c_prompt.shape[0]
    mod_all = ada_params_all(jnp.concatenate([c_prompt, c_sample], axis=0), w_ada, b_ada)
    out_p = _run_prompt(x_prompt, mod_all[:, :n_prompt], prep)
    out_s = _run_sample(x_sample, mod_all[:, n_prompt:], prep, cache_k, cache_v, cache_kidx, state_ssm_re,
                        state_ssm_im, state_pool, page_table)
    return (out_p[0], out_s[0]) + out_p[1:] + out_s[1:]
```

```python
import functools
import math

import jax
import jax.numpy as jnp
from jax import lax
from jax.experimental import pallas as pl
from jax.experimental.pallas import tpu as pltpu

F32 = jnp.float32
BF16 = jnp.bfloat16
I32 = jnp.int32

D_MODEL = 1024
EPS = 1e-6
A_WIDTH = 512
SSM_GROUP = 16
SSM_GROUPS = 32
SSM_STATE = 64
N_STATES = SSM_GROUPS * SSM_STATE
HEAD_DIM = 64
N_HEADS = 8
N_KV_HEADS = 2
KV_WIDTH = N_KV_HEADS * HEAD_DIM
IDX_HEADS = 8
IDX_DIM = 64
TOPK_MAX = 256
ROPE_THETA = 500000.0
ROT_HALF = 8
PAGE_SIZE = 128
POOL_WINDOWS = (2, 4, 8, 16)
POOL_GROUP_DIM = 256
POOL_STATE = 15
HIST_ROWS = 16
MOE_GROUPS = 4
EXPERTS_PER_GROUP = 4
N_EXPERTS = 16
D_FF_EXPERT = 256

OFF_Q, OFF_K, OFF_V, OFF_QI, OFF_KI, OFF_WI, IN_COLS = 512, 1024, 1152, 1280, 1792, 1856, 1864
IN_COLS_PADDED = 1920

V7X_LANES = 128
V7X_SUBLANES = 8
V7X_VMEM_BYTES = 64 * 2**20
VMEM_LIMIT_BYTES = (V7X_VMEM_BYTES * 3) // 4
INT_MIN = -(2**31)
NEG_INF = float("-inf")


def _params(*semantics):
    return pltpu.CompilerParams(dimension_semantics=semantics, vmem_limit_bytes=VMEM_LIMIT_BYTES)


def _norm_mod(x, gain, scale, shift):
    var = jnp.mean(x * x, axis=-1, keepdims=True)
    return (x * lax.rsqrt(var + EPS)) * gain * (1.0 + scale) + shift


def _silu(x):
    return x * jax.nn.sigmoid(x)


def _dot(a, b):
    return jnp.dot(a, b, preferred_element_type=F32)


def _dot_t(a, b):
    return lax.dot_general(a, b, (((1,), (1,)), ((), ())), preferred_element_type=F32)


class TokenGroup:
    def __init__(self, n_rows, rows_per_mod, tile):
        assert n_rows % rows_per_mod == 0 and (rows_per_mod % tile == 0 or tile % rows_per_mod == 0)
        self.n_rows, self.tile = n_rows, tile
        self.n_tiles = n_rows // tile
        if rows_per_mod >= tile:
            self.mod_rows = 1
            self.tiles_per_mod = rows_per_mod // tile
        else:
            self.mod_rows = tile
            self.tiles_per_mod = 1

    def mods(self, mod):
        if self.mod_rows == 1:
            return mod[:, None, :]
        rep = self.n_rows // mod.shape[0]
        return jnp.repeat(mod, rep, axis=0).reshape(self.n_tiles, self.tile, mod.shape[1])

    def mod_spec(self, piece):
        tpm = self.tiles_per_mod
        return pl.BlockSpec((None, self.mod_rows, D_MODEL), lambda i: (i // tpm, 0, piece))

    def row_spec(self, width):
        return pl.BlockSpec((self.tile, width), lambda i: (i, 0))


def _const_spec(shape):
    nd = len(shape)
    return pl.BlockSpec(shape, lambda *_: (0,) * nd)


def _ada_kernel(c_ref, w_ref, b_ref, o_ref):
    s = _silu(c_ref[...]).astype(BF16)
    o_ref[...] = _dot(s, w_ref[...].astype(BF16)) + b_ref[...]


def ada_params_all(c_all, w_ada, b_ada):
    n_layers, d, n6 = w_ada.shape
    m = c_all.shape[0]
    tn = 1536
    return pl.pallas_call(
        _ada_kernel,
        grid=(n_layers, n6 // tn),
        in_specs=[pl.BlockSpec((m, d), lambda l, j: (0, 0)),
                  pl.BlockSpec((None, d, tn), lambda l, j: (l, 0, j)),
                  pl.BlockSpec((None, 1, tn), lambda l, j: (l, 0, j))],
        out_specs=pl.BlockSpec((None, m, tn), lambda l, j: (l, 0, j)),
        out_shape=jax.ShapeDtypeStruct((n_layers, m, n6), F32),
        compiler_params=_params("arbitrary", "arbitrary"),
    )(c_all, w_ada, b_ada.reshape(n_layers, 1, n6))


def _rope(x, cos, s1, s2):
    w = x.shape[1]
    reps = w // V7X_LANES
    if reps > 1:
        cos, s1, s2 = (jnp.tile(t, (1, reps)) for t in (cos, s1, s2))
    return x * cos + pltpu.roll(x, w - ROT_HALF, 1) * s1 + pltpu.roll(x, ROT_HALF, 1) * s2


def _head_norm(x, mavg, gain):
    sq = x * x
    hi = sq.astype(BF16)
    lo = (sq - hi.astype(F32)).astype(BF16)
    var = _dot(hi, mavg) + _dot(lo, mavg)
    return (x * lax.rsqrt(var + EPS)) * gain


def _in_proj_kernel(x_ref, sc_ref, sh_ref, g_ref, w_ref, qg_ref, kg_ref, mavg_ref, rope_ref,
                    u_ref, q_ref, k_ref, v_ref, qi_ref, kiwi_ref, kb_ref, kib_ref, vt_ref):
    h = _norm_mod(x_ref[...], g_ref[...], sc_ref[...], sh_ref[...])
    p = _dot(h.astype(BF16), w_ref[...])
    cos, s1, s2 = rope_ref[0], rope_ref[1], rope_ref[2]
    mavg = mavg_ref[...]
    u_ref[...] = p[:, 0:OFF_Q]
    q = _rope(_head_norm(p[:, OFF_Q:OFF_K], mavg, qg_ref[...]), cos, s1, s2)
    q_ref[...] = (q * HEAD_DIM ** -0.5).astype(BF16)
    k = _head_norm(p[:, OFF_K:OFF_V], mavg[0:KV_WIDTH, 0:KV_WIDTH], kg_ref[...])
    k = _rope(k, cos, s1, s2)
    k_ref[...] = k
    kb_ref[...] = k.astype(BF16)
    v = p[:, OFF_V:OFF_QI]
    v_ref[...] = v
    vt_ref[...] = v.T.astype(BF16)
    qi = _rope(p[:, OFF_QI:OFF_KI], cos, s1, s2)
    qi_ref[...] = (qi * IDX_DIM ** -0.5).astype(BF16)
    kiwi = _rope(p[:, OFF_KI:IN_COLS_PADDED], rope_ref[3], rope_ref[4], rope_ref[5])
    kiwi_ref[...] = kiwi
    kib_ref[...] = kiwi.astype(BF16)


def _rope_tables(pos):
    t = pos.shape[0]
    inv = ROPE_THETA ** (-jnp.arange(ROT_HALF, dtype=F32) / ROT_HALF)
    ang = pos.astype(F32)[:, None] * inv[None, :]
    cos, sin = jnp.cos(ang), jnp.sin(ang)
    rest = HEAD_DIM - 2 * ROT_HALF
    z8, zr, onesr = jnp.zeros((t, ROT_HALF), F32), jnp.zeros((t, rest), F32), jnp.ones((t, rest), F32)
    c64 = jnp.concatenate([cos, cos, onesr], axis=1)
    s1_64 = jnp.concatenate([-sin, z8, zr], axis=1)
    s2_64 = jnp.concatenate([z8, sin, zr], axis=1)
    z64 = jnp.zeros((t, HEAD_DIM), F32)
    wscale = jnp.concatenate([jnp.full((t, IDX_HEADS), IDX_HEADS ** -0.5, F32),
                              jnp.ones((t, HEAD_DIM - IDX_HEADS), F32)], axis=1)
    return jnp.stack([jnp.concatenate([c64, c64], axis=1), jnp.concatenate([s1_64, s1_64], axis=1),
                      jnp.concatenate([s2_64, s2_64], axis=1), jnp.concatenate([c64, wscale], axis=1),
                      jnp.concatenate([s1_64, z64], axis=1), jnp.concatenate([s2_64, z64], axis=1)])


def in_proj(x2, mods, grp, g_mix, w_in_p, q_gain, k_gain, mavg, rope, seq_rows):
    n = x2.shape[0]
    tm = grp.tile
    rope_tiles = rope.shape[1] // tm
    tiles_per_seq = seq_rows // tm
    outs = pl.pallas_call(
        _in_proj_kernel,
        grid=(grp.n_tiles,),
        in_specs=[grp.row_spec(D_MODEL), grp.mod_spec(1), grp.mod_spec(0), _const_spec((1, D_MODEL)),
                  _const_spec((D_MODEL, IN_COLS_PADDED)), _const_spec((1, 512)), _const_spec((1, KV_WIDTH)),
                  _const_spec((512, 512)),
                  pl.BlockSpec((6, tm, V7X_LANES), lambda i: (0, i % rope_tiles, 0))],
        out_specs=[grp.row_spec(512), grp.row_spec(512), grp.row_spec(KV_WIDTH), grp.row_spec(KV_WIDTH),
                   grp.row_spec(512), grp.row_spec(V7X_LANES), grp.row_spec(KV_WIDTH), grp.row_spec(V7X_LANES),
                   pl.BlockSpec((None, KV_WIDTH, tm), lambda i: (i // tiles_per_seq, 0, i % tiles_per_seq))],
        out_shape=[jax.ShapeDtypeStruct((n, 512), F32), jax.ShapeDtypeStruct((n, 512), BF16),
                   jax.ShapeDtypeStruct((n, KV_WIDTH), F32), jax.ShapeDtypeStruct((n, KV_WIDTH), F32),
                   jax.ShapeDtypeStruct((n, 512), BF16), jax.ShapeDtypeStruct((n, V7X_LANES), F32),
                   jax.ShapeDtypeStruct((n, KV_WIDTH), BF16), jax.ShapeDtypeStruct((n, V7X_LANES), BF16),
                   jax.ShapeDtypeStruct((n // seq_rows, KV_WIDTH, seq_rows), BF16)],
        compiler_params=_params("arbitrary"),
    )(x2, mods, mods, g_mix, w_in_p, q_gain, k_gain, mavg, rope)
    return outs


S5_BATCH_BLOCK = V7X_SUBLANES
S5_HALF_IN = A_WIDTH // 2
S5_HALF_STATES = N_STATES // 2
S5_SCAN_COLS = 1024


def _gelu_tanh(x):
    cdf = 0.5 * (1.0 + jnp.tanh(math.sqrt(2.0 / math.pi) * (x + 0.044715 * (x * x * x))))
    return x * cdf


def _s5_kernel(u_ref, x0re_ref, x0im_ref, are_ref, aim_ref, bre_ref, bim_ref, cre_ref, cim_ref, d_ref,
               wglu_ref, bglu_ref, y_ref, sre_ref, sim_ref, st_re, st_im, xs_re, xs_im, *, tt):
    bb = S5_BATCH_BLOCK
    tb = pl.program_id(1)

    @pl.when(tb == 0)
    def _():
        st_re[...] = x0re_ref[...]
        st_im[...] = x0im_ref[...]

    u = u_ref[...].reshape(tt * bb, A_WIDTH)
    ub = u.astype(BF16)
    for half in range(2):
        rows = slice(half * S5_HALF_IN, (half + 1) * S5_HALF_IN)
        cols = slice(half * S5_HALF_STATES, (half + 1) * S5_HALF_STATES)
        xs_re[:, cols] = _dot(ub[:, rows], bre_ref[rows, cols])
        xs_im[:, cols] = _dot(ub[:, rows], bim_ref[rows, cols])

    for c in range(N_STATES // S5_SCAN_COLS):
        cs = slice(c * S5_SCAN_COLS, (c + 1) * S5_SCAN_COLS)
        a_re = are_ref[:, cs]
        a_im = aim_ref[:, cs]

        def step(t, carry, cs=cs, a_re=a_re, a_im=a_im):
            sr, si = carry
            r0 = pl.multiple_of(t * bb, bb)
            nr = a_re * sr - a_im * si + xs_re[pl.ds(r0, bb), cs]
            ni = a_re * si + a_im * sr + xs_im[pl.ds(r0, bb), cs]
            xs_re[pl.ds(r0, bb), cs] = nr
            xs_im[pl.ds(r0, bb), cs] = ni
            return nr, ni

        sr, si = lax.fori_loop(0, tt, step, (st_re[:, cs], st_im[:, cs]))
        st_re[:, cs] = sr
        st_im[:, cs] = si

    xr = xs_re[...].astype(BF16)
    xi = xs_im[...].astype(BF16)
    ys = []
    for half in range(2):
        rows = slice(half * S5_HALF_STATES, (half + 1) * S5_HALF_STATES)
        cols = slice(half * S5_HALF_IN, (half + 1) * S5_HALF_IN)
        ys.append(_dot(xr[:, rows], cre_ref[rows, cols]) - _dot(xi[:, rows], cim_ref[rows, cols]))
    y = jnp.concatenate(ys, axis=1) + d_ref[...] * u
    z = _gelu_tanh(y)
    o = z * jax.nn.sigmoid(_dot(z.astype(BF16), wglu_ref[...]) + bglu_ref[...])
    y_ref[...] = o.reshape(tt, bb, A_WIDTH)

    @pl.when(tb == pl.num_programs(1) - 1)
    def _():
        sre_ref[...] = st_re[...]
        sim_ref[...] = st_im[...]


def s5_discretise(lam_re, lam_im, log_dt, b_re, b_im, c_re, c_im):
    dt = jnp.exp(log_dt)[:, None]
    mag = jnp.exp(lam_re * dt)
    abar_re, abar_im = mag * jnp.cos(lam_im * dt), mag * jnp.sin(lam_im * dt)
    den = lam_re * lam_re + lam_im * lam_im
    ir, ii = lam_re / den, -lam_im / den
    cr = (abar_re - 1.0) * ir - abar_im * ii
    ci = (abar_re - 1.0) * ii + abar_im * ir
    bb_re = cr[..., None] * b_re - ci[..., None] * b_im
    bb_im = cr[..., None] * b_im + ci[..., None] * b_re
    eye = jnp.eye(SSM_GROUPS, dtype=F32)
    pack_b = lambda bb: jnp.einsum("gpn,gh->gnhp", bb, eye).reshape(A_WIDTH, N_STATES).astype(BF16)
    pack_c = lambda cc: jnp.einsum("gnp,gh->gphn", cc, eye).reshape(N_STATES, A_WIDTH).astype(BF16)
    bcast = lambda a: jnp.broadcast_to(a.reshape(1, N_STATES), (S5_BATCH_BLOCK, N_STATES))
    return bcast(abar_re), bcast(abar_im), pack_b(bb_re), pack_b(bb_im), pack_c(c_re), pack_c(c_im)


def s5_mixer(u3, x0_re, x0_im, disc, d_skip, w_glu, b_glu, tt):
    t, b, _ = u3.shape
    bb = S5_BATCH_BLOCK
    a_re, a_im, bre, bim, cre, cim = disc
    state_spec = pl.BlockSpec((bb, N_STATES), lambda i, j: (i, 0))
    seq_spec = pl.BlockSpec((tt, bb, A_WIDTH), lambda i, j: (j, i, 0))
    return pl.pallas_call(
        functools.partial(_s5_kernel, tt=tt),
        grid=(b // bb, t // tt),
        in_specs=[seq_spec, state_spec, state_spec, _const_spec((bb, N_STATES)), _const_spec((bb, N_STATES)),
                  _const_spec((A_WIDTH, N_STATES)), _const_spec((A_WIDTH, N_STATES)),
                  _const_spec((N_STATES, A_WIDTH)), _const_spec((N_STATES, A_WIDTH)),
                  _const_spec((1, A_WIDTH)), _const_spec((A_WIDTH, A_WIDTH)), _const_spec((1, A_WIDTH))],
        out_specs=[seq_spec, state_spec, state_spec],
        out_shape=[jax.ShapeDtypeStruct((t, b, A_WIDTH), F32), jax.ShapeDtypeStruct((b, N_STATES), F32),
                   jax.ShapeDtypeStruct((b, N_STATES), F32)],
        scratch_shapes=[pltpu.VMEM((bb, N_STATES), F32), pltpu.VMEM((bb, N_STATES), F32),
                        pltpu.VMEM((tt * bb, N_STATES), F32), pltpu.VMEM((tt * bb, N_STATES), F32)],
        compiler_params=_params("arbitrary", "arbitrary"),
    )(u3, x0_re, x0_im, a_re, a_im, bre, bim, cre, cim, d_skip, w_glu, b_glu)


def _count(mask):
    return jnp.sum(jnp.where(mask, 1.0, 0.0), axis=1, keepdims=True)


def _topk_bias(sc, keep, qpos0, key_ref, bias_ref):
    m, s_keys = sc.shape
    keep = float(keep)
    kpos = lax.broadcasted_iota(I32, (m, s_keys), 1)
    qpos = qpos0 + lax.broadcasted_iota(I32, (m, 1), 0)
    causal = kpos <= qpos
    bits = lax.bitcast_convert_type(sc, I32)
    mag = bits & 0x7FFFFFFF
    key_ref[...] = jnp.where(causal, jnp.where(bits < 0, -mag, mag), INT_MIN)

    def two_bits(it, thr):
        unit = lax.shift_left(jnp.int32(1), 30 - 2 * it)
        key = key_ref[...]
        new = thr
        for cand in (thr + unit, thr + 2 * unit, thr + 3 * unit):
            new = jnp.where(_count(key >= cand) >= keep, cand, new)
        return new

    thr = lax.fori_loop(0, 16, two_bits, jnp.full((m, 1), INT_MIN, I32))
    key = key_ref[...]
    gt = key > thr
    eq = key == thr
    need = keep - _count(gt)
    bias_ref[...] = jnp.where((gt | eq) & causal, 0.0, NEG_INF)
    tie = (_count(eq) > need) & (thr > INT_MIN)

    @pl.when(jnp.max(jnp.where(tie, 1.0, 0.0)) > 0.0)
    def _():
        def idx_step(_, lohi):
            lo, hi = lohi
            mid = (lo + hi) >> 1
            ok = _count((key_ref[...] == thr) & (kpos <= mid)) >= need
            return jnp.where(ok, lo, mid + 1), jnp.where(ok, mid, hi)

        n_steps = max(1, (s_keys - 1).bit_length())
        last, _ = lax.fori_loop(0, n_steps, idx_step,
                                (jnp.zeros((m, 1), I32), jnp.full((m, 1), s_keys - 1, I32)))
        k2 = key_ref[...]
        sel = (k2 > thr) | ((k2 == thr) & (kpos <= last))
        bias_ref[...] = jnp.where(sel & causal, 0.0, NEG_INF)


def _select_attend_t(s_keys, keep, qpos0, q_hm, qi_hm, wi_t, ki_ref, k_ref, vt_ref, key_ref, bias_ref, o_ref):
    tq = o_ref.shape[0]
    ck = ATTN_KEY_CHUNK if s_keys % ATTN_KEY_CHUNK == 0 else tq
    assert s_keys % ck == 0
    chunks = [slice(c * ck, (c + 1) * ck) for c in range(s_keys // ck)]
    group = N_HEADS // N_KV_HEADS
    keep = float(keep)
    qpos = qpos0 + lax.broadcasted_iota(I32, (1, tq), 1)
    lanes = lambda a, j: a[:, j * tq:(j + 1) * tq]
    fold8 = lambda a, op: op(a.reshape(a.shape[0] // V7X_SUBLANES, V7X_SUBLANES, a.shape[1]), axis=0)

    for c, rows in enumerate(chunks):
        ki_c = ki_ref[rows, 0:IDX_DIM]
        sc = jnp.zeros((ck, tq), F32)
        for h in range(IDX_HEADS):
            sc = sc + jnp.maximum(_dot_t(ki_c, qi_hm[h * tq:(h + 1) * tq, :]), 0.0) * wi_t[h:h + 1, :]
        kpos = c * ck + lax.broadcasted_iota(I32, (ck, tq), 0)
        bits = lax.bitcast_convert_type(sc, I32)
        mag = bits & 0x7FFFFFFF
        key_ref[rows, :] = jnp.where(kpos <= qpos, jnp.where(bits < 0, -mag, mag), INT_MIN)

    ones = jnp.ones((V7X_SUBLANES, s_keys), BF16)

    def counts(conds):
        mask = jnp.concatenate([jnp.where(m, 1.0, 0.0).astype(BF16) for m in conds], axis=1)
        n = _dot(ones, mask)[0:1, :]
        return [lanes(n, j) for j in range(len(conds))]

    def two_bits(it, carry):
        thr, cnt = carry
        unit = lax.shift_left(jnp.int32(1), 30 - 2 * it)
        cands = [thr + unit, thr + 2 * unit, thr + 3 * unit]
        key = key_ref[...]
        for cand, n in zip(cands, counts([key >= cand for cand in cands])):
            ok = n >= keep
            thr, cnt = jnp.where(ok, cand, thr), jnp.where(ok, n, cnt)
        return thr, cnt

    start = (jnp.full((1, tq), INT_MIN, I32), jnp.full((1, tq), s_keys, F32))
    thr, cnt = lax.cond(qpos0 + tq > int(keep), lambda: lax.fori_loop(0, 16, two_bits, start), lambda: start)
    bias_ref[...] = jnp.where(key_ref[...] >= jnp.maximum(thr, INT_MIN + 1), 0.0, NEG_INF)
    tie = (cnt > keep) & (thr > INT_MIN)

    @pl.when(jnp.max(jnp.where(tie, 1.0, 0.0)) > 0.0)
    def _():
        kpos = lax.broadcasted_iota(I32, (s_keys, tq), 0)
        need = keep - counts([key_ref[...] > thr])[0]

        def idx_step(_, lohi):
            lo, hi = lohi
            mid = (lo + hi) >> 1
            ok = counts([(key_ref[...] == thr) & (kpos <= mid)])[0] >= need
            return jnp.where(ok, lo, mid + 1), jnp.where(ok, mid, hi)

        n_steps = max(1, (s_keys - 1).bit_length())
        last, _ = lax.fori_loop(0, n_steps, idx_step,
                                (jnp.zeros((1, tq), I32), jnp.full((1, tq), s_keys - 1, I32)))
        k2 = key_ref[...]
        sel = (k2 > thr) | ((k2 == thr) & (kpos <= last))
        bias_ref[...] = jnp.where(sel & (k2 > INT_MIN), 0.0, NEG_INF)

    outs = []
    for g in range(N_KV_HEADS):
        q4 = q_hm[g * group * tq:(g + 1) * group * tq, :]
        feats = slice(g * HEAD_DIM, (g + 1) * HEAD_DIM)
        score = lambda rows: _dot_t(k_ref[rows, feats], q4) + jnp.tile(bias_ref[rows, :], (1, group))
        m8 = None
        for rows in chunks:
            mc = fold8(score(rows), jnp.max)
            m8 = mc if m8 is None else jnp.maximum(m8, mc)
        m = jnp.max(m8, axis=0, keepdims=True)
        l8 = jnp.zeros((V7X_SUBLANES, group * tq), F32)
        ot = jnp.zeros((HEAD_DIM, group * tq), F32)
        for rows in chunks:
            p = jnp.exp(score(rows) - m)
            l8 = l8 + fold8(p, jnp.sum)
            ot = ot + _dot(vt_ref[feats, rows], p.astype(BF16))
        ot = ot / jnp.sum(l8, axis=0, keepdims=True)
        outs.extend(lanes(ot, h) for h in range(group))
    o_ref[...] = jnp.concatenate(outs, axis=0).T.astype(o_ref.dtype)


def _attn_prompt_kernel(q_ref, qi_ref, wit_ref, k_ref, vt_ref, ki_ref, o_ref, key_ref, bias_ref, *, tq, buckets,
                        keep):
    i = pl.program_id(1)
    need_keys = (i + 1) * tq
    lo = 0
    for s_keys in buckets:
        @pl.when((need_keys > lo) & (need_keys <= s_keys))
        def _(s_keys=s_keys):
            _select_attend_t(s_keys, keep, i * tq, q_ref[...], qi_ref[...], wit_ref[...], ki_ref, k_ref, vt_ref,
                             key_ref.at[0:s_keys, :], bias_ref.at[0:s_keys, :], o_ref)
        lo = s_keys


def attend_prompt(q, qi, kiwi, kb, kib, vt, batch, seq, tq, n_buckets=8):
    n = q.shape[0]
    nq = seq // tq
    step = max(tq, seq // n_buckets)
    buckets = tuple(range(step, seq + 1, step))
    head_major = lambda a: jnp.swapaxes(a.reshape(n // tq, tq, N_HEADS, HEAD_DIM), 1, 2).reshape(
        n // tq, N_HEADS * tq, HEAD_DIM)
    wi_t = jnp.swapaxes(kiwi[:, IDX_DIM:IDX_DIM + IDX_HEADS].reshape(n // tq, tq, IDX_HEADS), 1, 2)
    hm_spec = pl.BlockSpec((None, N_HEADS * tq, HEAD_DIM), lambda b, i: (b * nq + i, 0, 0))
    kv_spec = pl.BlockSpec((seq, KV_WIDTH), lambda b, i: (b, 0))
    return pl.pallas_call(
        functools.partial(_attn_prompt_kernel, tq=tq, buckets=buckets, keep=min(TOPK_MAX, seq // 4)),
        grid=(batch, nq),
        in_specs=[hm_spec, hm_spec, pl.BlockSpec((None, IDX_HEADS, tq), lambda b, i: (b * nq + i, 0, 0)),
                  kv_spec, pl.BlockSpec((None, KV_WIDTH, seq), lambda b, i: (b, 0, 0)), kv_spec],
        out_specs=pl.BlockSpec((tq, 512), lambda b, i: (b * nq + i, 0)),
        out_shape=jax.ShapeDtypeStruct((n, 512), BF16),
        scratch_shapes=[pltpu.VMEM((seq, tq), I32), pltpu.VMEM((seq, tq), F32)],
        compiler_params=_params("arbitrary", "arbitrary"),
    )(head_major(q), head_major(qi), wi_t, kb, vt, kib)


ATTN_TQ = 128
ATTN_KEY_CHUNK = 256
SAMPLE_Q_ROWS = V7X_SUBLANES


def _attn_sample_kernel(pt_ref, q_ref, qi_ref, wi_ref, knew_ref, vnew_ref, kinew_ref, ck_hbm, cv_hbm, cki_hbm,
                        o_ref, k_all, v_all, ki_all, sems, key_ref, bias_ref, *, layer, n_pages, keep):
    b = pl.program_id(0)
    slot = b % 2
    past = n_pages * PAGE_SIZE

    def page_copies(seq, sl):
        copies = []
        for p in range(n_pages):
            phys = pt_ref[seq, p]
            window = pl.ds(p * PAGE_SIZE, PAGE_SIZE)
            copies.append(pltpu.make_async_copy(ck_hbm.at[layer, phys], k_all.at[sl, :, window], sems.at[sl, 0]))
            copies.append(pltpu.make_async_copy(cv_hbm.at[layer, phys], v_all.at[sl, :, window], sems.at[sl, 1]))
            copies.append(pltpu.make_async_copy(cki_hbm.at[layer, phys], ki_all.at[sl, :, window], sems.at[sl, 2]))
        return copies

    @pl.when(b == 0)
    def _():
        for c in page_copies(0, 0):
            c.start()

    @pl.when(b + 1 < pl.num_programs(0))
    def _():
        for c in page_copies(b + 1, 1 - slot):
            c.start()

    k_all[slot, :, past:past + PAGE_SIZE] = knew_ref[...]
    v_all[slot, :, past:past + PAGE_SIZE] = vnew_ref[...]
    ki_all[slot, :, past:past + PAGE_SIZE] = kinew_ref[...]
    for c in page_copies(b, slot):
        c.wait()
    rows = SAMPLE_Q_ROWS
    d = _dot(qi_ref[...], ki_all[slot].astype(BF16))
    wi = wi_ref[:, IDX_DIM:IDX_DIM + IDX_HEADS]
    sc = jnp.zeros((rows, past + PAGE_SIZE), F32)
    for h in range(IDX_HEADS):
        sc = sc + jnp.maximum(d[h * rows:(h + 1) * rows, :], 0.0) * wi[:, h:h + 1]
    _topk_bias(sc, keep, past, key_ref, bias_ref)
    group = N_HEADS // N_KV_HEADS
    bias = jnp.tile(bias_ref[...], (group, 1))
    q = q_ref[...]
    for g in range(N_KV_HEADS):
        feats = slice(g * HEAD_DIM, (g + 1) * HEAD_DIM)
        qrows = slice(g * group * rows, (g + 1) * group * rows)
        s = _dot(q[qrows, :], k_all[slot, feats, :].astype(BF16)) + bias
        p = jnp.exp(s - jnp.max(s, axis=1, keepdims=True))
        l = jnp.sum(p, axis=1, keepdims=True)
        o_ref[qrows, :] = _dot_t(p.astype(BF16), v_all[slot, feats, :].astype(BF16)) / l


def attend_sample(q_st, qi_st, kiwi3, k_new_t, v_new_t, ki_new_t, cache_k_t, cache_v_t, cache_ki_t, page_table,
                  layer, seq):
    b = q_st.shape[0]
    n_pages = page_table.shape[1]
    s_keys = (n_pages + 1) * PAGE_SIZE
    rows = SAMPLE_Q_ROWS
    st_spec = pl.BlockSpec((None, N_HEADS * rows, HEAD_DIM), lambda i, pt: (i, 0, 0))
    q_spec = lambda w: pl.BlockSpec((None, rows, w), lambda i, pt: (i, 0, 0))
    new_spec = lambda f: pl.BlockSpec((None, f, PAGE_SIZE), lambda i, pt: (i, 0, 0))
    hbm = pl.BlockSpec(memory_space=pl.ANY)
    grid_spec = pltpu.PrefetchScalarGridSpec(
        num_scalar_prefetch=1,
        grid=(b,),
        in_specs=[st_spec, st_spec, q_spec(V7X_LANES), new_spec(KV_WIDTH), new_spec(KV_WIDTH),
                  new_spec(IDX_DIM), hbm, hbm, hbm],
        out_specs=st_spec,
        scratch_shapes=[pltpu.VMEM((2, KV_WIDTH, s_keys), F32), pltpu.VMEM((2, KV_WIDTH, s_keys), F32),
                        pltpu.VMEM((2, IDX_DIM, s_keys), F32), pltpu.SemaphoreType.DMA((2, 3)),
                        pltpu.VMEM((rows, s_keys), I32), pltpu.VMEM((rows, s_keys), F32)],
    )
    keep = min(TOPK_MAX, (n_pages * PAGE_SIZE + seq) // 4)
    return pl.pallas_call(
        functools.partial(_attn_sample_kernel, layer=layer, n_pages=n_pages, keep=keep),
        grid_spec=grid_spec,
        out_shape=jax.ShapeDtypeStruct((b, N_HEADS * rows, HEAD_DIM), F32),
        compiler_params=_params("arbitrary"),
    )(page_table, q_st, qi_st, kiwi3, k_new_t, v_new_t, ki_new_t, cache_k_t, cache_v_t, cache_ki_t)


def _out_proj_kernel(x_ref, ya_ref, yb_ref, gate_ref, w_ref, sc2_ref, sh2_ref, g2_ref, wr_ref, br_ref,
                     o_ref, rout_ref):
    y = _dot(ya_ref[...].astype(BF16), w_ref[0:A_WIDTH, :]) + _dot(yb_ref[...], w_ref[A_WIDTH:2 * A_WIDTH, :])
    x_new = x_ref[...] + gate_ref[...] * y
    o_ref[...] = x_new
    rout_ref[...] = _route_rows(x_new, g2_ref[...], sc2_ref[...], sh2_ref[...], wr_ref[...], br_ref[...])


def out_proj(x2, y_a, y_b, mods, grp, w_out, g_ffn, w_router, b_router):
    return pl.pallas_call(
        _out_proj_kernel,
        grid=(grp.n_tiles,),
        in_specs=[grp.row_spec(D_MODEL), grp.row_spec(A_WIDTH), grp.row_spec(512), grp.mod_spec(2),
                  _const_spec((2 * A_WIDTH, D_MODEL)), grp.mod_spec(4), grp.mod_spec(3), _const_spec((1, D_MODEL)),
                  _const_spec((D_MODEL, ROUTER_COLS)), _const_spec((1, ROUTER_COLS))],
        out_specs=[grp.row_spec(D_MODEL), pl.BlockSpec((ROUTE_ROWS, grp.tile), lambda i: (0, i))],
        out_shape=[jax.ShapeDtypeStruct(x2.shape, F32), jax.ShapeDtypeStruct((ROUTE_ROWS, x2.shape[0]), F32)],
        compiler_params=_params("arbitrary"),
    )(x2, y_a, y_b, mods, w_out, mods, mods, g_ffn, w_router, b_router)


def _pool_kernel(x_ref, hist_ref, sc_ref, sh_ref, gate_ref, g_ref, w_ref, ps_ref, *rest, tiles_per_seq, hist_is_x,
                 pos_base, state_rows, emit_route):
    if emit_route:
        sc2_ref, sh2_ref, g2_ref, wr_ref, br_ref, o_ref, st_ref, rout_ref = rest
    else:
        o_ref, st_ref = rest
    i = pl.program_id(0)
    tm = x_ref.shape[0]
    x = x_ref[...]
    h = _norm_mod(x, g_ref[...], sc_ref[...], sh_ref[...])
    if hist_is_x:
        hist = _norm_mod(hist_ref[...], g_ref[...], sc_ref[...], sh_ref[...])
        hist = jnp.where(i % tiles_per_seq == 0, 0.0, hist)
    else:
        hist = hist_ref[...]
    ext = jnp.concatenate([hist, h], axis=0)
    pos = pos_base + (i % tiles_per_seq) * tm + lax.broadcasted_iota(I32, (tm, 1), 0)
    ys = []
    for g, win in enumerate(POOL_WINDOWS):
        cols = slice(g * POOL_GROUP_DIM, (g + 1) * POOL_GROUP_DIM)
        s = ext[:, cols]
        shift = 1
        while shift < win:
            s = s + pltpu.roll(s, shift, 0)
            shift *= 2
        cnt = jnp.minimum(pos + 1, win).astype(F32)
        pooled = s[HIST_ROWS:, :] / cnt - h[:, cols]
        ys.append(_dot(pooled.astype(BF16), w_ref[g]))
    y = jnp.concatenate(ys, axis=1) * ps_ref[...]
    x_new = x + gate_ref[...] * y
    o_ref[...] = x_new
    st_ref[...] = ext[HIST_ROWS + tm - state_rows:, :]
    if emit_route:
        rout_ref[...] = _route_rows(x_new, g2_ref[...], sc2_ref[...], sh2_ref[...], wr_ref[...], br_ref[...])


def pool_mixer(x2, hist, mods, grp, g_mix, w_pool, pool_scale, route_weights=None, *, tiles_per_seq, hist_is_x,
               pos_base, state_rows):
    tm = grp.tile
    n_seq = grp.n_tiles // tiles_per_seq
    if hist_is_x:
        per_tile = tm // HIST_ROWS
        hist_spec = pl.BlockSpec((HIST_ROWS, D_MODEL), lambda i: (jnp.maximum(i * per_tile - 1, 0), 0))
    else:
        hist_spec = pl.BlockSpec((None, HIST_ROWS, D_MODEL), lambda i: (i, 0, 0))
    emit_route = route_weights is not None
    in_specs = [grp.row_spec(D_MODEL), hist_spec, grp.mod_spec(1), grp.mod_spec(0), grp.mod_spec(2),
                _const_spec((1, D_MODEL)), _const_spec((len(POOL_WINDOWS), POOL_GROUP_DIM, POOL_GROUP_DIM)),
                _const_spec((1, D_MODEL))]
    out_specs = [grp.row_spec(D_MODEL),
                 pl.BlockSpec((None, state_rows, D_MODEL), lambda i: (i // tiles_per_seq, 0, 0))]
    out_shape = [jax.ShapeDtypeStruct(x2.shape, F32), jax.ShapeDtypeStruct((n_seq, state_rows, D_MODEL), F32)]
    args = [x2, hist, mods, mods, mods, g_mix, w_pool, pool_scale]
    if emit_route:
        in_specs += [grp.mod_spec(4), grp.mod_spec(3), _const_spec((1, D_MODEL)),
                     _const_spec((D_MODEL, ROUTER_COLS)), _const_spec((1, ROUTER_COLS))]
        out_specs.append(pl.BlockSpec((ROUTE_ROWS, tm), lambda i: (0, i)))
        out_shape.append(jax.ShapeDtypeStruct((ROUTE_ROWS, x2.shape[0]), F32))
        args += [mods, mods, *route_weights]
    return pl.pallas_call(
        functools.partial(_pool_kernel, tiles_per_seq=tiles_per_seq, hist_is_x=hist_is_x, pos_base=pos_base,
                          state_rows=state_rows, emit_route=emit_route),
        grid=(grp.n_tiles,),
        in_specs=in_specs,
        out_specs=out_specs,
        out_shape=out_shape,
        compiler_params=_params("arbitrary"),
    )(*args)


ROUTER_COLS = V7X_LANES


MOE_GROUP_ROW = EXPERTS_PER_GROUP
ROUTE_ROWS = V7X_SUBLANES
MOE_CHUNK = 256
MOE_SEG_ALIGN = V7X_SUBLANES


def _router_gates(logits_t):
    col = lambda j: logits_t[j:j + 1, :]
    lc = [col(j) for j in range(MOE_GROUPS)]
    mc = functools.reduce(jnp.maximum, lc)
    pg = 1.0 / functools.reduce(lambda a, b: a + b, [jnp.exp(l - mc) for l in lc])
    grp = jnp.where(lc[0] == mc, 0, jnp.where(lc[1] == mc, 1, jnp.where(lc[2] == mc, 2, 3)))
    fl = []
    for j in range(EXPERTS_PER_GROUP):
        cands = [col(MOE_GROUPS + g * EXPERTS_PER_GROUP + j) for g in range(MOE_GROUPS)]
        fl.append(jnp.where(grp == 0, cands[0], jnp.where(grp == 1, cands[1],
                                                          jnp.where(grp == 2, cands[2], cands[3]))))
    first = lambda vals, mx: jnp.where(vals[0] == mx, 0, jnp.where(vals[1] == mx, 1, jnp.where(vals[2] == mx, 2, 3)))
    m1 = functools.reduce(jnp.maximum, fl)
    i1 = first(fl, m1)
    rest = [jnp.where(i1 == j, NEG_INF, fl[j]) for j in range(EXPERTS_PER_GROUP)]
    m2 = functools.reduce(jnp.maximum, rest)
    i2 = first(rest, m2)
    e2 = jnp.exp(m2 - m1)
    w1 = pg / (1.0 + e2)
    w2 = pg * e2 / (1.0 + e2)
    return grp, [jnp.where(i1 == j, w1, jnp.where(i2 == j, w2, 0.0)) for j in range(EXPERTS_PER_GROUP)]


def _route_rows(x, g_ffn, scale, shift, w_router, b_router):
    h = _norm_mod(x, g_ffn, scale, shift)
    h_hi, w_hi = h.astype(BF16), w_router.astype(BF16)
    h_lo = (h - h_hi.astype(F32)).astype(BF16)
    w_lo = (w_router - w_hi.astype(F32)).astype(BF16)
    logits = _dot(h_hi, w_hi) + (_dot(h_lo, w_hi) + _dot(h_hi, w_lo)) + b_router
    grp, gates = _router_gates(logits.T)
    sub = lax.broadcasted_iota(I32, (ROUTE_ROWS, x.shape[0]), 0)
    out = jnp.where(sub == MOE_GROUP_ROW, grp.astype(F32), 0.0)
    for j, gj in enumerate(gates):
        out = jnp.where(sub == j, gj, out)
    return out


def _route_kernel(x_ref, sc_ref, sh_ref, g_ref, wr_ref, br_ref, o_ref):
    o_ref[...] = _route_rows(x_ref[...], g_ref[...], sc_ref[...], sh_ref[...], wr_ref[...], br_ref[...])


def moe_route(x2, mods, grp, g_ffn, w_router, b_router):
    return pl.pallas_call(
        _route_kernel,
        grid=(grp.n_tiles,),
        in_specs=[grp.row_spec(D_MODEL), grp.mod_spec(4), grp.mod_spec(3), _const_spec((1, D_MODEL)),
                  _const_spec((D_MODEL, ROUTER_COLS)), _const_spec((1, ROUTER_COLS))],
        out_specs=pl.BlockSpec((ROUTE_ROWS, grp.tile), lambda i: (0, i)),
        out_shape=jax.ShapeDtypeStruct((ROUTE_ROWS, x2.shape[0]), F32),
        compiler_params=_params("arbitrary"),
    )(x2, mods, mods, g_ffn, w_router, b_router)


def _route_plan(rout_t, tm):
    n_tiles = rout_t.shape[1] // tm
    grp = rout_t[MOE_GROUP_ROW].astype(I32).reshape(n_tiles, tm)
    onehot = (grp[:, :, None] == jnp.arange(MOE_GROUPS, dtype=I32)[None, None, :]).astype(I32)
    cnt = jnp.sum(onehot, axis=1)
    padded = (cnt + MOE_SEG_ALIGN - 1) // MOE_SEG_ALIGN * MOE_SEG_ALIGN
    start = jnp.cumsum(padded, axis=1) - padded
    before = jnp.cumsum(onehot, axis=1) - onehot
    dest = jnp.sum(onehot * (before + start[:, None, :]), axis=2)
    return dest.reshape(-1), start.reshape(-1), cnt.reshape(-1)


def _moe_kernel(dest_ref, start_ref, cnt_ref, x_ref, sc_ref, sh_ref, gate_ref, g_ref, rout_ref, wg_ref, wu_ref,
                wd_ref, o_ref, hs_ref, gs_ref, ys_ref, *, chunk):
    i = pl.program_id(0)
    g = pl.program_id(1)
    tm = x_ref.shape[0]
    base = i * tm

    @pl.when((i == 0) & (g == 0))
    def _():
        hs_ref[...] = jnp.zeros_like(hs_ref)
        gs_ref[...] = jnp.zeros_like(gs_ref)

    @pl.when(g == 0)
    def _():
        o_ref[...] = _norm_mod(x_ref[...], g_ref[...], sc_ref[...], sh_ref[...])

        def move(r, carry):
            d = dest_ref[base + r]
            hs_ref[pl.ds(d, 1), :] = o_ref[pl.ds(r, 1), :]
            gs_ref[pl.ds(d, 1), :] = rout_ref[pl.ds(r, 1), :]
            return carry

        lax.fori_loop(0, tm, move, 0, unroll=8)

    start = start_ref[i * MOE_GROUPS + g]
    cnt = cnt_ref[i * MOE_GROUPS + g]

    def expert_rows(r0, size, n_valid):
        r0 = pl.multiple_of(r0, MOE_SEG_ALIGN)
        xs = hs_ref[pl.ds(r0, size), :].astype(BF16)
        gates = gs_ref[pl.ds(r0, size), :]
        valid = None if n_valid is None else lax.broadcasted_iota(I32, (size, 1), 0) < n_valid
        y = jnp.zeros((size, D_MODEL), F32)
        for e in range(EXPERTS_PER_GROUP):
            hid = _silu(_dot(xs, wg_ref[e])) * _dot(xs, wu_ref[e]) * gates[:, e:e + 1]
            if valid is not None:
                hid = jnp.where(valid, hid, 0.0)
            y = y + _dot(hid.astype(BF16), wd_ref[e])
        ys_ref[pl.ds(r0, size), :] = y

    def full_chunk(c, carry):
        expert_rows(start + c * chunk, chunk, None)
        return carry

    n_full = cnt // chunk
    lax.fori_loop(0, n_full, full_chunk, 0)
    rem = cnt - n_full * chunk
    tail = start + n_full * chunk
    half = chunk // 2

    @pl.when(rem > half)
    def _():
        expert_rows(tail, chunk, rem)

    @pl.when((rem > 0) & (rem <= half))
    def _():
        expert_rows(tail, half, rem)

    @pl.when(g == pl.num_programs(1) - 1)
    def _():
        def move_back(r, carry):
            d = dest_ref[base + r]
            o_ref[pl.ds(r, 1), :] = ys_ref[pl.ds(d, 1), :]
            return carry

        lax.fori_loop(0, tm, move_back, 0, unroll=8)
        o_ref[...] = x_ref[...] + gate_ref[...] * o_ref[...]


def hier_moe(x2, mods, grp, g_ffn, w_router, b_router, w_gate, w_up, w_down, rout_t=None):
    tm = grp.tile
    chunk = min(MOE_CHUNK, tm)
    if rout_t is None:
        rout_t = moe_route(x2, mods, grp, g_ffn, w_router, b_router)
    dest, start, cnt = _route_plan(rout_t, tm)
    rout = rout_t.T
    buf_rows = tm + chunk + MOE_GROUPS * MOE_SEG_ALIGN
    row = lambda w: pl.BlockSpec((tm, w), lambda i, g, *_: (i, 0))
    tpm = grp.tiles_per_mod
    mod_spec = lambda piece: pl.BlockSpec((None, grp.mod_rows, D_MODEL), lambda i, g, *_: (i // tpm, 0, piece))
    epg = EXPERTS_PER_GROUP
    grid_spec = pltpu.PrefetchScalarGridSpec(
        num_scalar_prefetch=3,
        grid=(grp.n_tiles, MOE_GROUPS),
        in_specs=[row(D_MODEL), mod_spec(4), mod_spec(3), mod_spec(5),
                  pl.BlockSpec((1, D_MODEL), lambda i, g, *_: (0, 0)), row(ROUTE_ROWS),
                  pl.BlockSpec((epg, D_MODEL, D_FF_EXPERT), lambda i, g, *_: (g, 0, 0)),
                  pl.BlockSpec((epg, D_MODEL, D_FF_EXPERT), lambda i, g, *_: (g, 0, 0)),
                  pl.BlockSpec((epg, D_FF_EXPERT, D_MODEL), lambda i, g, *_: (g, 0, 0))],
        out_specs=row(D_MODEL),
        scratch_shapes=[pltpu.VMEM((buf_rows, D_MODEL), F32), pltpu.VMEM((buf_rows, ROUTE_ROWS), F32),
                        pltpu.VMEM((buf_rows, D_MODEL), F32)],
    )
    return pl.pallas_call(
        functools.partial(_moe_kernel, chunk=chunk),
        grid_spec=grid_spec,
        out_shape=jax.ShapeDtypeStruct(x2.shape, F32),
        compiler_params=_params("arbitrary", "arbitrary"),
    )(dest, start, cnt, x2, mods, mods, mods, g_ffn, rout, w_gate, w_up, w_down)


def _prep_weights(w):
    depth = w["w_ada"].shape[0]
    n_ab = w["w_in"].shape[0]
    prep = {"layers": [], "ab": [], "c": []}
    blk = jnp.arange(512) // HEAD_DIM
    prep["mavg"] = jnp.where(blk[:, None] == blk[None, :], 1.0 / HEAD_DIM, 0.0).astype(BF16)
    for l in range(depth):
        wr = jnp.concatenate([w["w_coarse"][l], w["w_fine"][l]], axis=1)
        br = jnp.concatenate([w["b_coarse"][l], w["b_fine"][l]])
        pad = ROUTER_COLS - wr.shape[1]
        prep["layers"].append(dict(
            g_mix=w["g_mix"][l][None, :], g_ffn=w["g_ffn"][l][None, :],
            w_router=jnp.pad(wr, ((0, 0), (0, pad))), b_router=jnp.pad(br, (0, pad))[None, :],
            w_gate=w["w_gate"][l].astype(BF16), w_up=w["w_up"][l].astype(BF16), w_down=w["w_down"][l].astype(BF16)))
    for i in range(n_ab):
        prep["ab"].append(dict(
            w_in=jnp.pad(w["w_in"][i], ((0, 0), (0, IN_COLS_PADDED - IN_COLS))).astype(BF16),
            q_gain=jnp.tile(w["q_gain"][i], N_HEADS)[None, :], k_gain=jnp.tile(w["k_gain"][i], N_KV_HEADS)[None, :],
            disc=s5_discretise(w["lam_re"][i], w["lam_im"][i], w["log_dt"][i], w["ssm_b_re"][i], w["ssm_b_im"][i],
                               w["ssm_c_re"][i], w["ssm_c_im"][i]),
            d_skip=w["ssm_d"][i][None, :], w_glu=w["w_glu"][i].astype(BF16), b_glu=w["b_glu"][i][None, :],
            w_out=w["w_out"][i].astype(BF16)))
    for i in range(w["w_pool"].shape[0]):
        prep["c"].append(dict(w_pool=w["w_pool"][i].astype(BF16), pool_scale=w["pool_scale"][i][None, :]))
    return prep


def _run_prompt(x, mod_all, prep):
    batch, seq, _ = x.shape
    n = batch * seq
    x2 = x.reshape(n, D_MODEL)
    rope = _rope_tables(jnp.arange(seq, dtype=I32))
    tile = 512
    grp = TokenGroup(n, seq, tile)
    grp_moe = TokenGroup(n, seq, 1024)
    zero_state = jnp.zeros((batch, N_STATES), F32)
    ks, vs, kis, sres, sims, pools = [], [], [], [], [], []
    for layer in range(mod_all.shape[0]):
        i = layer // 2
        lw = prep["layers"][layer]
        route_w = (lw["g_ffn"], lw["w_router"], lw["b_router"])
        mods = grp.mods(mod_all[layer])
        if layer % 2 == 0:
            ab = prep["ab"][i]
            u, q, k, v, qi, kiwi, kb, kib, vt = in_proj(x2, mods, grp, lw["g_mix"], ab["w_in"], ab["q_gain"],
                                                        ab["k_gain"], prep["mavg"], rope, seq)
            u3 = jnp.swapaxes(u.reshape(batch, seq, A_WIDTH), 0, 1)
            y3, s_re, s_im = s5_mixer(u3, zero_state, zero_state, ab["disc"], ab["d_skip"], ab["w_glu"],
                                      ab["b_glu"], tt=64)
            y_a = jnp.swapaxes(y3, 0, 1).reshape(n, A_WIDTH)
            y_b = attend_prompt(q, qi, kiwi, kb, kib, vt, batch, seq, ATTN_TQ)
            x2, rout_t = out_proj(x2, y_a, y_b, mods, grp, ab["w_out"], *route_w)
            ks.append(k.reshape(batch, seq, N_KV_HEADS, HEAD_DIM))
            vs.append(v.reshape(batch, seq, N_KV_HEADS, HEAD_DIM))
            kis.append(kiwi[:, :IDX_DIM].reshape(batch, seq, IDX_DIM))
            sres.append(s_re.reshape(batch, SSM_GROUPS, SSM_STATE))
            sims.append(s_im.reshape(batch, SSM_GROUPS, SSM_STATE))
        else:
            c = prep["c"][i]
            x2, st, rout_t = pool_mixer(x2, x2, mods, grp, lw["g_mix"], c["w_pool"], c["pool_scale"], route_w,
                                        tiles_per_seq=seq // tile, hist_is_x=True, pos_base=0,
                                        state_rows=HIST_ROWS)
            pools.append(st[:, HIST_ROWS - POOL_STATE:, :])
        x2 = hier_moe(x2, grp_moe.mods(mod_all[layer]), grp_moe, *route_w, lw["w_gate"], lw["w_up"], lw["w_down"],
                      rout_t=rout_t)
    return (x2.reshape(batch, seq, D_MODEL), jnp.stack(ks), jnp.stack(vs), jnp.stack(kis), jnp.stack(sres),
            jnp.stack(sims), jnp.stack(pools))


def _run_sample(x, mod_all, prep, cache_k, cache_v, cache_kidx, state_re, state_im, state_pool, page_table):
    batch, seq, _ = x.shape
    n = batch * seq
    past_len = page_table.shape[1] * PAGE_SIZE
    x2 = x.reshape(n, D_MODEL)
    rope = _rope_tables(jnp.tile(past_len + jnp.arange(seq, dtype=I32), batch))
    grp = TokenGroup(n, seq, n)
    rows = SAMPLE_Q_ROWS
    grp_pool = TokenGroup(batch * rows, rows, rows)
    pad_q = lambda a: jnp.pad(a.reshape(batch, seq, a.shape[-1]), ((0, 0), (0, rows - seq), (0, 0)))
    page_t = lambda a: jnp.pad(jnp.swapaxes(a.reshape(batch, seq, a.shape[-1]), 1, 2),
                               ((0, 0), (0, 0), (0, PAGE_SIZE - seq)))
    stack_heads = lambda a: jnp.swapaxes(pad_q(a).reshape(batch, rows, N_HEADS, HEAD_DIM), 1, 2).reshape(
        batch, N_HEADS * rows, HEAD_DIM)
    n_layers_ab, n_phys = cache_k.shape[:2]
    cache_k_t = jnp.transpose(cache_k, (0, 1, 3, 4, 2)).reshape(n_layers_ab, n_phys, KV_WIDTH, PAGE_SIZE)
    cache_v_t = jnp.transpose(cache_v, (0, 1, 3, 4, 2)).reshape(n_layers_ab, n_phys, KV_WIDTH, PAGE_SIZE)
    cache_ki_t = jnp.transpose(cache_kidx, (0, 1, 3, 2))
    ks, vs, kis, sres, sims, pools = [], [], [], [], [], []
    for layer in range(mod_all.shape[0]):
        i = layer // 2
        lw = prep["layers"][layer]
        route_w = (lw["g_ffn"], lw["w_router"], lw["b_router"])
        mods = grp.mods(mod_all[layer])
        if layer % 2 == 0:
            ab = prep["ab"][i]
            u, q, k, v, qi, kiwi, _, _, _ = in_proj(x2, mods, grp, lw["g_mix"], ab["w_in"], ab["q_gain"],
                                                    ab["k_gain"], prep["mavg"], rope, n)
            u3 = jnp.swapaxes(u.reshape(batch, seq, A_WIDTH), 0, 1)
            y3, s_re, s_im = s5_mixer(u3, state_re[i].reshape(batch, N_STATES), state_im[i].reshape(batch, N_STATES),
                                      ab["disc"], ab["d_skip"], ab["w_glu"], ab["b_glu"], tt=seq)
            y_a = jnp.swapaxes(y3, 0, 1).reshape(n, A_WIDTH)
            o_st = attend_sample(stack_heads(q), stack_heads(qi), pad_q(kiwi), page_t(k), page_t(v),
                                 page_t(kiwi[:, :IDX_DIM]), cache_k_t, cache_v_t, cache_ki_t, page_table, i, seq)
            y_b = jnp.swapaxes(o_st.reshape(batch, N_HEADS, rows, HEAD_DIM), 1, 2)[:, :seq]
            x2, rout_t = out_proj(x2, y_a, y_b.reshape(n, 512).astype(BF16), mods, grp, ab["w_out"], *route_w)
            ks.append(k.reshape(batch, seq, N_KV_HEADS, HEAD_DIM))
            vs.append(v.reshape(batch, seq, N_KV_HEADS, HEAD_DIM))
            kis.append(kiwi[:, :IDX_DIM].reshape(batch, seq, IDX_DIM))
            sres.append(s_re.reshape(batch, SSM_GROUPS, SSM_STATE))
            sims.append(s_im.reshape(batch, SSM_GROUPS, SSM_STATE))
        else:
            c = prep["c"][i]
            xp = jnp.pad(x2.reshape(batch, seq, D_MODEL), ((0, 0), (0, rows - seq), (0, 0)))
            hist = jnp.pad(state_pool[i], ((0, 0), (HIST_ROWS - POOL_STATE, 0), (0, 0)))
            xo, st = pool_mixer(xp.reshape(batch * rows, D_MODEL), hist, grp_pool.mods(mod_all[layer]), grp_pool,
                                lw["g_mix"], c["w_pool"], c["pool_scale"], tiles_per_seq=1, hist_is_x=False,
                                pos_base=past_len, state_rows=HIST_ROWS + rows)
            x2 = xo.reshape(batch, rows, D_MODEL)[:, :seq].reshape(n, D_MODEL)
            pools.append(st[:, seq + HIST_ROWS - POOL_STATE:seq + HIST_ROWS, :])
            rout_t = None
        x2 = hier_moe(x2, mods, grp, *route_w, lw["w_gate"], lw["w_up"], lw["w_down"], rout_t=rout_t)
    return (x2.reshape(batch, seq, D_MODEL), jnp.stack(ks), jnp.stack(vs), jnp.stack(kis), jnp.stack(sres),
            jnp.stack(sims), jnp.stack(pools))


def kernel(x_prompt, x_sample, c_prompt, c_sample, cache_k, cache_v, cache_kidx, state_ssm_re, state_ssm_im,
           state_pool, page_table, w_ada, b_ada, g_mix, g_ffn, w_in, q_gain, k_gain, lam_re, lam_im, log_dt,
           ssm_b_re, ssm_b_im, ssm_c_re, ssm_c_im, ssm_d, w_glu, b_glu, w_out, w_pool, pool_scale, w_coarse,
           b_coarse, w_fine, b_fine, w_gate, w_up, w_down):
    weights = dict(w_ada=w_ada, g_mix=g_mix, g_ffn=g_ffn, w_in=w_in, q_gain=q_gain, k_gain=k_gain, lam_re=lam_re,
                   lam_im=lam_im, log_dt=log_dt, ssm_b_re=ssm_b_re, ssm_b_im=ssm_b_im, ssm_c_re=ssm_c_re,
                   ssm_c_im=ssm_c_im, ssm_d=ssm_d, w_glu=w_glu, b_glu=b_glu, w_out=w_out, w_pool=w_pool,
                   pool_scale=pool_scale, w_coarse=w_coarse, b_coarse=b_coarse, w_fine=w_fine, b_fine=b_fine,
                   w_gate=w_gate, w_up=w_up, w_down=w_down)
    prep = _prep_weights(weights)
    n_prompt = c_prompt.shape[0]
    mod_all = ada_params_all(jnp.concatenate([c_prompt, c_sample], axis=0), w_ada, b_ada)
    out_p = _run_prompt(x_prompt, mod_all[:, :n_prompt], prep)
    out_s = _run_sample(x_sample, mod_all[:, n_prompt:], prep, cache_k, cache_v, cache_kidx, state_ssm_re,
                        state_ssm_im, state_pool, page_table)
    return (out_p[0], out_s[0]) + out_p[1:] + out_s[1:]
```
